```python
import jax, jax.numpy as jnp
from jax import lax
import numpy as np

D_MODEL = 2048
BATCH = 8
SEQ = 2048
DEPTH = 1

ATTN_HEAD_DIM = 64
N_Q_HEADS = D_MODEL // 128
N_KV_HEADS = N_Q_HEADS // 4
Q_PER_KV = N_Q_HEADS // N_KV_HEADS
WINDOW = 128
ATTN_BLOCK = WINDOW
ATTN_WIDTH = N_Q_HEADS * ATTN_HEAD_DIM
KV_WIDTH = N_KV_HEADS * ATTN_HEAD_DIM
HGRN_HEAD_DIM = 128
HGRN_HEADS = D_MODEL // 256
HGRN_WIDTH = HGRN_HEADS * HGRN_HEAD_DIM
HGRN_CHUNK = 64
N_BRANCHES = 2
IN_WIDTH = ATTN_WIDTH + 2 * KV_WIDTH + 4 * HGRN_WIDTH + N_BRANCHES * D_MODEL
N_GROUPS = 4
EXPERTS_PER_GROUP = 8
N_EXPERTS = N_GROUPS * EXPERTS_PER_GROUP
TOP_K = 2
D_EXPERT = D_MODEL // 4
MOE_BLOCK = 128
EPS = 1e-6

kernel_name = "hybrid_swa_sink_hgrn2_hmoe_block"


def rmsnorm(x, g):
    x32 = x.astype(jnp.float32)
    y = x32 * lax.rsqrt(jnp.mean(x32 * x32, axis=-1, keepdims=True) + EPS)
    return (y * g.astype(jnp.float32)).astype(x.dtype)


def sliding_window_sink_attention(q, k, v, sinks):
    B, S, _ = q.shape
    nb = S // ATTN_BLOCK
    q = q.reshape(B, nb, ATTN_BLOCK, N_KV_HEADS, Q_PER_KV, ATTN_HEAD_DIM)
    k = k.reshape(B, nb, ATTN_BLOCK, N_KV_HEADS, ATTN_HEAD_DIM)
    v = v.reshape(B, nb, ATTN_BLOCK, N_KV_HEADS, ATTN_HEAD_DIM)

    def band(t):
        prev = jnp.concatenate([jnp.zeros_like(t[:, :1]), t[:, :-1]], axis=1)
        return jnp.concatenate([prev, t], axis=2)

    kb, vb = band(k), band(v)
    scores = jnp.einsum('bnqhgd,bnkhd->bnhgqk', q, kb).astype(jnp.float32) * (ATTN_HEAD_DIM ** -0.5)
    qi = jnp.arange(ATTN_BLOCK)[:, None]
    kj = jnp.arange(2 * ATTN_BLOCK)[None, :]
    rel = ATTN_BLOCK + qi - kj
    in_window = (rel >= 0) & (rel < WINDOW)
    key_exists = (jnp.arange(nb) > 0)[:, None, None] | (kj >= ATTN_BLOCK)[None]
    mask = in_window[None] & key_exists
    scores = jnp.where(mask[None, :, None, None], scores, -jnp.inf)
    sink = sinks.astype(jnp.float32).reshape(N_KV_HEADS, Q_PER_KV)[None, None, :, :, None, None]
    sink = jnp.broadcast_to(sink, scores.shape[:-1] + (1,))
    probs = jax.nn.softmax(jnp.concatenate([scores, sink], axis=-1), axis=-1)[..., :-1]
    out = jnp.einsum('bnhgqk,bnkhd->bnqhgd', probs.astype(v.dtype), vb)
    return out.reshape(B, S, ATTN_WIDTH)


def hgrn2_recurrence(q, f_logit, i, lower_bound):
    B, S, _ = q.shape
    nc = S // HGRN_CHUNK

    def chunks(t):
        t = t.reshape(B, nc, HGRN_CHUNK, HGRN_HEADS, HGRN_HEAD_DIM)
        return t.transpose(1, 0, 3, 2, 4).astype(jnp.float32)

    lb = lower_bound.astype(jnp.float32).reshape(HGRN_HEADS, 1, HGRN_HEAD_DIM)
    f = lb + (1.0 - lb) * jax.nn.sigmoid(chunks(f_logit))
    qh = jax.nn.silu(chunks(q)) * (HGRN_HEAD_DIM ** -0.5)
    kh = 1.0 - f
    vh = chunks(i)
    gh = jnp.log(f)
    causal = jnp.tril(jnp.ones((HGRN_CHUNK, HGRN_CHUNK), dtype=bool))

    def step(state, inp):
        qc, kc, vc, gc = inp
        b = jnp.cumsum(gc, axis=-2)
        diff = b[:, :, :, None, :] - b[:, :, None, :, :]
        decay = jnp.exp(jnp.where(causal[None, None, :, :, None], diff, -jnp.inf))
        attn = jnp.einsum('bhtk,bhsk,bhtsk->bhts', qc, kc, decay)
        o = jnp.einsum('bhts,bhsv->bhtv', attn, vc) + jnp.einsum('bhtk,bhkv->bhtv', qc * jnp.exp(b), state)
        b_last = b[:, :, -1:, :]
        new_state = jnp.exp(b_last[:, :, 0, :])[..., None] * state + \
            jnp.einsum('bhsk,bhsv->bhkv', kc * jnp.exp(b_last - b), vc)
        return new_state, o

    s0 = jnp.zeros((B, HGRN_HEADS, HGRN_HEAD_DIM, HGRN_HEAD_DIM), jnp.float32)
    _, o = lax.scan(step, s0, (qh, kh, vh, gh))
    return o.transpose(1, 0, 3, 2, 4).reshape(B, S, HGRN_HEADS, HGRN_HEAD_DIM)


def hierarchical_moe(xn, w_rg, b_rg, w_re, b_re, w1, w3, w2):
    B, S, D = xn.shape
    T = B * S
    xf = xn.reshape(T, D)
    p_group = jax.nn.softmax((xf @ w_rg).astype(jnp.float32) + b_rg.astype(jnp.float32), axis=-1)
    group = jnp.argmax(p_group, axis=-1)
    p_sel_group = jnp.take_along_axis(p_group, group[:, None], axis=-1)
    e_logits = ((xf @ w_re).astype(jnp.float32) + b_re.astype(jnp.float32)).reshape(T, N_GROUPS, EXPERTS_PER_GROUP)
    e_logits = jnp.take_along_axis(e_logits, group[:, None, None], axis=1)[:, 0]
    top_p, top_i = lax.top_k(jax.nn.softmax(e_logits, axis=-1), TOP_K)
    weights = p_sel_group * top_p / jnp.sum(top_p, axis=-1, keepdims=True)
    expert = group[:, None] * EXPERTS_PER_GROUP + top_i

    n_assign = T * TOP_K
    flat_e = expert.reshape(-1)
    flat_w = weights.reshape(-1)
    flat_tok = jnp.arange(n_assign) // TOP_K
    order = jnp.argsort(flat_e, stable=True)
    sorted_e = flat_e[order]
    sorted_tok = flat_tok[order]
    counts = jnp.bincount(flat_e, length=N_EXPERTS)
    padded = ((counts + MOE_BLOCK - 1) // MOE_BLOCK) * MOE_BLOCK
    pad_end = jnp.cumsum(padded)
    pad_start = pad_end - padded
    start = jnp.cumsum(counts) - counts
    dest = pad_start[sorted_e] + (jnp.arange(n_assign) - start[sorted_e])
    n_blocks = -(-n_assign // MOE_BLOCK) + N_EXPERTS
    block_expert = jnp.clip(jnp.searchsorted(pad_end, jnp.arange(n_blocks) * MOE_BLOCK, side='right'), 0, N_EXPERTS - 1)
    rows = jnp.zeros((n_blocks * MOE_BLOCK, D), xf.dtype).at[dest].set(xf[sorted_tok])

    def expert_block(args):
        xb, e = args
        hb = jax.nn.silu(xb @ w1[e]) * (xb @ w3[e])
        return hb @ w2[e]

    y_rows = lax.map(expert_block, (rows.reshape(n_blocks, MOE_BLOCK, D), block_expert)).reshape(-1, D)
    contrib = y_rows[dest].astype(jnp.float32) * flat_w[order][:, None]
    out = jnp.zeros((T, D), jnp.float32).at[sorted_tok].add(contrib)
    return out.reshape(B, S, D).astype(xn.dtype)


def setup_inputs(seed: int = 0) -> dict:
    key = jax.random.key(seed)
    ks = jax.random.split(key, 18)
    f32 = jnp.float32
    nrm = lambda k, shape, s: jax.random.normal(k, shape, f32) * s
    return {
        "x": nrm(ks[0], (BATCH, SEQ, D_MODEL), 1.0),
        "w_in": nrm(ks[1], (DEPTH, D_MODEL, IN_WIDTH), D_MODEL ** -0.5),
        "attn_sinks": nrm(ks[2], (DEPTH, N_Q_HEADS), 1.0),
        "hgrn_lb_logits": nrm(ks[3], (DEPTH + 1, HGRN_WIDTH), 0.5),
        "hgrn_norm_g": 1.0 + nrm(ks[4], (DEPTH, HGRN_HEAD_DIM), 0.02),
        "w_br_attn": nrm(ks[5], (DEPTH, ATTN_WIDTH, D_MODEL), ATTN_WIDTH ** -0.5),
        "w_br_hgrn": nrm(ks[6], (DEPTH, HGRN_WIDTH, D_MODEL), HGRN_WIDTH ** -0.5),
        "w_out": nrm(ks[7], (DEPTH, D_MODEL, D_MODEL), D_MODEL ** -0.5),
        "mix_norm_g": 1.0 + nrm(ks[8], (DEPTH, D_MODEL), 0.02),
        "moe_norm_g": 1.0 + nrm(ks[9], (DEPTH, D_MODEL), 0.02),
        "w_router_group": nrm(ks[10], (DEPTH, D_MODEL, N_GROUPS), D_MODEL ** -0.5),
        "b_router_group": nrm(ks[11], (DEPTH, N_GROUPS), 0.01),
        "w_router_expert": nrm(ks[12], (DEPTH, D_MODEL, N_EXPERTS), D_MODEL ** -0.5),
        "b_router_expert": nrm(ks[13], (DEPTH, N_EXPERTS), 0.01),
        "w1": nrm(ks[14], (DEPTH, N_EXPERTS, D_MODEL, D_EXPERT), D_MODEL ** -0.5),
        "w3": nrm(ks[15], (DEPTH, N_EXPERTS, D_MODEL, D_EXPERT), D_MODEL ** -0.5),
        "w2": nrm(ks[16], (DEPTH, N_EXPERTS, D_EXPERT, D_MODEL), D_EXPERT ** -0.5),
        "final_norm_g": 1.0 + nrm(ks[17], (D_MODEL,), 0.02),
    }


def reference(x, w_in, attn_sinks, hgrn_lb_logits, hgrn_norm_g, w_br_attn, w_br_hgrn, w_out,
              mix_norm_g, moe_norm_g, w_router_group, b_router_group, w_router_expert,
              b_router_expert, w1, w3, w2, final_norm_g):
    B, S, _ = x.shape
    widths = [ATTN_WIDTH, KV_WIDTH, KV_WIDTH, HGRN_WIDTH, HGRN_WIDTH, HGRN_WIDTH, HGRN_WIDTH, N_BRANCHES * D_MODEL]
    splits = [int(c) for c in np.cumsum(widths)[:-1]]
    lb_table = jnp.cumsum(jax.nn.softmax(hgrn_lb_logits.astype(jnp.float32), axis=0), axis=0)
    h = x
    for l in range(DEPTH):
        xn = rmsnorm(h, mix_norm_g[l])
        proj = xn @ w_in[l]
        q_a, k_a, v_a, q_h, f_h, i_h, og_h, gate_logits = jnp.split(proj, splits, axis=-1)
        o_a = sliding_window_sink_attention(q_a, k_a, v_a, attn_sinks[l])
        o_h = hgrn2_recurrence(q_h, f_h, i_h, lb_table[l])
        o_h = rmsnorm(o_h, hgrn_norm_g[l]) * jax.nn.silu(og_h.astype(jnp.float32).reshape(B, S, HGRN_HEADS, HGRN_HEAD_DIM))
        o_h = o_h.reshape(B, S, HGRN_WIDTH).astype(h.dtype)
        gates = jax.nn.sigmoid(gate_logits.astype(jnp.float32)).reshape(B, S, N_BRANCHES, D_MODEL)
        y_a = o_a @ w_br_attn[l]
        y_h = o_h @ w_br_hgrn[l]
        merged = (gates[:, :, 0] * y_a + gates[:, :, 1] * y_h).astype(h.dtype)
        h = h + merged @ w_out[l]
        hn = rmsnorm(h, moe_norm_g[l])
        h = h + hierarchical_moe(hn, w_router_group[l], b_router_group[l], w_router_expert[l],
                                 b_router_expert[l], w1[l], w3[l], w2[l])
    return rmsnorm(h, final_norm_g)
```

```python
import functools

import jax
import jax.numpy as jnp
from jax import lax
from jax.experimental import pallas as pl
from jax.experimental.pallas import tpu as pltpu

F32 = jnp.float32
BF16 = jnp.bfloat16

EPS = 1e-6
ATTN_HEAD_DIM = 64
Q_PER_KV = 4
N_KV_HEADS = 4
ATTN_BLOCK = 128
HGRN_HEAD_DIM = 128
HGRN_CHUNK = 64
N_GROUPS = 4
EXPERTS_PER_GROUP = 8
N_EXPERTS = N_GROUPS * EXPERTS_PER_GROUP
LANES = 128
ROUTE_LANES = LANES
EXPERT_LANE0 = N_GROUPS
MOE_ROWS = 256
VMEM_LIMIT = 56 * 1024 * 1024


def _sigmoid(x):
    return 1.0 / (1.0 + jnp.exp(-x))


def _cparams(sem, vmem=VMEM_LIMIT):
    return pltpu.CompilerParams(dimension_semantics=sem, vmem_limit_bytes=vmem)


def _inproj_kernel(x_ref, g_ref, w_ref, o_ref, xn_ref):
    @pl.when(pl.program_id(1) == 0)
    def _():
        x = x_ref[...]
        ms = jnp.mean(x * x, axis=-1, keepdims=True)
        xn_ref[...] = (x * lax.rsqrt(ms + EPS) * g_ref[...]).astype(BF16)

    o_ref[...] = jnp.dot(xn_ref[...], w_ref[...], preferred_element_type=F32).astype(o_ref.dtype)


def _in_proj(x2, g, w, tm, tn):
    t, d = x2.shape
    n = w.shape[1]
    return pl.pallas_call(
        _inproj_kernel,
        grid=(t // tm, n // tn),
        in_specs=[
            pl.BlockSpec((tm, d), lambda i, j: (i, 0)),
            pl.BlockSpec((1, d), lambda i, j: (0, 0)),
            pl.BlockSpec((d, tn), lambda i, j: (0, j)),
        ],
        out_specs=pl.BlockSpec((tm, tn), lambda i, j: (i, j)),
        out_shape=jax.ShapeDtypeStruct((t, n), BF16),
        scratch_shapes=[pltpu.VMEM((tm, d), BF16)],
        compiler_params=_cparams(("arbitrary", "arbitrary")),
        name="in_proj",
    )(x2, g, w)


def _attn_kernel(sink_ref, q_ref, kc_ref, kp_ref, vc_ref, vp_ref, o_ref):
    n = pl.program_id(1)
    blk, dh, g = ATTN_BLOCK, ATTN_HEAD_DIM, Q_PER_KV
    rows = g * blk
    qi = lax.broadcasted_iota(jnp.int32, (rows, blk), 0) % blk
    kj = lax.broadcasted_iota(jnp.int32, (rows, blk), 1)
    mask_cur = kj <= qi
    mask_prev = jnp.logical_and(kj > qi, n > 0)
    head_of_row = lax.broadcasted_iota(jnp.int32, (rows, 1), 0) // blk
    nt = (((1,), (1,)), ((), ()))
    scale = dh ** -0.5
    for h in range(N_KV_HEADS):
        kc = kc_ref[:, h * dh:(h + 1) * dh]
        kp = kp_ref[:, h * dh:(h + 1) * dh]
        vc = vc_ref[:, h * dh:(h + 1) * dh]
        vp = vp_ref[:, h * dh:(h + 1) * dh]
        q4 = jnp.concatenate(
            [q_ref[:, (h * g + j) * dh:(h * g + j + 1) * dh] for j in range(g)], axis=0)
        sc = lax.dot_general(q4, kc, nt, preferred_element_type=F32) * scale
        sp = lax.dot_general(q4, kp, nt, preferred_element_type=F32) * scale
        sc = jnp.where(mask_cur, sc, -jnp.inf)
        sp = jnp.where(mask_prev, sp, -jnp.inf)
        sink = jnp.zeros((rows, 1), F32)
        for j in range(g):
            sink = jnp.where(head_of_row == j, sink_ref[h * g + j], sink)
        m = jnp.maximum(jnp.maximum(jnp.max(sc, axis=-1, keepdims=True),
                                    jnp.max(sp, axis=-1, keepdims=True)), sink)
        pc = jnp.exp(sc - m)
        pp = jnp.exp(sp - m)
        den = (jnp.sum(pc, axis=-1, keepdims=True) + jnp.sum(pp, axis=-1, keepdims=True)
               + jnp.exp(sink - m))
        o = (jnp.dot(pc.astype(BF16), vc, preferred_element_type=F32)
             + jnp.dot(pp.astype(BF16), vp, preferred_element_type=F32)) / den
        for j in range(g):
            o_ref[:, (h * g + j) * dh:(h * g + j + 1) * dh] = (
                o[j * blk:(j + 1) * blk, :].astype(o_ref.dtype))


def _attention(proj3, sinks, off_q, off_k, off_v, attn_w, kv_w):
    b, s, _ = proj3.shape
    blk = ATTN_BLOCK
    qb, kb, vb = off_q // attn_w, off_k // kv_w, off_v // kv_w
    prev = lambda n: jnp.maximum(n - 1, 0)
    return pl.pallas_call(
        _attn_kernel,
        grid=(b, s // blk),
        in_specs=[
            pl.BlockSpec(memory_space=pltpu.SMEM),
            pl.BlockSpec((None, blk, attn_w), lambda i, n: (i, n, qb)),
            pl.BlockSpec((None, blk, kv_w), lambda i, n: (i, n, kb)),
            pl.BlockSpec((None, blk, kv_w), lambda i, n: (i, prev(n), kb)),
            pl.BlockSpec((None, blk, kv_w), lambda i, n: (i, n, vb)),
            pl.BlockSpec((None, blk, kv_w), lambda i, n: (i, prev(n), vb)),
        ],
        out_specs=pl.BlockSpec((None, blk, attn_w), lambda i, n: (i, n, 0)),
        out_shape=jax.ShapeDtypeStruct((b, s, attn_w), BF16),
        compiler_params=_cparams(("arbitrary", "arbitrary")),
        name="swa_attention",
    )(sinks, proj3, proj3, proj3, proj3, proj3)


def _hgrn_kernel(q_ref, f_ref, i_ref, og_ref, lb_ref, gn_ref, o_ref, st_ref, *, n_heads, n_chunks):
    c, dk = HGRN_CHUNK, HGRN_HEAD_DIM

    @pl.when(pl.program_id(1) == 0)
    def _():
        st_ref[...] = jnp.zeros_like(st_ref)

    ti = lax.broadcasted_iota(jnp.int32, (c, c), 0)
    si = lax.broadcasted_iota(jnp.int32, (c, c), 1)
    causal = si <= ti
    tri = causal.astype(BF16)
    nt = (((1,), (1,)), ((), ()))
    tn = (((0,), (0,)), ((), ()))
    qscale = dk ** -0.5

    def chunk(ci, carry):
        rows = pl.ds(pl.multiple_of(ci * c, c), c)
        for h in range(n_heads):
            cols = slice(h * dk, (h + 1) * dk)
            lb = lb_ref[:, cols]
            f = lb + (1.0 - lb) * _sigmoid(f_ref[rows, cols].astype(F32))
            gl = jnp.log(f)
            g_hi = gl.astype(BF16)
            g_lo = (gl - g_hi.astype(F32)).astype(BF16)
            bc = (jnp.dot(tri, g_hi, preferred_element_type=F32)
                  + jnp.dot(tri, g_lo, preferred_element_type=F32))
            b_last = bc[c - 1:c, :]
            r = 0.5 * b_last
            qv = q_ref[rows, cols].astype(F32)
            qs = qv * _sigmoid(qv) * qscale
            kk = 1.0 - f
            vv = i_ref[rows, cols]
            qt = (qs * jnp.exp(bc - r)).astype(BF16)
            kt = (kk * jnp.exp(r - bc)).astype(BF16)
            a = lax.dot_general(qt, kt, nt, preferred_element_type=F32)
            a = jnp.where(causal, a, 0.0).astype(BF16)
            o = jnp.dot(a, vv, preferred_element_type=F32)
            st = st_ref[h]
            qe = (qs * jnp.exp(bc)).astype(BF16)
            o = o + lax.dot_general(qe, st.astype(BF16), nt, preferred_element_type=F32)
            kd = (kk * jnp.exp(b_last - bc)).astype(BF16)
            st_ref[h] = st * jnp.exp(b_last) + lax.dot_general(
                vv, kd, tn, preferred_element_type=F32)
            ms = jnp.mean(o * o, axis=-1, keepdims=True)
            on = o * lax.rsqrt(ms + EPS) * gn_ref[...]
            ogv = og_ref[rows, cols].astype(F32)
            o_ref[rows, cols] = (on * (ogv * _sigmoid(ogv))).astype(o_ref.dtype)
        return carry

    lax.fori_loop(0, n_chunks, chunk, 0)


def _hgrn(proj3, lb, gn, off_q, off_f, off_i, off_og, width, ts):
    b, s, _ = proj3.shape
    n_heads = width // HGRN_HEAD_DIM
    spec = lambda off: pl.BlockSpec((None, ts, width), lambda i, t: (i, t, off // width))
    return pl.pallas_call(
        functools.partial(_hgrn_kernel, n_heads=n_heads, n_chunks=ts // HGRN_CHUNK),
        grid=(b, s // ts),
        in_specs=[
            spec(off_q), spec(off_f), spec(off_i), spec(off_og),
            pl.BlockSpec((1, width), lambda i, t: (0, 0)),
            pl.BlockSpec((1, HGRN_HEAD_DIM), lambda i, t: (0, 0)),
        ],
        out_specs=pl.BlockSpec((None, ts, width), lambda i, t: (i, t, 0)),
        out_shape=jax.ShapeDtypeStruct((b, s, width), BF16),
        scratch_shapes=[pltpu.VMEM((n_heads, HGRN_HEAD_DIM, HGRN_HEAD_DIM), F32)],
        compiler_params=_cparams(("arbitrary", "arbitrary")),
        name="hgrn2",
    )(proj3, proj3, proj3, proj3, lb, gn)


def _merge_kernel(oa_ref, oh_ref, g0_ref, g1_ref, x_ref, wa_ref, wh_ref, wo_ref, gm_ref,
                  wrh_ref, wrl_ref, br_ref, h_ref, hn_ref, route_ref, cnt_ref):
    tm = x_ref.shape[0]

    @pl.when(pl.program_id(0) == 0)
    def _():
        cnt_ref[...] = jnp.zeros_like(cnt_ref)

    ya = jnp.dot(oa_ref[...], wa_ref[...], preferred_element_type=F32)
    yh = jnp.dot(oh_ref[...], wh_ref[...], preferred_element_type=F32)
    merged = (_sigmoid(g0_ref[...].astype(F32)) * ya
              + _sigmoid(g1_ref[...].astype(F32)) * yh).astype(BF16)
    h = x_ref[...] + jnp.dot(merged, wo_ref[...], preferred_element_type=F32)
    h_ref[...] = h
    ms = jnp.mean(h * h, axis=-1, keepdims=True)
    hn = h * lax.rsqrt(ms + EPS) * gm_ref[...]
    hn_ref[...] = hn

    hn_hi = hn.astype(BF16)
    hn_lo = (hn - hn_hi.astype(F32)).astype(BF16)
    logits = (jnp.dot(hn_hi, wrh_ref[...], preferred_element_type=F32)
              + jnp.dot(hn_hi, wrl_ref[...], preferred_element_type=F32)
              + jnp.dot(hn_lo, wrh_ref[...], preferred_element_type=F32)
              + br_ref[...])
    lane = lax.broadcasted_iota(jnp.int32, (tm, ROUTE_LANES), 1)
    neg = -jnp.inf

    def first_argmax(v):
        m = jnp.max(v, axis=-1, keepdims=True)
        idx = jnp.min(jnp.where(v == m, lane, ROUTE_LANES), axis=-1, keepdims=True)
        return m, idx

    is_group = lane < N_GROUPS
    gmax, gidx = first_argmax(jnp.where(is_group, logits, neg))
    p_sel = 1.0 / jnp.sum(jnp.where(is_group, jnp.exp(logits - gmax), 0.0), axis=-1, keepdims=True)
    eidx = lane - EXPERT_LANE0
    in_group = jnp.logical_and(
        jnp.logical_and(eidx >= 0, eidx < N_EXPERTS), (eidx // EXPERTS_PER_GROUP) == gidx)
    el = jnp.where(in_group, logits, neg)
    m1, i1 = first_argmax(el)
    m2, i2 = first_argmax(jnp.where(lane == i1, neg, el))
    t = jnp.exp(m2 - m1)
    w1 = p_sel / (1.0 + t)
    w2 = p_sel * t / (1.0 + t)

    sel1 = lane == i1
    sel2 = lane == i2
    onehot = jnp.logical_or(sel1, sel2).astype(BF16)
    ri = lax.broadcasted_iota(jnp.int32, (tm, tm), 0)
    ci = lax.broadcasted_iota(jnp.int32, (tm, tm), 1)
    before = (ci < ri).astype(BF16)
    cum = jnp.dot(before, onehot, preferred_element_type=F32) + cnt_ref[...]
    r1 = jnp.sum(jnp.where(sel1, cum, 0.0), axis=-1, keepdims=True)
    r2 = jnp.sum(jnp.where(sel2, cum, 0.0), axis=-1, keepdims=True)
    cnt_ref[...] += jnp.sum(onehot.astype(F32), axis=0, keepdims=True)

    e1 = (i1 - EXPERT_LANE0).astype(F32)
    e2 = (i2 - EXPERT_LANE0).astype(F32)
    out = jnp.zeros((tm, ROUTE_LANES), F32)
    for k, v in enumerate((e1, e2, w1, w2, r1, r2)):
        out = jnp.where(lane == k, v, out)
    route_ref[...] = out


def _merge_route(o_a, o_h, proj, x2, wa, wh, wo, gm, wr_hi, wr_lo, br, off_g0, off_g1, tm):
    t, d = x2.shape
    aw, hw = o_a.shape[1], o_h.shape[1]
    const = lambda shape: pl.BlockSpec(shape, lambda i: (0, 0), pipeline_mode=pl.Buffered(1))
    row = lambda w: pl.BlockSpec((tm, w), lambda i: (i, 0))
    return pl.pallas_call(
        _merge_kernel,
        grid=(t // tm,),
        in_specs=[
            row(aw), row(hw),
            pl.BlockSpec((tm, d), lambda i: (i, off_g0 // d)),
            pl.BlockSpec((tm, d), lambda i: (i, off_g1 // d)),
            row(d),
            const((aw, d)), const((hw, d)), const((d, d)), const((1, d)),
            const((d, ROUTE_LANES)), const((d, ROUTE_LANES)), const((1, ROUTE_LANES)),
        ],
        out_specs=[row(d), row(d), row(ROUTE_LANES),
                   pl.BlockSpec((1, ROUTE_LANES), lambda i: (0, 0))],
        out_shape=[
            jax.ShapeDtypeStruct((t, d), F32),
            jax.ShapeDtypeStruct((t, d), F32),
            jax.ShapeDtypeStruct((t, ROUTE_LANES), F32),
            jax.ShapeDtypeStruct((1, ROUTE_LANES), F32),
        ],
        compiler_params=_cparams(("arbitrary",)),
        name="merge_route",
    )(o_a, o_h, proj, proj, x2, wa, wh, wo, gm, wr_hi, wr_lo, br)


def _dispatch_kernel(fill_ref, nfill_ref, dest_ref, hn_ref, rows_ref, zbuf, sem):
    tm = hn_ref.shape[0]
    bm = zbuf.shape[0]

    @pl.when(pl.program_id(0) == 0)
    def _():
        zbuf[...] = jnp.zeros_like(zbuf)

        def fill(j, carry):
            start = pl.multiple_of(fill_ref[j], bm)
            pltpu.make_async_copy(zbuf, rows_ref.at[pl.ds(start, bm)], sem).start()
            return carry

        def drain(j, carry):
            pltpu.make_async_copy(zbuf, rows_ref.at[pl.ds(0, bm)], sem).wait()
            return carry

        lax.fori_loop(0, nfill_ref[0], fill, 0)
        lax.fori_loop(0, nfill_ref[0], drain, 0)

    def row_copy(r, k):
        return pltpu.make_async_copy(
            hn_ref.at[pl.ds(r, 1)], rows_ref.at[pl.ds(dest_ref[0, k * tm + r], 1)], sem)

    def issue(r, carry):
        row_copy(r, 0).start()
        row_copy(r, 1).start()
        return carry

    lax.fori_loop(0, tm, issue, 0)
    for _ in range(2):
        pltpu.make_async_copy(hn_ref, rows_ref.at[pl.ds(0, tm)], sem).wait()


def _dispatch(hn, dest, fill, nfill, n_rows, tm, bm):
    t, d = hn.shape
    grid_spec = pltpu.PrefetchScalarGridSpec(
        num_scalar_prefetch=2,
        grid=(t // tm,),
        in_specs=[
            pl.BlockSpec((None, 1, 2 * tm), lambda i, f, n: (i, 0, 0), memory_space=pltpu.SMEM),
            pl.BlockSpec((tm, d), lambda i, f, n: (i, 0)),
        ],
        out_specs=pl.BlockSpec(memory_space=pl.ANY),
        scratch_shapes=[pltpu.VMEM((bm, d), hn.dtype), pltpu.SemaphoreType.DMA],
    )
    return pl.pallas_call(
        _dispatch_kernel,
        grid_spec=grid_spec,
        out_shape=jax.ShapeDtypeStruct((n_rows, d), hn.dtype),
        compiler_params=_cparams(("arbitrary",)),
        name="moe_dispatch",
    )(fill, nfill, dest, hn)


def _expert_kernel(be_ref, meta_ref, x_ref, w1_ref, w3_ref, w2_ref, y_ref, w1b, w3b, w2b):
    i = pl.program_id(0)
    new_expert = jnp.logical_or(i == 0, be_ref[i] != be_ref[jnp.maximum(i - 1, 0)])

    @pl.when(jnp.logical_and(i < meta_ref[0], new_expert))
    def _():
        w1b[...] = w1_ref[...].astype(BF16)
        w3b[...] = w3_ref[...].astype(BF16)
        w2b[...] = w2_ref[...].astype(BF16)

    @pl.when(i < meta_ref[0])
    def _():
        x = x_ref[...].astype(BF16)
        h1 = jnp.dot(x, w1b[...], preferred_element_type=F32)
        h3 = jnp.dot(x, w3b[...], preferred_element_type=F32)
        hb = (h1 * _sigmoid(h1) * h3).astype(BF16)
        y_ref[...] = jnp.dot(hb, w2b[...], preferred_element_type=F32)

    @pl.when(i >= meta_ref[0])
    def _():
        y_ref[...] = jnp.zeros_like(y_ref)


def _experts(rows, block_expert, meta, w1, w3, w2, bm):
    n_rows, d = rows.shape
    de = w1.shape[-1]
    n_blocks = n_rows // bm
    blk = lambda i, be, meta: (jnp.minimum(i, meta[0] - 1), 0)
    out_blk = lambda i, be, meta: (i, 0)
    grid_spec = pltpu.PrefetchScalarGridSpec(
        num_scalar_prefetch=2,
        grid=(n_blocks,),
        in_specs=[
            pl.BlockSpec((bm, d), blk),
            pl.BlockSpec((None, d, de), lambda i, be, meta: (be[i], 0, 0)),
            pl.BlockSpec((None, d, de), lambda i, be, meta: (be[i], 0, 0)),
            pl.BlockSpec((None, de, d), lambda i, be, meta: (be[i], 0, 0)),
        ],
        out_specs=pl.BlockSpec((bm, d), out_blk),
        scratch_shapes=[pltpu.VMEM((d, de), BF16), pltpu.VMEM((d, de), BF16),
                        pltpu.VMEM((de, d), BF16)],
    )
    return pl.pallas_call(
        _expert_kernel,
        grid_spec=grid_spec,
        out_shape=jax.ShapeDtypeStruct((n_rows, d), F32),
        compiler_params=_cparams(("arbitrary",)),
        name="moe_experts",
    )(block_expert, meta, rows, w1, w3, w2)


def _combine_kernel(dcur_ref, dnext_ref, h_ref, route_ref, g_ref, y_ref, o_ref, ybuf, sems):
    i = pl.program_id(0)
    nt = pl.num_programs(0)
    tm = h_ref.shape[0]
    slot = i % 2

    def gather(d_ref, s):
        def issue(r, carry):
            for k in range(2):
                pltpu.make_async_copy(
                    y_ref.at[pl.ds(d_ref[0, k * tm + r], 1)],
                    ybuf.at[s, k, pl.ds(r, 1)], sems.at[s]).start()
            return carry
        lax.fori_loop(0, tm, issue, 0)

    @pl.when(i == 0)
    def _():
        gather(dcur_ref, slot)

    @pl.when(i + 1 < nt)
    def _():
        gather(dnext_ref, 1 - slot)

    for k in range(2):
        pltpu.make_async_copy(y_ref.at[pl.ds(0, tm)], ybuf.at[slot, k], sems.at[slot]).wait()

    route = route_ref[...]
    w1 = route[:, 2:3]
    w2 = route[:, 3:4]
    h = h_ref[...] + w1 * ybuf[slot, 0] + w2 * ybuf[slot, 1]
    ms = jnp.mean(h * h, axis=-1, keepdims=True)
    o_ref[...] = h * lax.rsqrt(ms + EPS) * g_ref[...]


def _combine(h, route, dest, y_rows, g, tm):
    t, d = h.shape
    nt = t // tm
    return pl.pallas_call(
        _combine_kernel,
        grid=(nt,),
        in_specs=[
            pl.BlockSpec((None, 1, 2 * tm), lambda i: (i, 0, 0), memory_space=pltpu.SMEM),
            pl.BlockSpec((None, 1, 2 * tm), lambda i: (jnp.minimum(i + 1, nt - 1), 0, 0),
                         memory_space=pltpu.SMEM),
            pl.BlockSpec((tm, d), lambda i: (i, 0)),
            pl.BlockSpec((tm, ROUTE_LANES), lambda i: (i, 0)),
            pl.BlockSpec((1, d), lambda i: (0, 0)),
            pl.BlockSpec(memory_space=pl.ANY),
        ],
        out_specs=pl.BlockSpec((tm, d), lambda i: (i, 0)),
        out_shape=jax.ShapeDtypeStruct((t, d), F32),
        scratch_shapes=[pltpu.VMEM((2, 2, tm, d), F32), pltpu.SemaphoreType.DMA((2,))],
        compiler_params=_cparams(("arbitrary",)),
        name="moe_combine",
    )(dest, dest, h, route, g, y_rows)


def kernel(x, w_in, attn_sinks, hgrn_lb_logits, hgrn_norm_g, w_br_attn, w_br_hgrn, w_out,
           mix_norm_g, moe_norm_g, w_router_group, b_router_group, w_router_expert,
           b_router_expert, w1, w3, w2, final_norm_g):
    b, s, d = x.shape
    t = b * s
    depth = w_in.shape[0]
    assert depth == 1
    attn_w = (d // 128) * ATTN_HEAD_DIM
    kv_w = attn_w // Q_PER_KV
    hg_w = (d // 256) * HGRN_HEAD_DIM
    assert kv_w == N_KV_HEADS * ATTN_HEAD_DIM

    src = {}
    off = 0
    for name, width in (("qa", attn_w), ("k", kv_w), ("v", kv_w), ("qh", hg_w), ("f", hg_w),
                        ("i", hg_w), ("og", hg_w), ("g0", d), ("g1", d)):
        src[name] = (off, width)
        off += width
    order = ("g0", "g1", "qa", "qh", "f", "i", "og", "k", "v")
    dst = {}
    off = 0
    for name in order:
        dst[name] = off
        off += src[name][1]
    n_in = off
    l = 0
    w_perm = jnp.concatenate(
        [w_in[l][:, src[n][0]:src[n][0] + src[n][1]] for n in order], axis=1).astype(BF16)

    lb = jnp.cumsum(jax.nn.softmax(hgrn_lb_logits.astype(F32), axis=0), axis=0)[l].reshape(1, hg_w)

    x2 = x.reshape(t, d)
    tm_in = min(1024, t)
    proj = _in_proj(x2, mix_norm_g[l].reshape(1, d), w_perm, tm_in, 512)
    proj3 = proj.reshape(b, s, n_in)

    o_a = _attention(proj3, attn_sinks[l].astype(F32), dst["qa"], dst["k"], dst["v"], attn_w, kv_w)
    o_h = _hgrn(proj3, lb, hgrn_norm_g[l].reshape(1, HGRN_HEAD_DIM).astype(F32),
                dst["qh"], dst["f"], dst["i"], dst["og"], hg_w, min(256, s))

    w_r = jnp.zeros((d, ROUTE_LANES), F32)
    w_r = w_r.at[:, :N_GROUPS].set(w_router_group[l])
    w_r = w_r.at[:, EXPERT_LANE0:EXPERT_LANE0 + N_EXPERTS].set(w_router_expert[l])
    wr_hi = w_r.astype(BF16)
    wr_lo = (w_r - wr_hi.astype(F32)).astype(BF16)
    b_r = jnp.zeros((1, ROUTE_LANES), F32)
    b_r = b_r.at[0, :N_GROUPS].set(b_router_group[l])
    b_r = b_r.at[0, EXPERT_LANE0:EXPERT_LANE0 + N_EXPERTS].set(b_router_expert[l])

    tm = min(256, t)
    h, hn, route, cnt = _merge_route(
        o_a.reshape(t, attn_w), o_h.reshape(t, hg_w), proj, x2,
        w_br_attn[l].astype(BF16), w_br_hgrn[l].astype(BF16), w_out[l].astype(BF16),
        moe_norm_g[l].reshape(1, d), wr_hi, wr_lo, b_r, dst["g0"], dst["g1"], tm)

    bm = MOE_ROWS
    counts = cnt[0, EXPERT_LANE0:EXPERT_LANE0 + N_EXPERTS].astype(jnp.int32)
    padded = ((counts + bm - 1) // bm) * bm
    pad_end = jnp.cumsum(padded)
    pad_start = pad_end - padded
    e1 = route[:, 0].astype(jnp.int32)
    e2 = route[:, 1].astype(jnp.int32)
    dest1 = pad_start[e1] + route[:, 4].astype(jnp.int32)
    dest2 = pad_start[e2] + route[:, 5].astype(jnp.int32)
    n_blocks = (2 * t) // bm + N_EXPERTS
    n_used = pad_end[-1] // bm
    blk_ids = jnp.minimum(jnp.arange(n_blocks, dtype=jnp.int32), n_used - 1)
    block_expert = jnp.minimum(
        jnp.sum((pad_end[None, :] <= (blk_ids * bm)[:, None]).astype(jnp.int32), axis=1),
        N_EXPERTS - 1)
    meta = jnp.stack([n_used, n_used]).astype(jnp.int32)
    nt = t // tm
    dest = jnp.concatenate([dest1.reshape(nt, 1, tm), dest2.reshape(nt, 1, tm)], axis=2)

    blk_all = jnp.arange(n_blocks, dtype=jnp.int32)
    fill = jnp.concatenate([
        jnp.maximum(pad_end - bm, 0).astype(jnp.int32),
        jnp.minimum(n_used + blk_all, n_blocks - 1) * bm])
    nfill = (N_EXPERTS + n_blocks - n_used).astype(jnp.int32).reshape(1)

    rows = _dispatch(hn, dest, fill, nfill, n_blocks * bm, tm, bm)
    y_rows = _experts(rows, block_expert, meta, w1[l], w3[l], w2[l], bm)
    out = _combine(h, route, dest, y_rows, final_norm_g.reshape(1, d), tm)
    return out.reshape(b, s, d)
```

```python
import functools

import jax
import jax.numpy as jnp
from jax import lax
from jax.experimental import pallas as pl
from jax.experimental.pallas import tpu as pltpu

F32 = jnp.float32
BF16 = jnp.bfloat16

EPS = 1e-6
ATTN_HEAD_DIM = 64
Q_PER_KV = 4
N_KV_HEADS = 4
ATTN_BLOCK = 128
HGRN_HEAD_DIM = 128
HGRN_CHUNK = 64
N_GROUPS = 4
EXPERTS_PER_GROUP = 8
N_EXPERTS = N_GROUPS * EXPERTS_PER_GROUP
LANES = 128
ROUTE_LANES = LANES
EXPERT_LANE0 = N_GROUPS
MOE_ROWS = 256
PLAN_ROWS = 8
PLAN_E, PLAN_R = 0, 4
VMEM_LIMIT = 56 * 1024 * 1024


def _sigmoid(x):
    return 1.0 / (1.0 + jnp.exp(-x))


def _cparams(sem, vmem=VMEM_LIMIT):
    return pltpu.CompilerParams(dimension_semantics=sem, vmem_limit_bytes=vmem)


def _to_row_tiles(v):
    m, d = v.shape
    return v.reshape(m, d // LANES, LANES).astype(BF16)


def _from_row_tiles(p):
    m, c, l = p.shape
    return p.reshape(m, c * l)


def _inproj_kernel(x_ref, g_ref, w_ref, o_ref, xn_ref):
    @pl.when(pl.program_id(1) == 0)
    def _():
        x = x_ref[...]
        ms = jnp.mean(x * x, axis=-1, keepdims=True)
        xn_ref[...] = (x * lax.rsqrt(ms + EPS) * g_ref[...]).astype(BF16)

    o_ref[...] = jnp.dot(xn_ref[...], w_ref[...], preferred_element_type=F32).astype(o_ref.dtype)


def _in_proj(x2, g, w, tm, tn):
    t, d = x2.shape
    n = w.shape[1]
    return pl.pallas_call(
        _inproj_kernel,
        grid=(t // tm, n // tn),
        in_specs=[
            pl.BlockSpec((tm, d), lambda i, j: (i, 0)),
            pl.BlockSpec((1, d), lambda i, j: (0, 0)),
            pl.BlockSpec((d, tn), lambda i, j: (0, j)),
        ],
        out_specs=pl.BlockSpec((tm, tn), lambda i, j: (i, j)),
        out_shape=jax.ShapeDtypeStruct((t, n), BF16),
        scratch_shapes=[pltpu.VMEM((tm, d), BF16)],
        compiler_params=_cparams(("arbitrary", "arbitrary")),
        name="in_proj",
    )(x2, g, w)


def _attn_kernel(sink_ref, q_ref, kc_ref, kp_ref, vc_ref, vp_ref, o_ref):
    n = pl.program_id(1)
    blk, dh, g = ATTN_BLOCK, ATTN_HEAD_DIM, Q_PER_KV
    rows = g * blk
    qi = lax.broadcasted_iota(jnp.int32, (rows, blk), 0) % blk
    kj = lax.broadcasted_iota(jnp.int32, (rows, blk), 1)
    mask_cur = kj <= qi
    mask_prev = jnp.logical_and(kj > qi, n > 0)
    head_of_row = lax.broadcasted_iota(jnp.int32, (rows, 1), 0) // blk
    nt = (((1,), (1,)), ((), ()))
    scale = dh ** -0.5
    for h in range(N_KV_HEADS):
        kc = kc_ref[:, h * dh:(h + 1) * dh]
        kp = kp_ref[:, h * dh:(h + 1) * dh]
        vc = vc_ref[:, h * dh:(h + 1) * dh]
        vp = vp_ref[:, h * dh:(h + 1) * dh]
        q4 = jnp.concatenate(
            [q_ref[:, (h * g + j) * dh:(h * g + j + 1) * dh] for j in range(g)], axis=0)
        sc = lax.dot_general(q4, kc, nt, preferred_element_type=F32) * scale
        sp = lax.dot_general(q4, kp, nt, preferred_element_type=F32) * scale
        sc = jnp.where(mask_cur, sc, -jnp.inf)
        sp = jnp.where(mask_prev, sp, -jnp.inf)
        sink = jnp.zeros((rows, 1), F32)
        for j in range(g):
            sink = jnp.where(head_of_row == j, sink_ref[h * g + j], sink)
        m = jnp.maximum(jnp.maximum(jnp.max(sc, axis=-1, keepdims=True),
                                    jnp.max(sp, axis=-1, keepdims=True)), sink)
        pc = jnp.exp(sc - m)
        pp = jnp.exp(sp - m)
        den = (jnp.sum(pc, axis=-1, keepdims=True) + jnp.sum(pp, axis=-1, keepdims=True)
               + jnp.exp(sink - m))
        o = (jnp.dot(pc.astype(BF16), vc, preferred_element_type=F32)
             + jnp.dot(pp.astype(BF16), vp, preferred_element_type=F32)) / den
        for j in range(g):
            o_ref[:, (h * g + j) * dh:(h * g + j + 1) * dh] = (
                o[j * blk:(j + 1) * blk, :].astype(o_ref.dtype))


def _attention(proj3, sinks, off_q, off_k, off_v, attn_w, kv_w):
    b, s, _ = proj3.shape
    blk = ATTN_BLOCK
    qb, kb, vb = off_q // attn_w, off_k // kv_w, off_v // kv_w
    prev = lambda n: jnp.maximum(n - 1, 0)
    return pl.pallas_call(
        _attn_kernel,
        grid=(b, s // blk),
        in_specs=[
            pl.BlockSpec(memory_space=pltpu.SMEM),
            pl.BlockSpec((None, blk, attn_w), lambda i, n: (i, n, qb)),
            pl.BlockSpec((None, blk, kv_w), lambda i, n: (i, n, kb)),
            pl.BlockSpec((None, blk, kv_w), lambda i, n: (i, prev(n), kb)),
            pl.BlockSpec((None, blk, kv_w), lambda i, n: (i, n, vb)),
            pl.BlockSpec((None, blk, kv_w), lambda i, n: (i, prev(n), vb)),
        ],
        out_specs=pl.BlockSpec((None, blk, attn_w), lambda i, n: (i, n, 0)),
        out_shape=jax.ShapeDtypeStruct((b, s, attn_w), BF16),
        compiler_params=_cparams(("arbitrary", "arbitrary")),
        name="swa_attention",
    )(sinks, proj3, proj3, proj3, proj3, proj3)


def _hgrn_kernel(q_ref, f_ref, i_ref, og_ref, lb_ref, gn_ref, o_ref, st_ref, *, n_heads, n_chunks):
    c, dk = HGRN_CHUNK, HGRN_HEAD_DIM

    @pl.when(pl.program_id(1) == 0)
    def _():
        st_ref[...] = jnp.zeros_like(st_ref)

    ti = lax.broadcasted_iota(jnp.int32, (c, c), 0)
    si = lax.broadcasted_iota(jnp.int32, (c, c), 1)
    causal = si <= ti
    tri = causal.astype(BF16)
    nt = (((1,), (1,)), ((), ()))
    tn = (((0,), (0,)), ((), ()))
    qscale = dk ** -0.5

    def chunk(ci, carry):
        rows = pl.ds(pl.multiple_of(ci * c, c), c)
        for h in range(n_heads):
            cols = slice(h * dk, (h + 1) * dk)
            lb = lb_ref[:, cols]
            f = lb + (1.0 - lb) * _sigmoid(f_ref[rows, cols].astype(F32))
            gl = jnp.log(f)
            g_hi = gl.astype(BF16)
            g_lo = (gl - g_hi.astype(F32)).astype(BF16)
            bc = (jnp.dot(tri, g_hi, preferred_element_type=F32)
                  + jnp.dot(tri, g_lo, preferred_element_type=F32))
            b_last = bc[c - 1:c, :]
            r = 0.5 * b_last
            qv = q_ref[rows, cols].astype(F32)
            qs = qv * _sigmoid(qv) * qscale
            kk = 1.0 - f
            vv = i_ref[rows, cols]
            qt = (qs * jnp.exp(bc - r)).astype(BF16)
            kt = (kk * jnp.exp(r - bc)).astype(BF16)
            a = lax.dot_general(qt, kt, nt, preferred_element_type=F32)
            a = jnp.where(causal, a, 0.0).astype(BF16)
            o = jnp.dot(a, vv, preferred_element_type=F32)
            st = st_ref[h]
            qe = (qs * jnp.exp(bc)).astype(BF16)
            o = o + lax.dot_general(qe, st.astype(BF16), nt, preferred_element_type=F32)
            kd = (kk * jnp.exp(b_last - bc)).astype(BF16)
            st_ref[h] = st * jnp.exp(b_last) + lax.dot_general(
                vv, kd, tn, preferred_element_type=F32)
            ms = jnp.mean(o * o, axis=-1, keepdims=True)
            on = o * lax.rsqrt(ms + EPS) * gn_ref[...]
            ogv = og_ref[rows, cols].astype(F32)
            o_ref[rows, cols] = (on * (ogv * _sigmoid(ogv))).astype(o_ref.dtype)
        return carry

    lax.fori_loop(0, n_chunks, chunk, 0)


def _hgrn(proj3, lb, gn, off_q, off_f, off_i, off_og, width, ts):
    b, s, _ = proj3.shape
    n_heads = width // HGRN_HEAD_DIM
    spec = lambda off: pl.BlockSpec((None, ts, width), lambda i, t: (i, t, off // width))
    return pl.pallas_call(
        functools.partial(_hgrn_kernel, n_heads=n_heads, n_chunks=ts // HGRN_CHUNK),
        grid=(b, s // ts),
        in_specs=[
            spec(off_q), spec(off_f), spec(off_i), spec(off_og),
            pl.BlockSpec((1, width), lambda i, t: (0, 0)),
            pl.BlockSpec((1, HGRN_HEAD_DIM), lambda i, t: (0, 0)),
        ],
        out_specs=pl.BlockSpec((None, ts, width), lambda i, t: (i, t, 0)),
        out_shape=jax.ShapeDtypeStruct((b, s, width), BF16),
        scratch_shapes=[pltpu.VMEM((n_heads, HGRN_HEAD_DIM, HGRN_HEAD_DIM), F32)],
        compiler_params=_cparams(("arbitrary", "arbitrary")),
        name="hgrn2",
    )(proj3, proj3, proj3, proj3, lb, gn)


def _merge_kernel(oa_ref, oh_ref, g0_ref, g1_ref, x_ref, wa_ref, wh_ref, wo_ref, gm_ref,
                  wrh_ref, wrl_ref, br_ref, h_ref, hn_ref, route_ref, plan_ref, cnt_ref):
    tm = x_ref.shape[0]

    @pl.when(pl.program_id(0) == 0)
    def _():
        cnt_ref[...] = jnp.zeros_like(cnt_ref)

    ya = jnp.dot(oa_ref[...], wa_ref[...], preferred_element_type=F32)
    yh = jnp.dot(oh_ref[...], wh_ref[...], preferred_element_type=F32)
    merged = (_sigmoid(g0_ref[...].astype(F32)) * ya
              + _sigmoid(g1_ref[...].astype(F32)) * yh).astype(BF16)
    h = x_ref[...] + jnp.dot(merged, wo_ref[...], preferred_element_type=F32)
    h_ref[...] = h
    ms = jnp.mean(h * h, axis=-1, keepdims=True)
    hn = h * lax.rsqrt(ms + EPS) * gm_ref[...]
    hn_ref[...] = _to_row_tiles(hn)

    hn_hi = hn.astype(BF16)
    hn_lo = (hn - hn_hi.astype(F32)).astype(BF16)
    logits = (jnp.dot(hn_hi, wrh_ref[...], preferred_element_type=F32)
              + jnp.dot(hn_hi, wrl_ref[...], preferred_element_type=F32)
              + jnp.dot(hn_lo, wrh_ref[...], preferred_element_type=F32)
              + br_ref[...])
    lane = lax.broadcasted_iota(jnp.int32, (tm, ROUTE_LANES), 1)
    neg = -jnp.inf

    def first_argmax(v):
        m = jnp.max(v, axis=-1, keepdims=True)
        idx = jnp.min(jnp.where(v == m, lane, ROUTE_LANES), axis=-1, keepdims=True)
        return m, idx

    is_group = lane < N_GROUPS
    gmax, gidx = first_argmax(jnp.where(is_group, logits, neg))
    p_sel = 1.0 / jnp.sum(jnp.where(is_group, jnp.exp(logits - gmax), 0.0), axis=-1, keepdims=True)
    eidx = lane - EXPERT_LANE0
    in_group = jnp.logical_and(
        jnp.logical_and(eidx >= 0, eidx < N_EXPERTS), (eidx // EXPERTS_PER_GROUP) == gidx)
    el = jnp.where(in_group, logits, neg)
    m1, i1 = first_argmax(el)
    m2, i2 = first_argmax(jnp.where(lane == i1, neg, el))
    t = jnp.exp(m2 - m1)
    w1 = p_sel / (1.0 + t)
    w2 = p_sel * t / (1.0 + t)

    sel1 = lane == i1
    sel2 = lane == i2
    onehot = jnp.logical_or(sel1, sel2).astype(BF16)
    ri = lax.broadcasted_iota(jnp.int32, (tm, tm), 0)
    ci = lax.broadcasted_iota(jnp.int32, (tm, tm), 1)
    before = (ci < ri).astype(BF16)
    cum = jnp.dot(before, onehot, preferred_element_type=F32) + cnt_ref[...]
    r1 = jnp.sum(jnp.where(sel1, cum, 0.0), axis=-1, keepdims=True)
    r2 = jnp.sum(jnp.where(sel2, cum, 0.0), axis=-1, keepdims=True)
    cnt_ref[...] += jnp.sum(onehot.astype(F32), axis=0, keepdims=True)

    e1 = (i1 - EXPERT_LANE0).astype(F32)
    e2 = (i2 - EXPERT_LANE0).astype(F32)
    out = jnp.zeros((tm, ROUTE_LANES), F32)
    for k, v in enumerate((e1, e2, w1, w2, r1, r2)):
        out = jnp.where(lane == k, v, out)
    route_ref[...] = out
    plan_ref[...] = out.T[:PLAN_ROWS, :].astype(jnp.int32)


def _merge_route(o_a, o_h, proj, x2, wa, wh, wo, gm, wr_hi, wr_lo, br, off_g0, off_g1, tm):
    t, d = x2.shape
    aw, hw = o_a.shape[1], o_h.shape[1]
    const = lambda shape: pl.BlockSpec(shape, lambda i: (0, 0), pipeline_mode=pl.Buffered(1))
    row = lambda w: pl.BlockSpec((tm, w), lambda i: (i, 0))
    return pl.pallas_call(
        _merge_kernel,
        grid=(t // tm,),
        in_specs=[
            row(aw), row(hw),
            pl.BlockSpec((tm, d), lambda i: (i, off_g0 // d)),
            pl.BlockSpec((tm, d), lambda i: (i, off_g1 // d)),
            row(d),
            const((aw, d)), const((hw, d)), const((d, d)), const((1, d)),
            const((d, ROUTE_LANES)), const((d, ROUTE_LANES)), const((1, ROUTE_LANES)),
        ],
        out_specs=[row(d), pl.BlockSpec((tm, d // LANES, LANES), lambda i: (i, 0, 0)),
                   row(ROUTE_LANES),
                   pl.BlockSpec((None, PLAN_ROWS, tm), lambda i: (i, 0, 0)),
                   pl.BlockSpec((1, ROUTE_LANES), lambda i: (0, 0))],
        out_shape=[
            jax.ShapeDtypeStruct((t, d), F32),
            jax.ShapeDtypeStruct((t, d // LANES, LANES), BF16),
            jax.ShapeDtypeStruct((t, ROUTE_LANES), F32),
            jax.ShapeDtypeStruct((t // tm, PLAN_ROWS, tm), jnp.int32),
            jax.ShapeDtypeStruct((1, ROUTE_LANES), F32),
        ],
        compiler_params=_cparams(("arbitrary",)),
        name="merge_route",
    )(o_a, o_h, proj, proj, x2, wa, wh, wo, gm, wr_hi, wr_lo, br)


def _dest_row(pst_ref, plan_ref, k, r):
    return pst_ref[plan_ref[PLAN_E + k, r]] + plan_ref[PLAN_R + k, r]


def _dispatch_kernel(fill_ref, nfill_ref, pst_ref, plan_ref, hn_ref, rows_ref, zbuf, sem):
    tm = hn_ref.shape[0]
    bm = zbuf.shape[0]

    @pl.when(pl.program_id(0) == 0)
    def _():
        zbuf[...] = jnp.zeros_like(zbuf)

        def fill(j, carry):
            start = pl.multiple_of(fill_ref[j], bm)
            pltpu.make_async_copy(zbuf, rows_ref.at[pl.ds(start, bm)], sem).start()
            return carry

        def drain(j, carry):
            pltpu.make_async_copy(zbuf, rows_ref.at[pl.ds(0, bm)], sem).wait()
            return carry

        lax.fori_loop(0, nfill_ref[0], fill, 0)
        lax.fori_loop(0, nfill_ref[0], drain, 0)

    def issue(r, carry):
        for k in range(2):
            pltpu.make_async_copy(
                hn_ref.at[r], rows_ref.at[_dest_row(pst_ref, plan_ref, k, r)], sem
            ).start(priority=k)
        return carry

    lax.fori_loop(0, tm, issue, 0)
    for _ in range(2):
        pltpu.make_async_copy(hn_ref, rows_ref.at[pl.ds(0, tm)], sem).wait()


def _dispatch(hn, plan, pst, fill, nfill, n_rows, tm, bm):
    t, c, l = hn.shape
    grid_spec = pltpu.PrefetchScalarGridSpec(
        num_scalar_prefetch=3,
        grid=(t // tm,),
        in_specs=[
            pl.BlockSpec((None, PLAN_ROWS, tm), lambda i, *_: (i, 0, 0), memory_space=pltpu.SMEM),
            pl.BlockSpec((tm, c, l), lambda i, *_: (i, 0, 0)),
        ],
        out_specs=pl.BlockSpec(memory_space=pl.ANY),
        scratch_shapes=[pltpu.VMEM((bm, c, l), hn.dtype), pltpu.SemaphoreType.DMA],
    )
    return pl.pallas_call(
        _dispatch_kernel,
        grid_spec=grid_spec,
        out_shape=jax.ShapeDtypeStruct((n_rows, c, l), hn.dtype),
        compiler_params=_cparams(("arbitrary",)),
        name="moe_dispatch",
    )(fill, nfill, pst, plan, hn)


def _expert_kernel(be_ref, meta_ref, x_ref, w1_ref, w3_ref, w2_ref, y_ref, w1b, w3b, w2b):
    i = pl.program_id(0)
    new_expert = jnp.logical_or(i == 0, be_ref[i] != be_ref[jnp.maximum(i - 1, 0)])

    @pl.when(jnp.logical_and(i < meta_ref[0], new_expert))
    def _():
        w1b[...] = w1_ref[...].astype(BF16)
        w3b[...] = w3_ref[...].astype(BF16)
        w2b[...] = w2_ref[...].astype(BF16)

    @pl.when(i < meta_ref[0])
    def _():
        x = _from_row_tiles(x_ref[...])
        h1 = jnp.dot(x, w1b[...], preferred_element_type=F32)
        h3 = jnp.dot(x, w3b[...], preferred_element_type=F32)
        hb = (h1 * _sigmoid(h1) * h3).astype(BF16)
        y_ref[...] = _to_row_tiles(jnp.dot(hb, w2b[...], preferred_element_type=F32))

    @pl.when(i >= meta_ref[0])
    def _():
        y_ref[...] = jnp.zeros_like(y_ref)


def _experts(rows, block_expert, meta, w1, w3, w2, bm):
    n_rows, c, l = rows.shape
    d = c * l
    de = w1.shape[-1]
    n_blocks = n_rows // bm
    grid_spec = pltpu.PrefetchScalarGridSpec(
        num_scalar_prefetch=2,
        grid=(n_blocks,),
        in_specs=[
            pl.BlockSpec((bm, c, l), lambda i, be, meta: (jnp.minimum(i, meta[0] - 1), 0, 0)),
            pl.BlockSpec((None, d, de), lambda i, be, meta: (be[i], 0, 0)),
            pl.BlockSpec((None, d, de), lambda i, be, meta: (be[i], 0, 0)),
            pl.BlockSpec((None, de, d), lambda i, be, meta: (be[i], 0, 0)),
        ],
        out_specs=pl.BlockSpec((bm, c, l), lambda i, be, meta: (i, 0, 0)),
        scratch_shapes=[pltpu.VMEM((d, de), BF16), pltpu.VMEM((d, de), BF16),
                        pltpu.VMEM((de, d), BF16)],
    )
    return pl.pallas_call(
        _expert_kernel,
        grid_spec=grid_spec,
        out_shape=jax.ShapeDtypeStruct((n_rows, c, l), BF16),
        compiler_params=_cparams(("arbitrary",)),
        name="moe_experts",
    )(block_expert, meta, rows, w1, w3, w2)


def _combine_kernel(pst_ref, pcur_ref, pnext_ref, h_ref, route_ref, g_ref, y_ref, o_ref,
                    ybuf, sems):
    i = pl.program_id(0)
    nt = pl.num_programs(0)
    tm, d = h_ref.shape
    slot = i % 2

    def gather(plan_ref, s):
        def issue(r, carry):
            for k in range(2):
                pltpu.make_async_copy(
                    y_ref.at[_dest_row(pst_ref, plan_ref, k, r)],
                    ybuf.at[s, k, r], sems.at[s]).start(priority=k)
            return carry
        lax.fori_loop(0, tm, issue, 0)

    @pl.when(i == 0)
    def _():
        gather(pcur_ref, slot)

    @pl.when(i + 1 < nt)
    def _():
        gather(pnext_ref, 1 - slot)

    for k in range(2):
        pltpu.make_async_copy(y_ref.at[pl.ds(0, tm)], ybuf.at[slot, k], sems.at[slot]).wait()

    route = route_ref[...]
    w1 = route[:, 2:3]
    w2 = route[:, 3:4]
    ya = _from_row_tiles(ybuf[slot, 0]).astype(F32)
    yb = _from_row_tiles(ybuf[slot, 1]).astype(F32)
    h = h_ref[...] + w1 * ya + w2 * yb
    ms = jnp.mean(h * h, axis=-1, keepdims=True)
    o_ref[...] = h * lax.rsqrt(ms + EPS) * g_ref[...]


def _combine(h, route, plan, pst, y_rows, g, tm):
    t, d = h.shape
    nt = t // tm
    plan_spec = lambda f: pl.BlockSpec((None, PLAN_ROWS, tm), f, memory_space=pltpu.SMEM)
    grid_spec = pltpu.PrefetchScalarGridSpec(
        num_scalar_prefetch=1,
        grid=(nt,),
        in_specs=[
            plan_spec(lambda i, *_: (i, 0, 0)),
            plan_spec(lambda i, *_: (jnp.minimum(i + 1, nt - 1), 0, 0)),
            pl.BlockSpec((tm, d), lambda i, *_: (i, 0)),
            pl.BlockSpec((tm, ROUTE_LANES), lambda i, *_: (i, 0)),
            pl.BlockSpec((1, d), lambda i, *_: (0, 0)),
            pl.BlockSpec(memory_space=pl.ANY),
        ],
        out_specs=pl.BlockSpec((tm, d), lambda i, *_: (i, 0)),
        scratch_shapes=[pltpu.VMEM((2, 2, tm, d // LANES, LANES), BF16),
                        pltpu.SemaphoreType.DMA((2,))],
    )
    return pl.pallas_call(
        _combine_kernel,
        grid_spec=grid_spec,
        out_shape=jax.ShapeDtypeStruct((t, d), F32),
        compiler_params=_cparams(("arbitrary",)),
        name="moe_combine",
    )(pst, plan, plan, h, route, g, y_rows)


def kernel(x, w_in, attn_sinks, hgrn_lb_logits, hgrn_norm_g, w_br_attn, w_br_hgrn, w_out,
           mix_norm_g, moe_norm_g, w_router_group, b_router_group, w_router_expert,
           b_router_expert, w1, w3, w2, final_norm_g):
    b, s, d = x.shape
    t = b * s
    depth = w_in.shape[0]
    assert depth == 1
    attn_w = (d // 128) * ATTN_HEAD_DIM
    kv_w = attn_w // Q_PER_KV
    hg_w = (d // 256) * HGRN_HEAD_DIM
    assert kv_w == N_KV_HEADS * ATTN_HEAD_DIM

    src = {}
    off = 0
    for name, width in (("qa", attn_w), ("k", kv_w), ("v", kv_w), ("qh", hg_w), ("f", hg_w),
                        ("i", hg_w), ("og", hg_w), ("g0", d), ("g1", d)):
        src[name] = (off, width)
        off += width
    order = ("g0", "g1", "qa", "qh", "f", "i", "og", "k", "v")
    dst = {}
    off = 0
    for name in order:
        dst[name] = off
        off += src[name][1]
    n_in = off
    l = 0
    w_perm = jnp.concatenate(
        [w_in[l][:, src[n][0]:src[n][0] + src[n][1]] for n in order], axis=1).astype(BF16)

    lb = jnp.cumsum(jax.nn.softmax(hgrn_lb_logits.astype(F32), axis=0), axis=0)[l].reshape(1, hg_w)

    x2 = x.reshape(t, d)
    tm_in = min(2048, t)
    proj = _in_proj(x2, mix_norm_g[l].reshape(1, d), w_perm, tm_in, 512)
    proj3 = proj.reshape(b, s, n_in)

    o_a = _attention(proj3, attn_sinks[l].astype(F32), dst["qa"], dst["k"], dst["v"], attn_w, kv_w)
    o_h = _hgrn(proj3, lb, hgrn_norm_g[l].reshape(1, HGRN_HEAD_DIM).astype(F32),
                dst["qh"], dst["f"], dst["i"], dst["og"], hg_w, min(256, s))

    w_r = jnp.zeros((d, ROUTE_LANES), F32)
    w_r = w_r.at[:, :N_GROUPS].set(w_router_group[l])
    w_r = w_r.at[:, EXPERT_LANE0:EXPERT_LANE0 + N_EXPERTS].set(w_router_expert[l])
    wr_hi = w_r.astype(BF16)
    wr_lo = (w_r - wr_hi.astype(F32)).astype(BF16)
    b_r = jnp.zeros((1, ROUTE_LANES), F32)
    b_r = b_r.at[0, :N_GROUPS].set(b_router_group[l])
    b_r = b_r.at[0, EXPERT_LANE0:EXPERT_LANE0 + N_EXPERTS].set(b_router_expert[l])

    tm = min(256, t)
    h, hn, route, plan, cnt = _merge_route(
        o_a.reshape(t, attn_w), o_h.reshape(t, hg_w), proj, x2,
        w_br_attn[l].astype(BF16), w_br_hgrn[l].astype(BF16), w_out[l].astype(BF16),
        moe_norm_g[l].reshape(1, d), wr_hi, wr_lo, b_r, dst["g0"], dst["g1"], tm)

    bm = MOE_ROWS
    counts = cnt[0, EXPERT_LANE0:EXPERT_LANE0 + N_EXPERTS].astype(jnp.int32)
    padded = ((counts + bm - 1) // bm) * bm
    pad_end = jnp.cumsum(padded)
    pad_start = (pad_end - padded).astype(jnp.int32)
    n_blocks = (2 * t) // bm + N_EXPERTS
    n_used = pad_end[-1] // bm
    blk_ids = jnp.minimum(jnp.arange(n_blocks, dtype=jnp.int32), n_used - 1)
    block_expert = jnp.minimum(
        jnp.sum((pad_end[None, :] <= (blk_ids * bm)[:, None]).astype(jnp.int32), axis=1),
        N_EXPERTS - 1)
    meta = jnp.stack([n_used, n_used]).astype(jnp.int32)

    blk_all = jnp.arange(n_blocks, dtype=jnp.int32)
    fill = jnp.concatenate([
        jnp.maximum(pad_end - bm, 0).astype(jnp.int32),
        jnp.minimum(n_used + blk_all, n_blocks - 1) * bm])
    nfill = (N_EXPERTS + n_blocks - n_used).astype(jnp.int32).reshape(1)

    rows = _dispatch(hn, plan, pad_start, fill, nfill, n_blocks * bm, tm, bm)
    y_rows = _experts(rows, block_expert, meta, w1[l], w3[l], w2[l], bm)
    out = _combine(h, route, plan, pad_start, y_rows, final_norm_g.reshape(1, d), tm)
    return out.reshape(b, s, d)
```

```python
import functools

import jax
import jax.numpy as jnp
from jax import lax
from jax.experimental import pallas as pl
from jax.experimental.pallas import tpu as pltpu

F32 = jnp.float32
BF16 = jnp.bfloat16

EPS = 1e-6
ATTN_HEAD_DIM = 64
Q_PER_KV = 4
N_KV_HEADS = 4
ATTN_BLOCK = 128
HGRN_HEAD_DIM = 128
HGRN_CHUNK = 64
N_GROUPS = 4
EXPERTS_PER_GROUP = 8
N_EXPERTS = N_GROUPS * EXPERTS_PER_GROUP
LANES = 128
ROUTE_LANES = LANES
EXPERT_LANE0 = N_GROUPS
MOE_ROWS = 256
PLAN_ROWS = 8
PLAN_E, PLAN_R = 0, 4
VMEM_LIMIT = 56 * 1024 * 1024


def _sigmoid(x):
    return 1.0 / (1.0 + jnp.exp(-x))


def _cparams(sem, vmem=VMEM_LIMIT):
    return pltpu.CompilerParams(dimension_semantics=sem, vmem_limit_bytes=vmem)


def _to_row_tiles(v):
    m, d = v.shape
    return v.reshape(m, d // LANES, LANES).astype(BF16)


def _from_row_tiles(p):
    m, c, l = p.shape
    return p.reshape(m, c * l)


def _inproj_kernel(x_ref, g_ref, w_ref, o_ref, xn_ref):
    @pl.when(pl.program_id(1) == 0)
    def _():
        x = x_ref[...]
        ms = jnp.mean(x * x, axis=-1, keepdims=True)
        xn_ref[...] = (x * lax.rsqrt(ms + EPS) * g_ref[...]).astype(BF16)

    o_ref[...] = jnp.dot(xn_ref[...], w_ref[...], preferred_element_type=F32).astype(o_ref.dtype)


def _in_proj(x2, g, w, tm, tn):
    t, d = x2.shape
    n = w.shape[1]
    return pl.pallas_call(
        _inproj_kernel,
        grid=(t // tm, n // tn),
        in_specs=[
            pl.BlockSpec((tm, d), lambda i, j: (i, 0)),
            pl.BlockSpec((1, d), lambda i, j: (0, 0)),
            pl.BlockSpec((d, tn), lambda i, j: (0, j)),
        ],
        out_specs=pl.BlockSpec((tm, tn), lambda i, j: (i, j)),
        out_shape=jax.ShapeDtypeStruct((t, n), BF16),
        scratch_shapes=[pltpu.VMEM((tm, d), BF16)],
        compiler_params=_cparams(("arbitrary", "arbitrary")),
        name="in_proj",
    )(x2, g, w)


def _attn_kernel(sink_ref, q_ref, kc_ref, kp_ref, vc_ref, vp_ref, o_ref):
    n = pl.program_id(1)
    blk, dh, g = ATTN_BLOCK, ATTN_HEAD_DIM, Q_PER_KV
    rows = g * blk
    qi = lax.broadcasted_iota(jnp.int32, (rows, blk), 0) % blk
    kj = lax.broadcasted_iota(jnp.int32, (rows, blk), 1)
    mask_cur = kj <= qi
    mask_prev = jnp.logical_and(kj > qi, n > 0)
    head_of_row = lax.broadcasted_iota(jnp.int32, (rows, 1), 0) // blk
    nt = (((1,), (1,)), ((), ()))
    scale = dh ** -0.5
    for h in range(N_KV_HEADS):
        kc = kc_ref[:, h * dh:(h + 1) * dh]
        kp = kp_ref[:, h * dh:(h + 1) * dh]
        vc = vc_ref[:, h * dh:(h + 1) * dh]
        vp = vp_ref[:, h * dh:(h + 1) * dh]
        q4 = jnp.concatenate(
            [q_ref[:, (h * g + j) * dh:(h * g + j + 1) * dh] for j in range(g)], axis=0)
        sc = lax.dot_general(q4, kc, nt, preferred_element_type=F32) * scale
        sp = lax.dot_general(q4, kp, nt, preferred_element_type=F32) * scale
        sc = jnp.where(mask_cur, sc, -jnp.inf)
        sp = jnp.where(mask_prev, sp, -jnp.inf)
        sink = jnp.zeros((rows, 1), F32)
        for j in range(g):
            sink = jnp.where(head_of_row == j, sink_ref[h * g + j], sink)
        m = jnp.maximum(jnp.maximum(jnp.max(sc, axis=-1, keepdims=True),
                                    jnp.max(sp, axis=-1, keepdims=True)), sink)
        pc = jnp.exp(sc - m)
        pp = jnp.exp(sp - m)
        den = (jnp.sum(pc, axis=-1, keepdims=True) + jnp.sum(pp, axis=-1, keepdims=True)
               + jnp.exp(sink - m))
        o = (jnp.dot(pc.astype(BF16), vc, preferred_element_type=F32)
             + jnp.dot(pp.astype(BF16), vp, preferred_element_type=F32)) / den
        for j in range(g):
            o_ref[:, (h * g + j) * dh:(h * g + j + 1) * dh] = (
                o[j * blk:(j + 1) * blk, :].astype(o_ref.dtype))


def _attention(proj3, sinks, off_q, off_k, off_v, attn_w, kv_w):
    b, s, _ = proj3.shape
    blk = ATTN_BLOCK
    qb, kb, vb = off_q // attn_w, off_k // kv_w, off_v // kv_w
    prev = lambda n: jnp.maximum(n - 1, 0)
    return pl.pallas_call(
        _attn_kernel,
        grid=(b, s // blk),
        in_specs=[
            pl.BlockSpec(memory_space=pltpu.SMEM),
            pl.BlockSpec((None, blk, attn_w), lambda i, n: (i, n, qb)),
            pl.BlockSpec((None, blk, kv_w), lambda i, n: (i, n, kb)),
            pl.BlockSpec((None, blk, kv_w), lambda i, n: (i, prev(n), kb)),
            pl.BlockSpec((None, blk, kv_w), lambda i, n: (i, n, vb)),
            pl.BlockSpec((None, blk, kv_w), lambda i, n: (i, prev(n), vb)),
        ],
        out_specs=pl.BlockSpec((None, blk, attn_w), lambda i, n: (i, n, 0)),
        out_shape=jax.ShapeDtypeStruct((b, s, attn_w), BF16),
        compiler_params=_cparams(("arbitrary", "arbitrary")),
        name="swa_attention",
    )(sinks, proj3, proj3, proj3, proj3, proj3)


def _hgrn_kernel(q_ref, f_ref, i_ref, og_ref, lb_ref, gn_ref, o_ref, st_ref, *, n_heads, n_chunks):
    c, dk = HGRN_CHUNK, HGRN_HEAD_DIM

    @pl.when(pl.program_id(1) == 0)
    def _():
        st_ref[...] = jnp.zeros_like(st_ref)

    ti = lax.broadcasted_iota(jnp.int32, (c, c), 0)
    si = lax.broadcasted_iota(jnp.int32, (c, c), 1)
    causal = si <= ti
    tri = causal.astype(BF16)
    nt = (((1,), (1,)), ((), ()))
    tn = (((0,), (0,)), ((), ()))
    qscale = dk ** -0.5

    def chunk(ci, carry):
        rows = pl.ds(pl.multiple_of(ci * c, c), c)
        for h in range(n_heads):
            cols = slice(h * dk, (h + 1) * dk)
            lb = lb_ref[:, cols]
            f = lb + (1.0 - lb) * _sigmoid(f_ref[rows, cols].astype(F32))
            gl = jnp.log(f)
            g_hi = gl.astype(BF16)
            g_lo = (gl - g_hi.astype(F32)).astype(BF16)
            bc = (jnp.dot(tri, g_hi, preferred_element_type=F32)
                  + jnp.dot(tri, g_lo, preferred_element_type=F32))
            b_last = bc[c - 1:c, :]
            r = 0.5 * b_last
            qv = q_ref[rows, cols].astype(F32)
            qs = qv * _sigmoid(qv) * qscale
            kk = 1.0 - f
            vv = i_ref[rows, cols]
            qt = (qs * jnp.exp(bc - r)).astype(BF16)
            kt = (kk * jnp.exp(r - bc)).astype(BF16)
            a = lax.dot_general(qt, kt, nt, preferred_element_type=F32)
            a = jnp.where(causal, a, 0.0).astype(BF16)
            o = jnp.dot(a, vv, preferred_element_type=F32)
            st = st_ref[h]
            qe = (qs * jnp.exp(bc)).astype(BF16)
            o = o + lax.dot_general(qe, st.astype(BF16), nt, preferred_element_type=F32)
            kd = (kk * jnp.exp(b_last - bc)).astype(BF16)
            st_ref[h] = st * jnp.exp(b_last) + lax.dot_general(
                vv, kd, tn, preferred_element_type=F32)
            ms = jnp.mean(o * o, axis=-1, keepdims=True)
            on = o * lax.rsqrt(ms + EPS) * gn_ref[...]
            ogv = og_ref[rows, cols].astype(F32)
            o_ref[rows, cols] = (on * (ogv * _sigmoid(ogv))).astype(o_ref.dtype)
        return carry

    lax.fori_loop(0, n_chunks, chunk, 0)


def _hgrn(proj3, lb, gn, off_q, off_f, off_i, off_og, width, ts):
    b, s, _ = proj3.shape
    n_heads = width // HGRN_HEAD_DIM
    spec = lambda off: pl.BlockSpec((None, ts, width), lambda i, t: (i, t, off // width))
    return pl.pallas_call(
        functools.partial(_hgrn_kernel, n_heads=n_heads, n_chunks=ts // HGRN_CHUNK),
        grid=(b, s // ts),
        in_specs=[
            spec(off_q), spec(off_f), spec(off_i), spec(off_og),
            pl.BlockSpec((1, width), lambda i, t: (0, 0)),
            pl.BlockSpec((1, HGRN_HEAD_DIM), lambda i, t: (0, 0)),
        ],
        out_specs=pl.BlockSpec((None, ts, width), lambda i, t: (i, t, 0)),
        out_shape=jax.ShapeDtypeStruct((b, s, width), BF16),
        scratch_shapes=[pltpu.VMEM((n_heads, HGRN_HEAD_DIM, HGRN_HEAD_DIM), F32)],
        compiler_params=_cparams(("arbitrary", "arbitrary")),
        name="hgrn2",
    )(proj3, proj3, proj3, proj3, lb, gn)


def _merge_kernel(oa_ref, oh_ref, g0_ref, g1_ref, x_ref, wa_ref, wh_ref, wo_ref, gm_ref,
                  wrh_ref, wrl_ref, br_ref, h_ref, hn_ref, route_ref, plan_ref, cnt_ref):
    tm = x_ref.shape[0]

    @pl.when(pl.program_id(0) == 0)
    def _():
        cnt_ref[...] = jnp.zeros_like(cnt_ref)

    ya = jnp.dot(oa_ref[...], wa_ref[...], preferred_element_type=F32)
    yh = jnp.dot(oh_ref[...], wh_ref[...], preferred_element_type=F32)
    merged = (_sigmoid(g0_ref[...].astype(F32)) * ya
              + _sigmoid(g1_ref[...].astype(F32)) * yh).astype(BF16)
    h = x_ref[...] + jnp.dot(merged, wo_ref[...], preferred_element_type=F32)
    h_ref[...] = h
    ms = jnp.mean(h * h, axis=-1, keepdims=True)
    hn = h * lax.rsqrt(ms + EPS) * gm_ref[...]
    hn_ref[...] = _to_row_tiles(hn)

    hn_hi = hn.astype(BF16)
    hn_lo = (hn - hn_hi.astype(F32)).astype(BF16)
    logits = (jnp.dot(hn_hi, wrh_ref[...], preferred_element_type=F32)
              + jnp.dot(hn_hi, wrl_ref[...], preferred_element_type=F32)
              + jnp.dot(hn_lo, wrh_ref[...], preferred_element_type=F32)
              + br_ref[...])
    lane = lax.broadcasted_iota(jnp.int32, (tm, ROUTE_LANES), 1)
    neg = -jnp.inf

    def first_argmax(v):
        m = jnp.max(v, axis=-1, keepdims=True)
        idx = jnp.min(jnp.where(v == m, lane, ROUTE_LANES), axis=-1, keepdims=True)
        return m, idx

    is_group = lane < N_GROUPS
    gmax, gidx = first_argmax(jnp.where(is_group, logits, neg))
    p_sel = 1.0 / jnp.sum(jnp.where(is_group, jnp.exp(logits - gmax), 0.0), axis=-1, keepdims=True)
    eidx = lane - EXPERT_LANE0
    in_group = jnp.logical_and(
        jnp.logical_and(eidx >= 0, eidx < N_EXPERTS), (eidx // EXPERTS_PER_GROUP) == gidx)
    el = jnp.where(in_group, logits, neg)
    m1, i1 = first_argmax(el)
    m2, i2 = first_argmax(jnp.where(lane == i1, neg, el))
    t = jnp.exp(m2 - m1)
    w1 = p_sel / (1.0 + t)
    w2 = p_sel * t / (1.0 + t)

    sel1 = lane == i1
    sel2 = lane == i2
    onehot = jnp.logical_or(sel1, sel2).astype(BF16)
    ri = lax.broadcasted_iota(jnp.int32, (tm, tm), 0)
    ci = lax.broadcasted_iota(jnp.int32, (tm, tm), 1)
    before = (ci < ri).astype(BF16)
    cum = jnp.dot(before, onehot, preferred_element_type=F32) + cnt_ref[...]
    r1 = jnp.sum(jnp.where(sel1, cum, 0.0), axis=-1, keepdims=True)
    r2 = jnp.sum(jnp.where(sel2, cum, 0.0), axis=-1, keepdims=True)
    cnt_ref[...] += jnp.sum(onehot.astype(F32), axis=0, keepdims=True)

    e1 = (i1 - EXPERT_LANE0).astype(F32)
    e2 = (i2 - EXPERT_LANE0).astype(F32)
    out = jnp.zeros((tm, ROUTE_LANES), F32)
    for k, v in enumerate((e1, e2, w1, w2, r1, r2)):
        out = jnp.where(lane == k, v, out)
    route_ref[...] = out
    plan_ref[...] = out.T[:PLAN_ROWS, :].astype(jnp.int32)


def _merge_route(o_a, o_h, proj, x2, wa, wh, wo, gm, wr_hi, wr_lo, br, off_g0, off_g1, tm):
    t, d = x2.shape
    aw, hw = o_a.shape[1], o_h.shape[1]
    const = lambda shape: pl.BlockSpec(shape, lambda i: (0, 0), pipeline_mode=pl.Buffered(1))
    row = lambda w: pl.BlockSpec((tm, w), lambda i: (i, 0))
    return pl.pallas_call(
        _merge_kernel,
        grid=(t // tm,),
        in_specs=[
            row(aw), row(hw),
            pl.BlockSpec((tm, d), lambda i: (i, off_g0 // d)),
            pl.BlockSpec((tm, d), lambda i: (i, off_g1 // d)),
            row(d),
            const((aw, d)), const((hw, d)), const((d, d)), const((1, d)),
            const((d, ROUTE_LANES)), const((d, ROUTE_LANES)), const((1, ROUTE_LANES)),
        ],
        out_specs=[row(d), pl.BlockSpec((tm, d // LANES, LANES), lambda i: (i, 0, 0)),
                   row(ROUTE_LANES),
                   pl.BlockSpec((None, PLAN_ROWS, tm), lambda i: (i, 0, 0)),
                   pl.BlockSpec((1, ROUTE_LANES), lambda i: (0, 0))],
        out_shape=[
            jax.ShapeDtypeStruct((t, d), F32),
            jax.ShapeDtypeStruct((t, d // LANES, LANES), BF16),
            jax.ShapeDtypeStruct((t, ROUTE_LANES), F32),
            jax.ShapeDtypeStruct((t // tm, PLAN_ROWS, tm), jnp.int32),
            jax.ShapeDtypeStruct((1, ROUTE_LANES), F32),
        ],
        compiler_params=_cparams(("arbitrary",)),
        name="merge_route",
    )(o_a, o_h, proj, proj, x2, wa, wh, wo, gm, wr_hi, wr_lo, br)


def _dest_kernel(pst_ref, plan_ref, dest_ref):
    e = plan_ref[:, PLAN_E:PLAN_E + 2, :]
    start = jnp.zeros(e.shape, jnp.int32)
    for j in range(N_EXPERTS):
        start = jnp.where(e == j, pst_ref[j], start)
    dest_ref[...] = start + plan_ref[:, PLAN_R:PLAN_R + 2, :]


def _dest_rows(plan, pst, tiles):
    nt, _, tm = plan.shape
    grid_spec = pltpu.PrefetchScalarGridSpec(
        num_scalar_prefetch=1,
        grid=(nt // tiles,),
        in_specs=[pl.BlockSpec((tiles, PLAN_ROWS, tm), lambda i, *_: (i, 0, 0))],
        out_specs=pl.BlockSpec((tiles, 2, tm), lambda i, *_: (i, 0, 0)),
    )
    return pl.pallas_call(
        _dest_kernel,
        grid_spec=grid_spec,
        out_shape=jax.ShapeDtypeStruct((nt, 2, tm), jnp.int32),
        compiler_params=_cparams(("arbitrary",)),
        name="moe_dest",
    )(pst, plan)


ISSUE_UNROLL = 8


def _dispatch_kernel(fill_ref, nfill_ref, dest_ref, hn_ref, rows_ref, zbuf, sem):
    tm = hn_ref.shape[0]
    bm = zbuf.shape[0]

    @pl.when(pl.program_id(0) == 0)
    def _():
        zbuf[...] = jnp.zeros_like(zbuf)

        def fill(j, carry):
            start = pl.multiple_of(fill_ref[j], bm)
            pltpu.make_async_copy(zbuf, rows_ref.at[pl.ds(start, bm)], sem).start()
            return carry

        def drain(j, carry):
            pltpu.make_async_copy(zbuf, rows_ref.at[pl.ds(0, bm)], sem).wait()
            return carry

        lax.fori_loop(0, nfill_ref[0], fill, 0)
        lax.fori_loop(0, nfill_ref[0], drain, 0)

    def issue(r, carry):
        for k in range(2):
            pltpu.make_async_copy(
                hn_ref.at[r], rows_ref.at[dest_ref[k, r]], sem).start(priority=k)
        return carry

    lax.fori_loop(0, tm, issue, 0, unroll=ISSUE_UNROLL)
    for _ in range(2):
        pltpu.make_async_copy(hn_ref, rows_ref.at[pl.ds(0, tm)], sem).wait()


def _dispatch(hn, dest, fill, nfill, n_rows, tm, bm):
    t, c, l = hn.shape
    grid_spec = pltpu.PrefetchScalarGridSpec(
        num_scalar_prefetch=2,
        grid=(t // tm,),
        in_specs=[
            pl.BlockSpec((None, 2, tm), lambda i, *_: (i, 0, 0), memory_space=pltpu.SMEM),
            pl.BlockSpec((tm, c, l), lambda i, *_: (i, 0, 0)),
        ],
        out_specs=pl.BlockSpec(memory_space=pl.ANY),
        scratch_shapes=[pltpu.VMEM((bm, c, l), hn.dtype), pltpu.SemaphoreType.DMA],
    )
    return pl.pallas_call(
        _dispatch_kernel,
        grid_spec=grid_spec,
        out_shape=jax.ShapeDtypeStruct((n_rows, c, l), hn.dtype),
        compiler_params=_cparams(("arbitrary",)),
        name="moe_dispatch",
    )(fill, nfill, dest, hn)


def _expert_kernel(be_ref, meta_ref, slot_ref, nxt_ref, x_ref, w1_hbm, w3_hbm, w2_hbm, y_ref,
                   wf1, wf3, wf2, w1b, w3b, w2b, sems):
    i = pl.program_id(0)
    e = be_ref[i]
    new_expert = jnp.logical_or(i == 0, e != be_ref[jnp.maximum(i - 1, 0)])

    def weight_copies(expert, slot):
        return [pltpu.make_async_copy(src.at[expert], dst.at[slot], sems.at[slot])
                for src, dst in ((w1_hbm, wf1), (w3_hbm, wf3), (w2_hbm, wf2))]

    @pl.when(jnp.logical_and(i < meta_ref[0], new_expert))
    def _():
        slot = slot_ref[e]
        nxt = nxt_ref[e]

        @pl.when(i == 0)
        def _():
            for c in weight_copies(e, slot):
                c.start()

        @pl.when(nxt >= 0)
        def _():
            for c in weight_copies(nxt, 1 - slot):
                c.start()

        for c in weight_copies(e, slot):
            c.wait()
        w1b[...] = wf1[slot].astype(BF16)
        w3b[...] = wf3[slot].astype(BF16)
        w2b[...] = wf2[slot].astype(BF16)

    @pl.when(i < meta_ref[0])
    def _():
        x = _from_row_tiles(x_ref[...])
        h1 = jnp.dot(x, w1b[...], preferred_element_type=F32)
        h3 = jnp.dot(x, w3b[...], preferred_element_type=F32)
        hb = (h1 * _sigmoid(h1) * h3).astype(BF16)
        y_ref[...] = _to_row_tiles(jnp.dot(hb, w2b[...], preferred_element_type=F32))

    @pl.when(i >= meta_ref[0])
    def _():
        y_ref[...] = jnp.zeros_like(y_ref)


def _experts(rows, block_expert, meta, slot_e, nxt_e, w1, w3, w2, bm):
    n_rows, c, l = rows.shape
    d = c * l
    de = w1.shape[-1]
    n_blocks = n_rows // bm
    hbm = pl.BlockSpec(memory_space=pl.ANY)
    grid_spec = pltpu.PrefetchScalarGridSpec(
        num_scalar_prefetch=4,
        grid=(n_blocks,),
        in_specs=[
            pl.BlockSpec((bm, c, l), lambda i, be, meta, *_: (jnp.minimum(i, meta[0] - 1), 0, 0)),
            hbm, hbm, hbm,
        ],
        out_specs=pl.BlockSpec((bm, c, l), lambda i, *_: (i, 0, 0)),
        scratch_shapes=[
            pltpu.VMEM((2, d, de), w1.dtype), pltpu.VMEM((2, d, de), w3.dtype),
            pltpu.VMEM((2, de, d), w2.dtype),
            pltpu.VMEM((d, de), BF16), pltpu.VMEM((d, de), BF16), pltpu.VMEM((de, d), BF16),
            pltpu.SemaphoreType.DMA((2,)),
        ],
    )
    return pl.pallas_call(
        _expert_kernel,
        grid_spec=grid_spec,
        out_shape=jax.ShapeDtypeStruct((n_rows, c, l), BF16),
        compiler_params=_cparams(("arbitrary",)),
        name="moe_experts",
    )(block_expert, meta, slot_e, nxt_e, rows, w1, w3, w2)


def _combine_kernel(dcur_ref, dnext_ref, h_ref, route_ref, g_ref, y_ref, o_ref, ybuf, sems):
    i = pl.program_id(0)
    nt = pl.num_programs(0)
    tm, d = h_ref.shape
    slot = i % 2

    def gather(dest_ref, s):
        def issue(r, carry):
            for k in range(2):
                pltpu.make_async_copy(
                    y_ref.at[dest_ref[k, r]], ybuf.at[s, k, r], sems.at[s]).start(priority=k)
            return carry
        lax.fori_loop(0, tm, issue, 0, unroll=ISSUE_UNROLL)

    @pl.when(i == 0)
    def _():
        gather(dcur_ref, slot)

    @pl.when(i + 1 < nt)
    def _():
        gather(dnext_ref, 1 - slot)

    for k in range(2):
        pltpu.make_async_copy(y_ref.at[pl.ds(0, tm)], ybuf.at[slot, k], sems.at[slot]).wait()

    route = route_ref[...]
    w1 = route[:, 2:3]
    w2 = route[:, 3:4]
    ya = _from_row_tiles(ybuf[slot, 0]).astype(F32)
    yb = _from_row_tiles(ybuf[slot, 1]).astype(F32)
    h = h_ref[...] + w1 * ya + w2 * yb
    ms = jnp.mean(h * h, axis=-1, keepdims=True)
    o_ref[...] = h * lax.rsqrt(ms + EPS) * g_ref[...]


def _combine(h, route, dest, y_rows, g, tm):
    t, d = h.shape
    nt = t // tm
    dest_spec = lambda f: pl.BlockSpec((None, 2, tm), f, memory_space=pltpu.SMEM)
    return pl.pallas_call(
        _combine_kernel,
        grid=(nt,),
        in_specs=[
            dest_spec(lambda i: (i, 0, 0)),
            dest_spec(lambda i: (jnp.minimum(i + 1, nt - 1), 0, 0)),
            pl.BlockSpec((tm, d), lambda i: (i, 0)),
            pl.BlockSpec((tm, ROUTE_LANES), lambda i: (i, 0)),
            pl.BlockSpec((1, d), lambda i: (0, 0)),
            pl.BlockSpec(memory_space=pl.ANY),
        ],
        out_specs=pl.BlockSpec((tm, d), lambda i: (i, 0)),
        out_shape=jax.ShapeDtypeStruct((t, d), F32),
        scratch_shapes=[pltpu.VMEM((2, 2, tm, d // LANES, LANES), BF16),
                        pltpu.SemaphoreType.DMA((2,))],
        compiler_params=_cparams(("arbitrary",)),
        name="moe_combine",
    )(dest, dest, h, route, g, y_rows)


def kernel(x, w_in, attn_sinks, hgrn_lb_logits, hgrn_norm_g, w_br_attn, w_br_hgrn, w_out,
           mix_norm_g, moe_norm_g, w_router_group, b_router_group, w_router_expert,
           b_router_expert, w1, w3, w2, final_norm_g):
    b, s, d = x.shape
    t = b * s
    depth = w_in.shape[0]
    assert depth == 1
    attn_w = (d // 128) * ATTN_HEAD_DIM
    kv_w = attn_w // Q_PER_KV
    hg_w = (d // 256) * HGRN_HEAD_DIM
    assert kv_w == N_KV_HEADS * ATTN_HEAD_DIM

    src = {}
    off = 0
    for name, width in (("qa", attn_w), ("k", kv_w), ("v", kv_w), ("qh", hg_w), ("f", hg_w),
                        ("i", hg_w), ("og", hg_w), ("g0", d), ("g1", d)):
        src[name] = (off, width)
        off += width
    order = ("g0", "g1", "qa", "qh", "f", "i", "og", "k", "v")
    dst = {}
    off = 0
    for name in order:
        dst[name] = off
        off += src[name][1]
    n_in = off
    l = 0
    w_perm = jnp.concatenate(
        [w_in[l][:, src[n][0]:src[n][0] + src[n][1]] for n in order], axis=1).astype(BF16)

    lb = jnp.cumsum(jax.nn.softmax(hgrn_lb_logits.astype(F32), axis=0), axis=0)[l].reshape(1, hg_w)

    x2 = x.reshape(t, d)
    tm_in = min(2048, t)
    proj = _in_proj(x2, mix_norm_g[l].reshape(1, d), w_perm, tm_in, 512)
    proj3 = proj.reshape(b, s, n_in)

    o_a = _attention(proj3, attn_sinks[l].astype(F32), dst["qa"], dst["k"], dst["v"], attn_w, kv_w)
    o_h = _hgrn(proj3, lb, hgrn_norm_g[l].reshape(1, HGRN_HEAD_DIM).astype(F32),
                dst["qh"], dst["f"], dst["i"], dst["og"], hg_w, min(256, s))

    w_r = jnp.zeros((d, ROUTE_LANES), F32)
    w_r = w_r.at[:, :N_GROUPS].set(w_router_group[l])
    w_r = w_r.at[:, EXPERT_LANE0:EXPERT_LANE0 + N_EXPERTS].set(w_router_expert[l])
    wr_hi = w_r.astype(BF16)
    wr_lo = (w_r - wr_hi.astype(F32)).astype(BF16)
    b_r = jnp.zeros((1, ROUTE_LANES), F32)
    b_r = b_r.at[0, :N_GROUPS].set(b_router_group[l])
    b_r = b_r.at[0, EXPERT_LANE0:EXPERT_LANE0 + N_EXPERTS].set(b_router_expert[l])

    tm = min(256, t)
    h, hn, route, plan, cnt = _merge_route(
        o_a.reshape(t, attn_w), o_h.reshape(t, hg_w), proj, x2,
        w_br_attn[l].astype(BF16), w_br_hgrn[l].astype(BF16), w_out[l].astype(BF16),
        moe_norm_g[l].reshape(1, d), wr_hi, wr_lo, b_r, dst["g0"], dst["g1"], tm)

    bm = MOE_ROWS
    counts = cnt[0, EXPERT_LANE0:EXPERT_LANE0 + N_EXPERTS].astype(jnp.int32)
    padded = ((counts + bm - 1) // bm) * bm
    pad_end = jnp.cumsum(padded)
    pad_start = (pad_end - padded).astype(jnp.int32)
    n_blocks = (2 * t) // bm + N_EXPERTS
    n_used = pad_end[-1] // bm
    blk_ids = jnp.minimum(jnp.arange(n_blocks, dtype=jnp.int32), n_used - 1)
    block_expert = jnp.minimum(
        jnp.sum((pad_end[None, :] <= (blk_ids * bm)[:, None]).astype(jnp.int32), axis=1),
        N_EXPERTS - 1)
    meta = jnp.stack([n_used, n_used]).astype(jnp.int32)

    blk_all = jnp.arange(n_blocks, dtype=jnp.int32)
    fill = jnp.concatenate([
        jnp.maximum(pad_end - bm, 0).astype(jnp.int32),
        jnp.minimum(n_used + blk_all, n_blocks - 1) * bm])
    nfill = (N_EXPERTS + n_blocks - n_used).astype(jnp.int32).reshape(1)

    eid = jnp.arange(N_EXPERTS, dtype=jnp.int32)
    nonempty = counts > 0
    slot_e = ((jnp.cumsum(nonempty.astype(jnp.int32)) - 1) % 2).astype(jnp.int32)
    later = jnp.logical_and(nonempty[None, :], eid[None, :] > eid[:, None])
    nxt_e = jnp.min(jnp.where(later, eid[None, :], N_EXPERTS), axis=1)
    nxt_e = jnp.where(nxt_e == N_EXPERTS, -1, nxt_e).astype(jnp.int32)

    nt = t // tm
    dest = _dest_rows(plan, pad_start, min(8, nt))
    rows = _dispatch(hn, dest, fill, nfill, n_blocks * bm, tm, bm)
    y_rows = _experts(rows, block_expert, meta, slot_e, nxt_e, w1[l], w3[l], w2[l], bm)
    out = _combine(h, route, dest, y_rows, final_norm_g.reshape(1, d), tm)
    return out.reshape(b, s, d)
```

```python
import functools

import jax
import jax.numpy as jnp
from jax import lax
from jax.experimental import pallas as pl
from jax.experimental.pallas import tpu as pltpu

F32 = jnp.float32
BF16 = jnp.bfloat16

EPS = 1e-6
ATTN_HEAD_DIM = 64
Q_PER_KV = 4
N_KV_HEADS = 4
ATTN_BLOCK = 128
HGRN_HEAD_DIM = 128
HGRN_CHUNK = 64
N_GROUPS = 4
EXPERTS_PER_GROUP = 8
N_EXPERTS = N_GROUPS * EXPERTS_PER_GROUP
LANES = 128
ROUTE_LANES = LANES
EXPERT_LANE0 = N_GROUPS
MOE_ROWS = 256
PLAN_ROWS = 8
PLAN_E, PLAN_R = 0, 4
VMEM_LIMIT = 56 * 1024 * 1024


def _sigmoid(x):
    return 1.0 / (1.0 + jnp.exp(-x))


def _cparams(sem, vmem=VMEM_LIMIT):
    return pltpu.CompilerParams(dimension_semantics=sem, vmem_limit_bytes=vmem)


def _to_row_tiles(v):
    m, d = v.shape
    return v.reshape(m, d // LANES, LANES).astype(BF16)


def _from_row_tiles(p):
    m, c, l = p.shape
    return p.reshape(m, c * l)


def _inproj_kernel(cols_ref, x_ref, g_ref, w_ref, o_ref, xn_ref):
    del cols_ref

    @pl.when(pl.program_id(1) == 0)
    def _():
        x = x_ref[...]
        ms = jnp.mean(x * x, axis=-1, keepdims=True)
        xn_ref[...] = (x * lax.rsqrt(ms + EPS) * g_ref[...]).astype(BF16)

    o_ref[...] = jnp.dot(xn_ref[...], w_ref[...], preferred_element_type=F32).astype(o_ref.dtype)


def _in_proj(x2, g, w, col_blocks, tm, tn):
    t, d = x2.shape
    n = w.shape[1]
    grid_spec = pltpu.PrefetchScalarGridSpec(
        num_scalar_prefetch=1,
        grid=(t // tm, n // tn),
        in_specs=[
            pl.BlockSpec((tm, d), lambda i, j, cols: (i, 0)),
            pl.BlockSpec((1, d), lambda i, j, cols: (0, 0)),
            pl.BlockSpec((d, tn), lambda i, j, cols: (0, cols[j])),
        ],
        out_specs=pl.BlockSpec((tm, tn), lambda i, j, cols: (i, j)),
        scratch_shapes=[pltpu.VMEM((tm, d), BF16)],
    )
    return pl.pallas_call(
        _inproj_kernel,
        grid_spec=grid_spec,
        out_shape=jax.ShapeDtypeStruct((t, n), BF16),
        compiler_params=_cparams(("arbitrary", "arbitrary")),
        name="in_proj",
    )(col_blocks, x2, g, w)


def _attn_kernel(sink_ref, q_ref, kc_ref, kp_ref, vc_ref, vp_ref, o_ref):
    n = pl.program_id(1)
    blk, dh, g = ATTN_BLOCK, ATTN_HEAD_DIM, Q_PER_KV
    rows = g * blk
    qi = lax.broadcasted_iota(jnp.int32, (rows, blk), 0) % blk
    kj = lax.broadcasted_iota(jnp.int32, (rows, blk), 1)
    mask_cur = kj <= qi
    prev_bias = jnp.where(n > 0, 0.0, -jnp.inf)
    head_of_row = lax.broadcasted_iota(jnp.int32, (rows, 1), 0) // blk
    nt = (((1,), (1,)), ((), ()))
    scale = dh ** -0.5
    for h in range(N_KV_HEADS):
        kc = kc_ref[:, h * dh:(h + 1) * dh]
        kp = kp_ref[:, h * dh:(h + 1) * dh]
        vc = vc_ref[:, h * dh:(h + 1) * dh]
        vp = vp_ref[:, h * dh:(h + 1) * dh]
        q4 = jnp.concatenate(
            [q_ref[:, (h * g + j) * dh:(h * g + j + 1) * dh] for j in range(g)], axis=0)
        sc = lax.dot_general(q4, kc, nt, preferred_element_type=F32)
        sp = lax.dot_general(q4, kp, nt, preferred_element_type=F32)
        s = jnp.where(mask_cur, sc, sp + prev_bias) * scale
        sink = jnp.zeros((rows, 1), F32)
        for j in range(g):
            sink = jnp.where(head_of_row == j, sink_ref[h * g + j], sink)
        m = jnp.maximum(jnp.max(s, axis=-1, keepdims=True), sink)
        p = jnp.exp(s - m)
        den = jnp.sum(p, axis=-1, keepdims=True) + jnp.exp(sink - m)
        p_cur = jnp.where(mask_cur, p, 0.0).astype(BF16)
        p_prev = jnp.where(mask_cur, 0.0, p).astype(BF16)
        o = (jnp.dot(p_cur, vc, preferred_element_type=F32)
             + jnp.dot(p_prev, vp, preferred_element_type=F32)) / den
        for j in range(g):
            o_ref[:, (h * g + j) * dh:(h * g + j + 1) * dh] = (
                o[j * blk:(j + 1) * blk, :].astype(o_ref.dtype))


def _attention(proj3, sinks, off_q, off_k, off_v, attn_w, kv_w):
    b, s, _ = proj3.shape
    blk = ATTN_BLOCK
    qb, kb, vb = off_q // attn_w, off_k // kv_w, off_v // kv_w
    prev = lambda n: jnp.maximum(n - 1, 0)
    return pl.pallas_call(
        _attn_kernel,
        grid=(b, s // blk),
        in_specs=[
            pl.BlockSpec(memory_space=pltpu.SMEM),
            pl.BlockSpec((None, blk, attn_w), lambda i, n: (i, n, qb)),
            pl.BlockSpec((None, blk, kv_w), lambda i, n: (i, n, kb)),
            pl.BlockSpec((None, blk, kv_w), lambda i, n: (i, prev(n), kb)),
            pl.BlockSpec((None, blk, kv_w), lambda i, n: (i, n, vb)),
            pl.BlockSpec((None, blk, kv_w), lambda i, n: (i, prev(n), vb)),
        ],
        out_specs=pl.BlockSpec((None, blk, attn_w), lambda i, n: (i, n, 0)),
        out_shape=jax.ShapeDtypeStruct((b, s, attn_w), BF16),
        compiler_params=_cparams(("arbitrary", "arbitrary")),
        name="swa_attention",
    )(sinks, proj3, proj3, proj3, proj3, proj3)


def _hgrn_kernel(q_ref, f_ref, i_ref, og_ref, lb_ref, gn_ref, o_ref, st_ref, *, n_heads, n_chunks):
    c, dk = HGRN_CHUNK, HGRN_HEAD_DIM

    @pl.when(pl.program_id(1) == 0)
    def _():
        st_ref[...] = jnp.zeros_like(st_ref)

    ti = lax.broadcasted_iota(jnp.int32, (c, c), 0)
    si = lax.broadcasted_iota(jnp.int32, (c, c), 1)
    causal = si <= ti
    tri = causal.astype(BF16)
    nt = (((1,), (1,)), ((), ()))
    tn = (((0,), (0,)), ((), ()))
    qscale = dk ** -0.5

    def chunk(ci, carry):
        rows = pl.ds(pl.multiple_of(ci * c, c), c)
        for h in range(n_heads):
            cols = slice(h * dk, (h + 1) * dk)
            lb = lb_ref[:, cols]
            f = lb + (1.0 - lb) * _sigmoid(f_ref[rows, cols].astype(F32))
            gl = jnp.log(f)
            g_hi = gl.astype(BF16)
            g_lo = (gl - g_hi.astype(F32)).astype(BF16)
            bc = (jnp.dot(tri, g_hi, preferred_element_type=F32)
                  + jnp.dot(tri, g_lo, preferred_element_type=F32))
            b_last = bc[c - 1:c, :]
            r = 0.5 * b_last
            qv = q_ref[rows, cols].astype(F32)
            qs = qv * _sigmoid(qv) * qscale
            kk = 1.0 - f
            vv = i_ref[rows, cols]
            qt = (qs * jnp.exp(bc - r)).astype(BF16)
            kt = (kk * jnp.exp(r - bc)).astype(BF16)
            a = lax.dot_general(qt, kt, nt, preferred_element_type=F32)
            a = jnp.where(causal, a, 0.0).astype(BF16)
            o = jnp.dot(a, vv, preferred_element_type=F32)
            st = st_ref[h]
            qe = (qs * jnp.exp(bc)).astype(BF16)
            o = o + lax.dot_general(qe, st.astype(BF16), nt, preferred_element_type=F32)
            kd = (kk * jnp.exp(b_last - bc)).astype(BF16)
            st_ref[h] = st * jnp.exp(b_last) + lax.dot_general(
                vv, kd, tn, preferred_element_type=F32)
            ms = jnp.mean(o * o, axis=-1, keepdims=True)
            on = o * lax.rsqrt(ms + EPS) * gn_ref[...]
            ogv = og_ref[rows, cols].astype(F32)
            o_ref[rows, cols] = (on * (ogv * _sigmoid(ogv))).astype(o_ref.dtype)
        return carry

    lax.fori_loop(0, n_chunks, chunk, 0)


def _hgrn(proj3, lb, gn, off_q, off_f, off_i, off_og, width, ts):
    b, s, _ = proj3.shape
    n_heads = width // HGRN_HEAD_DIM
    spec = lambda off: pl.BlockSpec((None, ts, width), lambda i, t: (i, t, off // width))
    return pl.pallas_call(
        functools.partial(_hgrn_kernel, n_heads=n_heads, n_chunks=ts // HGRN_CHUNK),
        grid=(b, s // ts),
        in_specs=[
            spec(off_q), spec(off_f), spec(off_i), spec(off_og),
            pl.BlockSpec((1, width), lambda i, t: (0, 0)),
            pl.BlockSpec((1, HGRN_HEAD_DIM), lambda i, t: (0, 0)),
        ],
        out_specs=pl.BlockSpec((None, ts, width), lambda i, t: (i, t, 0)),
        out_shape=jax.ShapeDtypeStruct((b, s, width), BF16),
        scratch_shapes=[pltpu.VMEM((n_heads, HGRN_HEAD_DIM, HGRN_HEAD_DIM), F32)],
        compiler_params=_cparams(("arbitrary", "arbitrary")),
        name="hgrn2",
    )(proj3, proj3, proj3, proj3, lb, gn)


def _merge_kernel(oa_ref, oh_ref, g0_ref, g1_ref, x_ref, wa_ref, wh_ref, wo_ref, gm_ref,
                  wrh_ref, wrl_ref, br_ref, h_ref, hn_ref, route_ref, plan_ref, cnt_ref):
    tm = x_ref.shape[0]

    @pl.when(pl.program_id(0) == 0)
    def _():
        cnt_ref[...] = jnp.zeros_like(cnt_ref)

    ya = jnp.dot(oa_ref[...], wa_ref[...], preferred_element_type=F32)
    yh = jnp.dot(oh_ref[...], wh_ref[...], preferred_element_type=F32)
    merged = (_sigmoid(g0_ref[...].astype(F32)) * ya
              + _sigmoid(g1_ref[...].astype(F32)) * yh).astype(BF16)
    h = x_ref[...] + jnp.dot(merged, wo_ref[...], preferred_element_type=F32)
    h_ref[...] = h
    ms = jnp.mean(h * h, axis=-1, keepdims=True)
    hn = h * lax.rsqrt(ms + EPS) * gm_ref[...]
    hn_ref[...] = _to_row_tiles(hn)

    hn_hi = hn.astype(BF16)
    hn_lo = (hn - hn_hi.astype(F32)).astype(BF16)
    logits = (jnp.dot(hn_hi, wrh_ref[...], preferred_element_type=F32)
              + jnp.dot(hn_hi, wrl_ref[...], preferred_element_type=F32)
              + jnp.dot(hn_lo, wrh_ref[...], preferred_element_type=F32)
              + br_ref[...])
    lane = lax.broadcasted_iota(jnp.int32, (tm, ROUTE_LANES), 1)
    neg = -jnp.inf

    def first_argmax(v):
        m = jnp.max(v, axis=-1, keepdims=True)
        idx = jnp.min(jnp.where(v == m, lane, ROUTE_LANES), axis=-1, keepdims=True)
        return m, idx

    is_group = lane < N_GROUPS
    gmax, gidx = first_argmax(jnp.where(is_group, logits, neg))
    p_sel = 1.0 / jnp.sum(jnp.where(is_group, jnp.exp(logits - gmax), 0.0), axis=-1, keepdims=True)
    eidx = lane - EXPERT_LANE0
    in_group = jnp.logical_and(
        jnp.logical_and(eidx >= 0, eidx < N_EXPERTS), (eidx // EXPERTS_PER_GROUP) == gidx)
    el = jnp.where(in_group, logits, neg)
    m1, i1 = first_argmax(el)
    m2, i2 = first_argmax(jnp.where(lane == i1, neg, el))
    t = jnp.exp(m2 - m1)
    w1 = p_sel / (1.0 + t)
    w2 = p_sel * t / (1.0 + t)

    sel1 = lane == i1
    sel2 = lane == i2
    onehot = jnp.logical_or(sel1, sel2).astype(BF16)
    ri = lax.broadcasted_iota(jnp.int32, (tm, tm), 0)
    ci = lax.broadcasted_iota(jnp.int32, (tm, tm), 1)
    before = (ci < ri).astype(BF16)
    cum = jnp.dot(before, onehot, preferred_element_type=F32) + cnt_ref[...]
    r1 = jnp.sum(jnp.where(sel1, cum, 0.0), axis=-1, keepdims=True)
    r2 = jnp.sum(jnp.where(sel2, cum, 0.0), axis=-1, keepdims=True)
    cnt_ref[...] += jnp.sum(onehot.astype(F32), axis=0, keepdims=True)

    e1 = (i1 - EXPERT_LANE0).astype(F32)
    e2 = (i2 - EXPERT_LANE0).astype(F32)
    out = jnp.zeros((tm, ROUTE_LANES), F32)
    for k, v in enumerate((e1, e2, w1, w2, r1, r2)):
        out = jnp.where(lane == k, v, out)
    route_ref[...] = out
    plan_ref[...] = out.T[:PLAN_ROWS, :].astype(jnp.int32)


def _merge_route(o_a, o_h, proj, x2, wa, wh, wo, gm, wr_hi, wr_lo, br, off_g0, off_g1, tm):
    t, d = x2.shape
    aw, hw = o_a.shape[1], o_h.shape[1]
    const = lambda shape: pl.BlockSpec(shape, lambda i: (0, 0), pipeline_mode=pl.Buffered(1))
    row = lambda w: pl.BlockSpec((tm, w), lambda i: (i, 0))
    return pl.pallas_call(
        _merge_kernel,
        grid=(t // tm,),
        in_specs=[
            row(aw), row(hw),
            pl.BlockSpec((tm, d), lambda i: (i, off_g0 // d)),
            pl.BlockSpec((tm, d), lambda i: (i, off_g1 // d)),
            row(d),
            const((aw, d)), const((hw, d)), const((d, d)), const((1, d)),
            const((d, ROUTE_LANES)), const((d, ROUTE_LANES)), const((1, ROUTE_LANES)),
        ],
        out_specs=[row(d), pl.BlockSpec((tm, d // LANES, LANES), lambda i: (i, 0, 0)),
                   row(ROUTE_LANES),
                   pl.BlockSpec((None, PLAN_ROWS, tm), lambda i: (i, 0, 0)),
                   pl.BlockSpec((1, ROUTE_LANES), lambda i: (0, 0))],
        out_shape=[
            jax.ShapeDtypeStruct((t, d), F32),
            jax.ShapeDtypeStruct((t, d // LANES, LANES), BF16),
            jax.ShapeDtypeStruct((t, ROUTE_LANES), F32),
            jax.ShapeDtypeStruct((t // tm, PLAN_ROWS, tm), jnp.int32),
            jax.ShapeDtypeStruct((1, ROUTE_LANES), F32),
        ],
        compiler_params=_cparams(("arbitrary",)),
        name="merge_route",
    )(o_a, o_h, proj, proj, x2, wa, wh, wo, gm, wr_hi, wr_lo, br)


def _dest_kernel(pst_ref, plan_ref, dest_ref):
    e = plan_ref[:, PLAN_E:PLAN_E + 2, :]
    start = jnp.zeros(e.shape, jnp.int32)
    for j in range(N_EXPERTS):
        start = jnp.where(e == j, pst_ref[j], start)
    dest_ref[...] = start + plan_ref[:, PLAN_R:PLAN_R + 2, :]


def _dest_rows(plan, pst, tiles):
    nt, _, tm = plan.shape
    grid_spec = pltpu.PrefetchScalarGridSpec(
        num_scalar_prefetch=1,
        grid=(nt // tiles,),
        in_specs=[pl.BlockSpec((tiles, PLAN_ROWS, tm), lambda i, *_: (i, 0, 0))],
        out_specs=pl.BlockSpec((tiles, 2, tm), lambda i, *_: (i, 0, 0)),
    )
    return pl.pallas_call(
        _dest_kernel,
        grid_spec=grid_spec,
        out_shape=jax.ShapeDtypeStruct((nt, 2, tm), jnp.int32),
        compiler_params=_cparams(("arbitrary",)),
        name="moe_dest",
    )(pst, plan)


ISSUE_UNROLL = 8


def _dispatch_kernel(fill_ref, nfill_ref, dest_ref, hn_ref, rows_ref, zbuf, sem):
    tm = hn_ref.shape[0]
    bm = zbuf.shape[0]

    @pl.when(pl.program_id(0) == 0)
    def _():
        zbuf[...] = jnp.zeros_like(zbuf)

        def fill(j, carry):
            start = pl.multiple_of(fill_ref[j], bm)
            pltpu.make_async_copy(zbuf, rows_ref.at[pl.ds(start, bm)], sem).start()
            return carry

        def drain(j, carry):
            pltpu.make_async_copy(zbuf, rows_ref.at[pl.ds(0, bm)], sem).wait()
            return carry

        lax.fori_loop(0, nfill_ref[0], fill, 0)
        lax.fori_loop(0, nfill_ref[0], drain, 0)

    def issue(r, carry):
        for k in range(2):
            pltpu.make_async_copy(
                hn_ref.at[r], rows_ref.at[dest_ref[k, r]], sem).start(priority=k)
        return carry

    lax.fori_loop(0, tm, issue, 0, unroll=ISSUE_UNROLL)
    for _ in range(2):
        pltpu.make_async_copy(hn_ref, rows_ref.at[pl.ds(0, tm)], sem).wait()


def _dispatch(hn, dest, fill, nfill, n_rows, tm, bm):
    t, c, l = hn.shape
    grid_spec = pltpu.PrefetchScalarGridSpec(
        num_scalar_prefetch=2,
        grid=(t // tm,),
        in_specs=[
            pl.BlockSpec((None, 2, tm), lambda i, *_: (i, 0, 0), memory_space=pltpu.SMEM),
            pl.BlockSpec((tm, c, l), lambda i, *_: (i, 0, 0)),
        ],
        out_specs=pl.BlockSpec(memory_space=pl.ANY),
        scratch_shapes=[pltpu.VMEM((bm, c, l), hn.dtype), pltpu.SemaphoreType.DMA],
    )
    return pl.pallas_call(
        _dispatch_kernel,
        grid_spec=grid_spec,
        out_shape=jax.ShapeDtypeStruct((n_rows, c, l), hn.dtype),
        compiler_params=_cparams(("arbitrary",)),
        name="moe_dispatch",
    )(fill, nfill, dest, hn)


def _expert_kernel(be_ref, meta_ref, slot_ref, nxt_ref, x_ref, w1_hbm, w3_hbm, w2_hbm, y_ref,
                   wf1, wf3, wf2, w1b, w3b, w2b, sems):
    i = pl.program_id(0)
    e = be_ref[i]
    new_expert = jnp.logical_or(i == 0, e != be_ref[jnp.maximum(i - 1, 0)])

    def weight_copies(expert, slot):
        return [pltpu.make_async_copy(src.at[expert], dst.at[slot], sems.at[slot])
                for src, dst in ((w1_hbm, wf1), (w3_hbm, wf3), (w2_hbm, wf2))]

    @pl.when(jnp.logical_and(i < meta_ref[0], new_expert))
    def _():
        slot = slot_ref[e]
        nxt = nxt_ref[e]

        @pl.when(i == 0)
        def _():
            for c in weight_copies(e, slot):
                c.start()

        @pl.when(nxt >= 0)
        def _():
            for c in weight_copies(nxt, 1 - slot):
                c.start()

        for c in weight_copies(e, slot):
            c.wait()
        w1b[...] = wf1[slot].astype(BF16)
        w3b[...] = wf3[slot].astype(BF16)
        w2b[...] = wf2[slot].astype(BF16)

    @pl.when(i < meta_ref[0])
    def _():
        x = _from_row_tiles(x_ref[...])
        h1 = jnp.dot(x, w1b[...], preferred_element_type=F32)
        h3 = jnp.dot(x, w3b[...], preferred_element_type=F32)
        hb = (h1 * _sigmoid(h1) * h3).astype(BF16)
        y_ref[...] = _to_row_tiles(jnp.dot(hb, w2b[...], preferred_element_type=F32))

    @pl.when(i >= meta_ref[0])
    def _():
        y_ref[...] = jnp.zeros_like(y_ref)


def _experts(rows, block_expert, meta, slot_e, nxt_e, w1, w3, w2, bm):
    n_rows, c, l = rows.shape
    d = c * l
    de = w1.shape[-1]
    n_blocks = n_rows // bm
    hbm = pl.BlockSpec(memory_space=pl.ANY)
    grid_spec = pltpu.PrefetchScalarGridSpec(
        num_scalar_prefetch=4,
        grid=(n_blocks,),
        in_specs=[
            pl.BlockSpec((bm, c, l), lambda i, be, meta, *_: (jnp.minimum(i, meta[0] - 1), 0, 0)),
            hbm, hbm, hbm,
        ],
        out_specs=pl.BlockSpec((bm, c, l), lambda i, *_: (i, 0, 0)),
        scratch_shapes=[
            pltpu.VMEM((2, d, de), w1.dtype), pltpu.VMEM((2, d, de), w3.dtype),
            pltpu.VMEM((2, de, d), w2.dtype),
            pltpu.VMEM((d, de), BF16), pltpu.VMEM((d, de), BF16), pltpu.VMEM((de, d), BF16),
            pltpu.SemaphoreType.DMA((2,)),
        ],
    )
    return pl.pallas_call(
        _expert_kernel,
        grid_spec=grid_spec,
        out_shape=jax.ShapeDtypeStruct((n_rows, c, l), BF16),
        compiler_params=_cparams(("arbitrary",)),
        name="moe_experts",
    )(block_expert, meta, slot_e, nxt_e, rows, w1, w3, w2)


def _combine_kernel(dcur_ref, dnext_ref, h_ref, route_ref, g_ref, y_ref, o_ref, ybuf, sems):
    i = pl.program_id(0)
    nt = pl.num_programs(0)
    tm, d = h_ref.shape
    slot = i % 2

    def gather(dest_ref, s):
        def issue(r, carry):
            for k in range(2):
                pltpu.make_async_copy(
                    y_ref.at[dest_ref[k, r]], ybuf.at[s, k, r], sems.at[s]).start(priority=k)
            return carry
        lax.fori_loop(0, tm, issue, 0, unroll=ISSUE_UNROLL)

    @pl.when(i == 0)
    def _():
        gather(dcur_ref, slot)

    @pl.when(i + 1 < nt)
    def _():
        gather(dnext_ref, 1 - slot)

    for k in range(2):
        pltpu.make_async_copy(y_ref.at[pl.ds(0, tm)], ybuf.at[slot, k], sems.at[slot]).wait()

    route = route_ref[...]
    w1 = route[:, 2:3]
    w2 = route[:, 3:4]
    ya = _from_row_tiles(ybuf[slot, 0]).astype(F32)
    yb = _from_row_tiles(ybuf[slot, 1]).astype(F32)
    h = h_ref[...] + w1 * ya + w2 * yb
    ms = jnp.mean(h * h, axis=-1, keepdims=True)
    o_ref[...] = h * lax.rsqrt(ms + EPS) * g_ref[...]


def _combine(h, route, dest, y_rows, g, tm):
    t, d = h.shape
    nt = t // tm
    dest_spec = lambda f: pl.BlockSpec((None, 2, tm), f, memory_space=pltpu.SMEM)
    return pl.pallas_call(
        _combine_kernel,
        grid=(nt,),
        in_specs=[
            dest_spec(lambda i: (i, 0, 0)),
            dest_spec(lambda i: (jnp.minimum(i + 1, nt - 1), 0, 0)),
            pl.BlockSpec((tm, d), lambda i: (i, 0)),
            pl.BlockSpec((tm, ROUTE_LANES), lambda i: (i, 0)),
            pl.BlockSpec((1, d), lambda i: (0, 0)),
            pl.BlockSpec(memory_space=pl.ANY),
        ],
        out_specs=pl.BlockSpec((tm, d), lambda i: (i, 0)),
        out_shape=jax.ShapeDtypeStruct((t, d), F32),
        scratch_shapes=[pltpu.VMEM((2, 2, tm, d // LANES, LANES), BF16),
                        pltpu.SemaphoreType.DMA((2,))],
        compiler_params=_cparams(("arbitrary",)),
        name="moe_combine",
    )(dest, dest, h, route, g, y_rows)


def kernel(x, w_in, attn_sinks, hgrn_lb_logits, hgrn_norm_g, w_br_attn, w_br_hgrn, w_out,
           mix_norm_g, moe_norm_g, w_router_group, b_router_group, w_router_expert,
           b_router_expert, w1, w3, w2, final_norm_g):
    b, s, d = x.shape
    t = b * s
    depth = w_in.shape[0]
    assert depth == 1
    attn_w = (d // 128) * ATTN_HEAD_DIM
    kv_w = attn_w // Q_PER_KV
    hg_w = (d // 256) * HGRN_HEAD_DIM
    assert kv_w == N_KV_HEADS * ATTN_HEAD_DIM

    src = {}
    off = 0
    for name, width in (("qa", attn_w), ("k", kv_w), ("v", kv_w), ("qh", hg_w), ("f", hg_w),
                        ("i", hg_w), ("og", hg_w), ("g0", d), ("g1", d)):
        src[name] = (off, width)
        off += width
    order = ("g0", "g1", "qa", "qh", "f", "i", "og", "k", "v")
    dst = {}
    off = 0
    for name in order:
        dst[name] = off
        off += src[name][1]
    n_in = off
    l = 0
    tn_in = 512
    assert all(src[n][0] % tn_in == 0 for n in order if n != "v") and (2 * kv_w) % tn_in == 0
    col_blocks = []
    for n in order[:-1]:
        width = src[n][1] if n != "k" else 2 * kv_w
        col_blocks += [src[n][0] // tn_in + c for c in range(width // tn_in)]
    col_blocks = jnp.asarray(col_blocks, jnp.int32)
    w_bf = w_in[l].astype(BF16)

    lb = jnp.cumsum(jax.nn.softmax(hgrn_lb_logits.astype(F32), axis=0), axis=0)[l].reshape(1, hg_w)

    x2 = x.reshape(t, d)
    tm_in = min(2048, t)
    proj = _in_proj(x2, mix_norm_g[l].reshape(1, d), w_bf, col_blocks, tm_in, tn_in)
    proj3 = proj.reshape(b, s, n_in)

    o_a = _attention(proj3, attn_sinks[l].astype(F32), dst["qa"], dst["k"], dst["v"], attn_w, kv_w)
    o_h = _hgrn(proj3, lb, hgrn_norm_g[l].reshape(1, HGRN_HEAD_DIM).astype(F32),
                dst["qh"], dst["f"], dst["i"], dst["og"], hg_w, min(256, s))

    w_r = jnp.zeros((d, ROUTE_LANES), F32)
    w_r = w_r.at[:, :N_GROUPS].set(w_router_group[l])
    w_r = w_r.at[:, EXPERT_LANE0:EXPERT_LANE0 + N_EXPERTS].set(w_router_expert[l])
    wr_hi = w_r.astype(BF16)
    wr_lo = (w_r - wr_hi.astype(F32)).astype(BF16)
    b_r = jnp.zeros((1, ROUTE_LANES), F32)
    b_r = b_r.at[0, :N_GROUPS].set(b_router_group[l])
    b_r = b_r.at[0, EXPERT_LANE0:EXPERT_LANE0 + N_EXPERTS].set(b_router_expert[l])

    tm = min(256, t)
    h, hn, route, plan, cnt = _merge_route(
        o_a.reshape(t, attn_w), o_h.reshape(t, hg_w), proj, x2,
        w_br_attn[l].astype(BF16), w_br_hgrn[l].astype(BF16), w_out[l].astype(BF16),
        moe_norm_g[l].reshape(1, d), wr_hi, wr_lo, b_r, dst["g0"], dst["g1"], tm)

    bm = MOE_ROWS
    counts = cnt[0, EXPERT_LANE0:EXPERT_LANE0 + N_EXPERTS].astype(jnp.int32)
    padded = ((counts + bm - 1) // bm) * bm
    pad_end = jnp.cumsum(padded)
    pad_start = (pad_end - padded).astype(jnp.int32)
    n_blocks = (2 * t) // bm + N_EXPERTS
    n_used = pad_end[-1] // bm
    blk_ids = jnp.minimum(jnp.arange(n_blocks, dtype=jnp.int32), n_used - 1)
    block_expert = jnp.minimum(
        jnp.sum((pad_end[None, :] <= (blk_ids * bm)[:, None]).astype(jnp.int32), axis=1),
        N_EXPERTS - 1)
    meta = jnp.stack([n_used, n_used]).astype(jnp.int32)

    blk_all = jnp.arange(n_blocks, dtype=jnp.int32)
    fill = jnp.concatenate([
        jnp.maximum(pad_end - bm, 0).astype(jnp.int32),
        jnp.minimum(n_used + blk_all, n_blocks - 1) * bm])
    nfill = (N_EXPERTS + n_blocks - n_used).astype(jnp.int32).reshape(1)

    eid = jnp.arange(N_EXPERTS, dtype=jnp.int32)
    nonempty = counts > 0
    slot_e = ((jnp.cumsum(nonempty.astype(jnp.int32)) - 1) % 2).astype(jnp.int32)
    later = jnp.logical_and(nonempty[None, :], eid[None, :] > eid[:, None])
    nxt_e = jnp.min(jnp.where(later, eid[None, :], N_EXPERTS), axis=1)
    nxt_e = jnp.where(nxt_e == N_EXPERTS, -1, nxt_e).astype(jnp.int32)

    nt = t // tm
    dest = _dest_rows(plan, pad_start, min(8, nt))
    rows = _dispatch(hn, dest, fill, nfill, n_blocks * bm, tm, bm)
    y_rows = _experts(rows, block_expert, meta, slot_e, nxt_e, w1[l], w3[l], w2[l], bm)
    out = _combine(h, route, dest, y_rows, final_norm_g.reshape(1, d), tm)
    return out.reshape(b, s, d)
```

```python
import functools

import jax
import jax.numpy as jnp
from jax import lax
from jax.experimental import pallas as pl
from jax.experimental.pallas import tpu as pltpu

F32 = jnp.float32
BF16 = jnp.bfloat16

EPS = 1e-6
ATTN_HEAD_DIM = 64
Q_PER_KV = 4
N_KV_HEADS = 4
ATTN_BLOCK = 128
HGRN_HEAD_DIM = 128
HGRN_CHUNK = 64
N_GROUPS = 4
EXPERTS_PER_GROUP = 8
N_EXPERTS = N_GROUPS * EXPERTS_PER_GROUP
LANES = 128
ROUTE_LANES = LANES
EXPERT_LANE0 = N_GROUPS
MOE_ROWS = 256
PLAN_ROWS = 8
PLAN_E, PLAN_R = 0, 4
VMEM_LIMIT = 56 * 1024 * 1024


def _sigmoid(x):
    return 0.5 * jnp.tanh(0.5 * x) + 0.5


def _cparams(sem, vmem=VMEM_LIMIT):
    return pltpu.CompilerParams(dimension_semantics=sem, vmem_limit_bytes=vmem)


def _to_row_tiles(v):
    m, d = v.shape
    return v.reshape(m, d // LANES, LANES).astype(BF16)


def _from_row_tiles(p):
    m, c, l = p.shape
    return p.reshape(m, c * l)


def _inproj_kernel(cols_ref, x_ref, g_ref, w_ref, o_ref, xn_ref):
    del cols_ref

    @pl.when(pl.program_id(1) == 0)
    def _():
        x = x_ref[...]
        ms = jnp.mean(x * x, axis=-1, keepdims=True)
        xn_ref[...] = (x * lax.rsqrt(ms + EPS) * g_ref[...]).astype(BF16)

    o_ref[...] = jnp.dot(xn_ref[...], w_ref[...], preferred_element_type=F32).astype(o_ref.dtype)


def _in_proj(x2, g, w, col_blocks, tm, tn):
    t, d = x2.shape
    n = w.shape[1]
    grid_spec = pltpu.PrefetchScalarGridSpec(
        num_scalar_prefetch=1,
        grid=(t // tm, n // tn),
        in_specs=[
            pl.BlockSpec((tm, d), lambda i, j, cols: (i, 0)),
            pl.BlockSpec((1, d), lambda i, j, cols: (0, 0)),
            pl.BlockSpec((d, tn), lambda i, j, cols: (0, cols[j])),
        ],
        out_specs=pl.BlockSpec((tm, tn), lambda i, j, cols: (i, j)),
        scratch_shapes=[pltpu.VMEM((tm, d), BF16)],
    )
    return pl.pallas_call(
        _inproj_kernel,
        grid_spec=grid_spec,
        out_shape=jax.ShapeDtypeStruct((t, n), BF16),
        compiler_params=_cparams(("arbitrary", "arbitrary")),
        name="in_proj",
    )(col_blocks, x2, g, w)


def _attn_kernel(sink_ref, q_ref, kc_ref, kp_ref, vc_ref, vp_ref, o_ref):
    n = pl.program_id(1)
    blk, dh, g = ATTN_BLOCK, ATTN_HEAD_DIM, Q_PER_KV
    rows = g * blk
    qi = lax.broadcasted_iota(jnp.int32, (rows, blk), 0) % blk
    kj = lax.broadcasted_iota(jnp.int32, (rows, blk), 1)
    mask_cur = kj <= qi
    prev_bias = jnp.where(n > 0, 0.0, -jnp.inf)
    head_of_row = lax.broadcasted_iota(jnp.int32, (rows, 1), 0) // blk
    nt = (((1,), (1,)), ((), ()))
    scale = dh ** -0.5
    for h in range(N_KV_HEADS):
        kc = kc_ref[:, h * dh:(h + 1) * dh]
        kp = kp_ref[:, h * dh:(h + 1) * dh]
        vc = vc_ref[:, h * dh:(h + 1) * dh]
        vp = vp_ref[:, h * dh:(h + 1) * dh]
        q4 = jnp.concatenate(
            [q_ref[:, (h * g + j) * dh:(h * g + j + 1) * dh] for j in range(g)], axis=0)
        sc = lax.dot_general(q4, kc, nt, preferred_element_type=F32)
        sp = lax.dot_general(q4, kp, nt, preferred_element_type=F32)
        s = jnp.where(mask_cur, sc, sp + prev_bias) * scale
        sink = jnp.zeros((rows, 1), F32)
        for j in range(g):
            sink = jnp.where(head_of_row == j, sink_ref[h * g + j], sink)
        m = jnp.maximum(jnp.max(s, axis=-1, keepdims=True), sink)
        p = jnp.exp(s - m)
        den = jnp.sum(p, axis=-1, keepdims=True) + jnp.exp(sink - m)
        p_cur = jnp.where(mask_cur, p, 0.0).astype(BF16)
        p_prev = jnp.where(mask_cur, 0.0, p).astype(BF16)
        o = (jnp.dot(p_cur, vc, preferred_element_type=F32)
             + jnp.dot(p_prev, vp, preferred_element_type=F32)) / den
        for j in range(g):
            o_ref[:, (h * g + j) * dh:(h * g + j + 1) * dh] = (
                o[j * blk:(j + 1) * blk, :].astype(o_ref.dtype))


def _attention(proj3, sinks, off_q, off_k, off_v, attn_w, kv_w):
    b, s, _ = proj3.shape
    blk = ATTN_BLOCK
    qb, kb, vb = off_q // attn_w, off_k // kv_w, off_v // kv_w
    prev = lambda n: jnp.maximum(n - 1, 0)
    return pl.pallas_call(
        _attn_kernel,
        grid=(b, s // blk),
        in_specs=[
            pl.BlockSpec(memory_space=pltpu.SMEM),
            pl.BlockSpec((None, blk, attn_w), lambda i, n: (i, n, qb)),
            pl.BlockSpec((None, blk, kv_w), lambda i, n: (i, n, kb)),
            pl.BlockSpec((None, blk, kv_w), lambda i, n: (i, prev(n), kb)),
            pl.BlockSpec((None, blk, kv_w), lambda i, n: (i, n, vb)),
            pl.BlockSpec((None, blk, kv_w), lambda i, n: (i, prev(n), vb)),
        ],
        out_specs=pl.BlockSpec((None, blk, attn_w), lambda i, n: (i, n, 0)),
        out_shape=jax.ShapeDtypeStruct((b, s, attn_w), BF16),
        compiler_params=_cparams(("arbitrary", "arbitrary")),
        name="swa_attention",
    )(sinks, proj3, proj3, proj3, proj3, proj3)


def _hgrn_kernel(q_ref, f_ref, i_ref, og_ref, lb_ref, gn_ref, o_ref,
                 st_ref, gh_s, f_s, b_s, qt_s, kt_s, qe_s, kd_s, dl_s, oi_s, u_s,
                 *, n_heads, n_chunks):
    c, dk = HGRN_CHUNK, HGRN_HEAD_DIM
    w = n_heads * dk

    @pl.when(pl.program_id(1) == 0)
    def _():
        st_ref[...] = jnp.zeros_like(st_ref)

    ti = lax.broadcasted_iota(jnp.int32, (c, c), 0)
    si = lax.broadcasted_iota(jnp.int32, (c, c), 1)
    causal = si <= ti
    tri = causal.astype(BF16)
    nt = (((1,), (1,)), ((), ()))
    tn = (((0,), (0,)), ((), ()))
    qscale = dk ** -0.5
    chunk_rows = [slice(ci * c, (ci + 1) * c) for ci in range(n_chunks)]
    head_cols = [slice(h * dk, (h + 1) * dk) for h in range(n_heads)]

    lb = lb_ref[...]
    f = lb + (1.0 - lb) * _sigmoid(f_ref[...].astype(F32))
    f_s[...] = f
    gl = jnp.log(f)
    g_hi = gl.astype(BF16)
    gh_s[:, :w] = g_hi
    gh_s[:, w:] = (gl - g_hi.astype(F32)).astype(BF16)

    for rows in chunk_rows:
        bcat = jnp.dot(tri, gh_s[rows, :], preferred_element_type=F32)
        b_s[rows, :] = bcat[:, :w] + bcat[:, w:]

    for ci, rows in enumerate(chunk_rows):
        bc = b_s[rows, :]
        b_last = bc[c - 1:c, :]
        r = 0.5 * b_last
        qv = q_ref[rows, :].astype(F32)
        qt = qv * _sigmoid(qv) * qscale * jnp.exp(bc - r)
        kt = (1.0 - f_s[rows, :]) * jnp.exp(r - bc)
        qt_s[rows, :] = qt.astype(BF16)
        kt_s[rows, :] = kt.astype(BF16)
        qe_s[rows, :] = (qt * jnp.exp(r)).astype(BF16)
        kd_s[rows, :] = (kt * jnp.exp(b_last - r)).astype(BF16)
        dl_s[ci:ci + 1, :] = jnp.exp(b_last)

    for ci, rows in enumerate(chunk_rows):
        for h, cols in enumerate(head_cols):
            a = lax.dot_general(qt_s[rows, cols], kt_s[rows, cols], nt,
                                preferred_element_type=F32)
            a = jnp.where(causal, a, 0.0).astype(BF16)
            vv = i_ref[rows, cols]
            oi_s[rows, cols] = jnp.dot(a, vv, preferred_element_type=F32)
            u_s[ci, h] = lax.dot_general(vv, kd_s[rows, cols], tn, preferred_element_type=F32)

    gn = gn_ref[...]
    for ci, rows in enumerate(chunk_rows):
        for h, cols in enumerate(head_cols):
            st = st_ref[h]
            o = oi_s[rows, cols] + lax.dot_general(
                qe_s[rows, cols], st.astype(BF16), nt, preferred_element_type=F32)
            st_ref[h] = st * dl_s[ci:ci + 1, cols] + u_s[ci, h]
            ms = jnp.mean(o * o, axis=-1, keepdims=True)
            ogv = og_ref[rows, cols].astype(F32)
            o_ref[rows, cols] = (
                o * lax.rsqrt(ms + EPS) * gn * (ogv * _sigmoid(ogv))).astype(o_ref.dtype)


def _hgrn(proj3, lb, gn, off_q, off_f, off_i, off_og, width, ts):
    b, s, _ = proj3.shape
    n_heads = width // HGRN_HEAD_DIM
    n_chunks = ts // HGRN_CHUNK
    dk = HGRN_HEAD_DIM
    spec = lambda off: pl.BlockSpec((None, ts, width), lambda i, t: (i, t, off // width))
    return pl.pallas_call(
        functools.partial(_hgrn_kernel, n_heads=n_heads, n_chunks=n_chunks),
        grid=(b, s // ts),
        in_specs=[
            spec(off_q), spec(off_f), spec(off_i), spec(off_og),
            pl.BlockSpec((1, width), lambda i, t: (0, 0)),
            pl.BlockSpec((1, dk), lambda i, t: (0, 0)),
        ],
        out_specs=pl.BlockSpec((None, ts, width), lambda i, t: (i, t, 0)),
        out_shape=jax.ShapeDtypeStruct((b, s, width), BF16),
        scratch_shapes=[
            pltpu.VMEM((n_heads, dk, dk), F32),
            pltpu.VMEM((ts, 2 * width), BF16),
            pltpu.VMEM((ts, width), F32),
            pltpu.VMEM((ts, width), F32),
            pltpu.VMEM((ts, width), BF16),
            pltpu.VMEM((ts, width), BF16),
            pltpu.VMEM((ts, width), BF16),
            pltpu.VMEM((ts, width), BF16),
            pltpu.VMEM((max(n_chunks, 8), width), F32),
            pltpu.VMEM((ts, width), F32),
            pltpu.VMEM((n_chunks, n_heads, dk, dk), F32),
        ],
        compiler_params=_cparams(("arbitrary", "arbitrary")),
        name="hgrn2",
    )(proj3, proj3, proj3, proj3, lb, gn)


def _merge_kernel(oa_ref, oh_ref, g0_ref, g1_ref, x_ref, wa_ref, wh_ref, wo_ref, gm_ref,
                  wrh_ref, wrl_ref, br_ref, h_ref, hn_ref, route_ref, plan_ref, cnt_ref):
    tm = x_ref.shape[0]

    @pl.when(pl.program_id(0) == 0)
    def _():
        cnt_ref[...] = jnp.zeros_like(cnt_ref)

    ya = jnp.dot(oa_ref[...], wa_ref[...], preferred_element_type=F32)
    yh = jnp.dot(oh_ref[...], wh_ref[...], preferred_element_type=F32)
    merged = (_sigmoid(g0_ref[...].astype(F32)) * ya
              + _sigmoid(g1_ref[...].astype(F32)) * yh).astype(BF16)
    h = x_ref[...] + jnp.dot(merged, wo_ref[...], preferred_element_type=F32)
    h_ref[...] = h
    ms = jnp.mean(h * h, axis=-1, keepdims=True)
    hn = h * lax.rsqrt(ms + EPS) * gm_ref[...]
    hn_ref[...] = _to_row_tiles(hn)

    hn_hi = hn.astype(BF16)
    hn_lo = (hn - hn_hi.astype(F32)).astype(BF16)
    logits = (jnp.dot(hn_hi, wrh_ref[...], preferred_element_type=F32)
              + jnp.dot(hn_hi, wrl_ref[...], preferred_element_type=F32)
              + jnp.dot(hn_lo, wrh_ref[...], preferred_element_type=F32)
              + br_ref[...])
    lane = lax.broadcasted_iota(jnp.int32, (tm, ROUTE_LANES), 1)
    neg = -jnp.inf

    def first_argmax(v):
        m = jnp.max(v, axis=-1, keepdims=True)
        idx = jnp.min(jnp.where(v == m, lane, ROUTE_LANES), axis=-1, keepdims=True)
        return m, idx

    is_group = lane < N_GROUPS
    gmax, gidx = first_argmax(jnp.where(is_group, logits, neg))
    p_sel = 1.0 / jnp.sum(jnp.where(is_group, jnp.exp(logits - gmax), 0.0), axis=-1, keepdims=True)
    eidx = lane - EXPERT_LANE0
    in_group = jnp.logical_and(
        jnp.logical_and(eidx >= 0, eidx < N_EXPERTS), (eidx // EXPERTS_PER_GROUP) == gidx)
    el = jnp.where(in_group, logits, neg)
    m1, i1 = first_argmax(el)
    m2, i2 = first_argmax(jnp.where(lane == i1, neg, el))
    t = jnp.exp(m2 - m1)
    w1 = p_sel / (1.0 + t)
    w2 = p_sel * t / (1.0 + t)

    sel1 = lane == i1
    sel2 = lane == i2
    onehot = jnp.logical_or(sel1, sel2).astype(BF16)
    ri = lax.broadcasted_iota(jnp.int32, (tm, tm), 0)
    ci = lax.broadcasted_iota(jnp.int32, (tm, tm), 1)
    before = (ci < ri).astype(BF16)
    cum = jnp.dot(before, onehot, preferred_element_type=F32) + cnt_ref[...]
    r1 = jnp.sum(jnp.where(sel1, cum, 0.0), axis=-1, keepdims=True)
    r2 = jnp.sum(jnp.where(sel2, cum, 0.0), axis=-1, keepdims=True)
    cnt_ref[...] += jnp.sum(onehot.astype(F32), axis=0, keepdims=True)

    e1 = (i1 - EXPERT_LANE0).astype(F32)
    e2 = (i2 - EXPERT_LANE0).astype(F32)
    out = jnp.zeros((tm, ROUTE_LANES), F32)
    for k, v in enumerate((e1, e2, w1, w2, r1, r2)):
        out = jnp.where(lane == k, v, out)
    route_ref[...] = out
    plan_ref[...] = out.T[:PLAN_ROWS, :].astype(jnp.int32)


def _merge_route(o_a, o_h, proj, x2, wa, wh, wo, gm, wr_hi, wr_lo, br, off_g0, off_g1, tm):
    t, d = x2.shape
    aw, hw = o_a.shape[1], o_h.shape[1]
    const = lambda shape: pl.BlockSpec(shape, lambda i: (0, 0), pipeline_mode=pl.Buffered(1))
    row = lambda w: pl.BlockSpec((tm, w), lambda i: (i, 0))
    return pl.pallas_call(
        _merge_kernel,
        grid=(t // tm,),
        in_specs=[
            row(aw), row(hw),
            pl.BlockSpec((tm, d), lambda i: (i, off_g0 // d)),
            pl.BlockSpec((tm, d), lambda i: (i, off_g1 // d)),
            row(d),
            const((aw, d)), const((hw, d)), const((d, d)), const((1, d)),
            const((d, ROUTE_LANES)), const((d, ROUTE_LANES)), const((1, ROUTE_LANES)),
        ],
        out_specs=[row(d), pl.BlockSpec((tm, d // LANES, LANES), lambda i: (i, 0, 0)),
                   row(ROUTE_LANES),
                   pl.BlockSpec((None, PLAN_ROWS, tm), lambda i: (i, 0, 0)),
                   pl.BlockSpec((1, ROUTE_LANES), lambda i: (0, 0))],
        out_shape=[
            jax.ShapeDtypeStruct((t, d), F32),
            jax.ShapeDtypeStruct((t, d // LANES, LANES), BF16),
            jax.ShapeDtypeStruct((t, ROUTE_LANES), F32),
            jax.ShapeDtypeStruct((t // tm, PLAN_ROWS, tm), jnp.int32),
            jax.ShapeDtypeStruct((1, ROUTE_LANES), F32),
        ],
        compiler_params=_cparams(("arbitrary",)),
        name="merge_route",
    )(o_a, o_h, proj, proj, x2, wa, wh, wo, gm, wr_hi, wr_lo, br)


def _dest_kernel(pst_ref, plan_ref, dest_ref):
    e = plan_ref[:, PLAN_E:PLAN_E + 2, :]
    start = jnp.zeros(e.shape, jnp.int32)
    for j in range(N_EXPERTS):
        start = jnp.where(e == j, pst_ref[j], start)
    dest_ref[...] = start + plan_ref[:, PLAN_R:PLAN_R + 2, :]


def _dest_rows(plan, pst, tiles):
    nt, _, tm = plan.shape
    grid_spec = pltpu.PrefetchScalarGridSpec(
        num_scalar_prefetch=1,
        grid=(nt // tiles,),
        in_specs=[pl.BlockSpec((tiles, PLAN_ROWS, tm), lambda i, *_: (i, 0, 0))],
        out_specs=pl.BlockSpec((tiles, 2, tm), lambda i, *_: (i, 0, 0)),
    )
    return pl.pallas_call(
        _dest_kernel,
        grid_spec=grid_spec,
        out_shape=jax.ShapeDtypeStruct((nt, 2, tm), jnp.int32),
        compiler_params=_cparams(("arbitrary",)),
        name="moe_dest",
    )(pst, plan)


ISSUE_UNROLL = 8


def _dispatch_kernel(fill_ref, nfill_ref, dest_ref, hn_ref, rows_ref, zbuf, sem):
    tm = hn_ref.shape[0]
    bm = zbuf.shape[0]

    @pl.when(pl.program_id(0) == 0)
    def _():
        zbuf[...] = jnp.zeros_like(zbuf)

        def fill(j, carry):
            start = pl.multiple_of(fill_ref[j], bm)
            pltpu.make_async_copy(zbuf, rows_ref.at[pl.ds(start, bm)], sem).start()
            return carry

        def drain(j, carry):
            pltpu.make_async_copy(zbuf, rows_ref.at[pl.ds(0, bm)], sem).wait()
            return carry

        lax.fori_loop(0, nfill_ref[0], fill, 0)
        lax.fori_loop(0, nfill_ref[0], drain, 0)

    def issue(r, carry):
        for k in range(2):
            pltpu.make_async_copy(
                hn_ref.at[r], rows_ref.at[dest_ref[k, r]], sem).start(priority=k)
        return carry

    lax.fori_loop(0, tm, issue, 0, unroll=ISSUE_UNROLL)
    for _ in range(2):
        pltpu.make_async_copy(hn_ref, rows_ref.at[pl.ds(0, tm)], sem).wait()


def _dispatch(hn, dest, fill, nfill, n_rows, tm, bm):
    t, c, l = hn.shape
    grid_spec = pltpu.PrefetchScalarGridSpec(
        num_scalar_prefetch=2,
        grid=(t // tm,),
        in_specs=[
            pl.BlockSpec((None, 2, tm), lambda i, *_: (i, 0, 0), memory_space=pltpu.SMEM),
            pl.BlockSpec((tm, c, l), lambda i, *_: (i, 0, 0)),
        ],
        out_specs=pl.BlockSpec(memory_space=pl.ANY),
        scratch_shapes=[pltpu.VMEM((bm, c, l), hn.dtype), pltpu.SemaphoreType.DMA],
    )
    return pl.pallas_call(
        _dispatch_kernel,
        grid_spec=grid_spec,
        out_shape=jax.ShapeDtypeStruct((n_rows, c, l), hn.dtype),
        compiler_params=_cparams(("arbitrary",)),
        name="moe_dispatch",
    )(fill, nfill, dest, hn)


def _expert_kernel(be_ref, meta_ref, slot_ref, nxt_ref, x_ref, w1_hbm, w3_hbm, w2_hbm, y_ref,
                   wf1, wf3, wf2, w1b, w3b, w2b, sems):
    i = pl.program_id(0)
    e = be_ref[i]
    new_expert = jnp.logical_or(i == 0, e != be_ref[jnp.maximum(i - 1, 0)])

    def weight_copies(expert, slot):
        return [pltpu.make_async_copy(src.at[expert], dst.at[slot], sems.at[slot])
                for src, dst in ((w1_hbm, wf1), (w3_hbm, wf3), (w2_hbm, wf2))]

    @pl.when(jnp.logical_and(i < meta_ref[0], new_expert))
    def _():
        slot = slot_ref[e]
        nxt = nxt_ref[e]

        @pl.when(i == 0)
        def _():
            for c in weight_copies(e, slot):
                c.start()

        @pl.when(nxt >= 0)
        def _():
            for c in weight_copies(nxt, 1 - slot):
                c.start()

        for c in weight_copies(e, slot):
            c.wait()
        w1b[...] = wf1[slot].astype(BF16)
        w3b[...] = wf3[slot].astype(BF16)
        w2b[...] = wf2[slot].astype(BF16)

    @pl.when(i < meta_ref[0])
    def _():
        x = _from_row_tiles(x_ref[...])
        h1 = jnp.dot(x, w1b[...], preferred_element_type=F32)
        h3 = jnp.dot(x, w3b[...], preferred_element_type=F32)
        hb = (h1 * _sigmoid(h1) * h3).astype(BF16)
        y_ref[...] = _to_row_tiles(jnp.dot(hb, w2b[...], preferred_element_type=F32))

    @pl.when(i >= meta_ref[0])
    def _():
        y_ref[...] = jnp.zeros_like(y_ref)


def _experts(rows, block_expert, meta, slot_e, nxt_e, w1, w3, w2, bm):
    n_rows, c, l = rows.shape
    d = c * l
    de = w1.shape[-1]
    n_blocks = n_rows // bm
    hbm = pl.BlockSpec(memory_space=pl.ANY)
    grid_spec = pltpu.PrefetchScalarGridSpec(
        num_scalar_prefetch=4,
        grid=(n_blocks,),
        in_specs=[
            pl.BlockSpec((bm, c, l), lambda i, be, meta, *_: (jnp.minimum(i, meta[0] - 1), 0, 0)),
            hbm, hbm, hbm,
        ],
        out_specs=pl.BlockSpec((bm, c, l), lambda i, *_: (i, 0, 0)),
        scratch_shapes=[
            pltpu.VMEM((2, d, de), w1.dtype), pltpu.VMEM((2, d, de), w3.dtype),
            pltpu.VMEM((2, de, d), w2.dtype),
            pltpu.VMEM((d, de), BF16), pltpu.VMEM((d, de), BF16), pltpu.VMEM((de, d), BF16),
            pltpu.SemaphoreType.DMA((2,)),
        ],
    )
    return pl.pallas_call(
        _expert_kernel,
        grid_spec=grid_spec,
        out_shape=jax.ShapeDtypeStruct((n_rows, c, l), BF16),
        compiler_params=_cparams(("arbitrary",)),
        name="moe_experts",
    )(block_expert, meta, slot_e, nxt_e, rows, w1, w3, w2)


def _combine_kernel(dcur_ref, dnext_ref, h_ref, route_ref, g_ref, y_ref, o_ref, ybuf, sems):
    i = pl.program_id(0)
    nt = pl.num_programs(0)
    tm, d = h_ref.shape
    slot = i % 2

    def gather(dest_ref, s):
        def issue(r, carry):
            for k in range(2):
                pltpu.make_async_copy(
                    y_ref.at[dest_ref[k, r]], ybuf.at[s, k, r], sems.at[s]).start(priority=k)
            return carry
        lax.fori_loop(0, tm, issue, 0, unroll=ISSUE_UNROLL)

    @pl.when(i == 0)
    def _():
        gather(dcur_ref, slot)

    @pl.when(i + 1 < nt)
    def _():
        gather(dnext_ref, 1 - slot)

    for k in range(2):
        pltpu.make_async_copy(y_ref.at[pl.ds(0, tm)], ybuf.at[slot, k], sems.at[slot]).wait()

    route = route_ref[...]
    w1 = route[:, 2:3]
    w2 = route[:, 3:4]
    ya = _from_row_tiles(ybuf[slot, 0]).astype(F32)
    yb = _from_row_tiles(ybuf[slot, 1]).astype(F32)
    h = h_ref[...] + w1 * ya + w2 * yb
    ms = jnp.mean(h * h, axis=-1, keepdims=True)
    o_ref[...] = h * lax.rsqrt(ms + EPS) * g_ref[...]


def _combine(h, route, dest, y_rows, g, tm):
    t, d = h.shape
    nt = t // tm
    dest_spec = lambda f: pl.BlockSpec((None, 2, tm), f, memory_space=pltpu.SMEM)
    return pl.pallas_call(
        _combine_kernel,
        grid=(nt,),
        in_specs=[
            dest_spec(lambda i: (i, 0, 0)),
            dest_spec(lambda i: (jnp.minimum(i + 1, nt - 1), 0, 0)),
            pl.BlockSpec((tm, d), lambda i: (i, 0)),
            pl.BlockSpec((tm, ROUTE_LANES), lambda i: (i, 0)),
            pl.BlockSpec((1, d), lambda i: (0, 0)),
            pl.BlockSpec(memory_space=pl.ANY),
        ],
        out_specs=pl.BlockSpec((tm, d), lambda i: (i, 0)),
        out_shape=jax.ShapeDtypeStruct((t, d), F32),
        scratch_shapes=[pltpu.VMEM((2, 2, tm, d // LANES, LANES), BF16),
                        pltpu.SemaphoreType.DMA((2,))],
        compiler_params=_cparams(("arbitrary",)),
        name="moe_combine",
    )(dest, dest, h, route, g, y_rows)


def kernel(x, w_in, attn_sinks, hgrn_lb_logits, hgrn_norm_g, w_br_attn, w_br_hgrn, w_out,
           mix_norm_g, moe_norm_g, w_router_group, b_router_group, w_router_expert,
           b_router_expert, w1, w3, w2, final_norm_g):
    b, s, d = x.shape
    t = b * s
    depth = w_in.shape[0]
    assert depth == 1
    attn_w = (d // 128) * ATTN_HEAD_DIM
    kv_w = attn_w // Q_PER_KV
    hg_w = (d // 256) * HGRN_HEAD_DIM
    assert kv_w == N_KV_HEADS * ATTN_HEAD_DIM

    src = {}
    off = 0
    for name, width in (("qa", attn_w), ("k", kv_w), ("v", kv_w), ("qh", hg_w), ("f", hg_w),
                        ("i", hg_w), ("og", hg_w), ("g0", d), ("g1", d)):
        src[name] = (off, width)
        off += width
    order = ("g0", "g1", "qa", "qh", "f", "i", "og", "k", "v")
    dst = {}
    off = 0
    for name in order:
        dst[name] = off
        off += src[name][1]
    n_in = off
    l = 0
    tn_in = 512
    assert all(src[n][0] % tn_in == 0 for n in order if n != "v") and (2 * kv_w) % tn_in == 0
    col_blocks = []
    for n in order[:-1]:
        width = src[n][1] if n != "k" else 2 * kv_w
        col_blocks += [src[n][0] // tn_in + c for c in range(width // tn_in)]
    col_blocks = jnp.asarray(col_blocks, jnp.int32)
    w_bf = w_in[l].astype(BF16)

    lb = jnp.cumsum(jax.nn.softmax(hgrn_lb_logits.astype(F32), axis=0), axis=0)[l].reshape(1, hg_w)

    x2 = x.reshape(t, d)
    tm_in = min(2048, t)
    proj = _in_proj(x2, mix_norm_g[l].reshape(1, d), w_bf, col_blocks, tm_in, tn_in)
    proj3 = proj.reshape(b, s, n_in)

    o_a = _attention(proj3, attn_sinks[l].astype(F32), dst["qa"], dst["k"], dst["v"], attn_w, kv_w)
    o_h = _hgrn(proj3, lb, hgrn_norm_g[l].reshape(1, HGRN_HEAD_DIM).astype(F32),
                dst["qh"], dst["f"], dst["i"], dst["og"], hg_w, min(256, s))

    w_r = jnp.zeros((d, ROUTE_LANES), F32)
    w_r = w_r.at[:, :N_GROUPS].set(w_router_group[l])
    w_r = w_r.at[:, EXPERT_LANE0:EXPERT_LANE0 + N_EXPERTS].set(w_router_expert[l])
    wr_hi = w_r.astype(BF16)
    wr_lo = (w_r - wr_hi.astype(F32)).astype(BF16)
    b_r = jnp.zeros((1, ROUTE_LANES), F32)
    b_r = b_r.at[0, :N_GROUPS].set(b_router_group[l])
    b_r = b_r.at[0, EXPERT_LANE0:EXPERT_LANE0 + N_EXPERTS].set(b_router_expert[l])

    tm = min(256, t)
    h, hn, route, plan, cnt = _merge_route(
        o_a.reshape(t, attn_w), o_h.reshape(t, hg_w), proj, x2,
        w_br_attn[l].astype(BF16), w_br_hgrn[l].astype(BF16), w_out[l].astype(BF16),
        moe_norm_g[l].reshape(1, d), wr_hi, wr_lo, b_r, dst["g0"], dst["g1"], tm)

    bm = MOE_ROWS
    counts = cnt[0, EXPERT_LANE0:EXPERT_LANE0 + N_EXPERTS].astype(jnp.int32)
    padded = ((counts + bm - 1) // bm) * bm
    pad_end = jnp.cumsum(padded)
    pad_start = (pad_end - padded).astype(jnp.int32)
    n_blocks = (2 * t) // bm + N_EXPERTS
    n_used = pad_end[-1] // bm
    blk_ids = jnp.minimum(jnp.arange(n_blocks, dtype=jnp.int32), n_used - 1)
    block_expert = jnp.minimum(
        jnp.sum((pad_end[None, :] <= (blk_ids * bm)[:, None]).astype(jnp.int32), axis=1),
        N_EXPERTS - 1)
    meta = jnp.stack([n_used, n_used]).astype(jnp.int32)

    blk_all = jnp.arange(n_blocks, dtype=jnp.int32)
    fill = jnp.concatenate([
        jnp.maximum(pad_end - bm, 0).astype(jnp.int32),
        jnp.minimum(n_used + blk_all, n_blocks - 1) * bm])
    nfill = (N_EXPERTS + n_blocks - n_used).astype(jnp.int32).reshape(1)

    eid = jnp.arange(N_EXPERTS, dtype=jnp.int32)
    nonempty = counts > 0
    slot_e = ((jnp.cumsum(nonempty.astype(jnp.int32)) - 1) % 2).astype(jnp.int32)
    later = jnp.logical_and(nonempty[None, :], eid[None, :] > eid[:, None])
    nxt_e = jnp.min(jnp.where(later, eid[None, :], N_EXPERTS), axis=1)
    nxt_e = jnp.where(nxt_e == N_EXPERTS, -1, nxt_e).astype(jnp.int32)

    nt = t // tm
    dest = _dest_rows(plan, pad_start, min(8, nt))
    rows = _dispatch(hn, dest, fill, nfill, n_blocks * bm, tm, bm)
    y_rows = _experts(rows, block_expert, meta, slot_e, nxt_e, w1[l], w3[l], w2[l], bm)
    out = _combine(h, route, dest, y_rows, final_norm_g.reshape(1, d), tm)
    return out.reshape(b, s, d)
```

```python
import functools

import jax
import jax.numpy as jnp
from jax import lax
from jax.experimental import pallas as pl
from jax.experimental.pallas import tpu as pltpu

F32 = jnp.float32
BF16 = jnp.bfloat16

EPS = 1e-6
ATTN_HEAD_DIM = 64
Q_PER_KV = 4
N_KV_HEADS = 4
ATTN_BLOCK = 128
HGRN_HEAD_DIM = 128
HGRN_CHUNK = 64
N_GROUPS = 4
EXPERTS_PER_GROUP = 8
N_EXPERTS = N_GROUPS * EXPERTS_PER_GROUP
LANES = 128
ROUTE_LANES = LANES
EXPERT_LANE0 = N_GROUPS
MOE_ROWS = 256
PLAN_ROWS = 8
PLAN_E, PLAN_R = 0, 4
VMEM_LIMIT = 56 * 1024 * 1024


def _sigmoid(x):
    return 0.5 * jnp.tanh(0.5 * x) + 0.5


def _cparams(sem, vmem=VMEM_LIMIT):
    return pltpu.CompilerParams(dimension_semantics=sem, vmem_limit_bytes=vmem)


def _to_row_tiles(v):
    m, d = v.shape
    return v.reshape(m, d // LANES, LANES).astype(BF16)


def _from_row_tiles(p):
    m, c, l = p.shape
    return p.reshape(m, c * l)


def _inproj_kernel(cols_ref, x_ref, g_ref, w_ref, o_ref, xn_ref):
    del cols_ref

    @pl.when(pl.program_id(1) == 0)
    def _():
        x = x_ref[...]
        ms = jnp.mean(x * x, axis=-1, keepdims=True)
        xn_ref[...] = (x * lax.rsqrt(ms + EPS) * g_ref[...]).astype(BF16)

    o_ref[...] = jnp.dot(xn_ref[...], w_ref[...], preferred_element_type=F32).astype(o_ref.dtype)


def _in_proj(x2, g, w, col_blocks, tm, tn):
    t, d = x2.shape
    n = w.shape[1]
    grid_spec = pltpu.PrefetchScalarGridSpec(
        num_scalar_prefetch=1,
        grid=(t // tm, n // tn),
        in_specs=[
            pl.BlockSpec((tm, d), lambda i, j, cols: (i, 0)),
            pl.BlockSpec((1, d), lambda i, j, cols: (0, 0)),
            pl.BlockSpec((d, tn), lambda i, j, cols: (0, cols[j])),
        ],
        out_specs=pl.BlockSpec((tm, tn), lambda i, j, cols: (i, j)),
        scratch_shapes=[pltpu.VMEM((tm, d), BF16)],
    )
    return pl.pallas_call(
        _inproj_kernel,
        grid_spec=grid_spec,
        out_shape=jax.ShapeDtypeStruct((t, n), BF16),
        compiler_params=_cparams(("arbitrary", "arbitrary")),
        name="in_proj",
    )(col_blocks, x2, g, w)


def _attn_kernel(sink_ref, q_ref, kc_ref, kp_ref, vc_ref, vp_ref, o_ref):
    n = pl.program_id(1)
    blk, dh, g = ATTN_BLOCK, ATTN_HEAD_DIM, Q_PER_KV
    rows = g * blk
    qi = lax.broadcasted_iota(jnp.int32, (rows, blk), 0) % blk
    kj = lax.broadcasted_iota(jnp.int32, (rows, blk), 1)
    mask_cur = kj <= qi
    prev_bias = jnp.where(n > 0, 0.0, -jnp.inf)
    head_of_row = lax.broadcasted_iota(jnp.int32, (rows, 1), 0) // blk
    nt = (((1,), (1,)), ((), ()))
    scale = dh ** -0.5
    for h in range(N_KV_HEADS):
        kc = kc_ref[:, h * dh:(h + 1) * dh]
        kp = kp_ref[:, h * dh:(h + 1) * dh]
        vc = vc_ref[:, h * dh:(h + 1) * dh]
        vp = vp_ref[:, h * dh:(h + 1) * dh]
        q4 = jnp.concatenate(
            [q_ref[:, (h * g + j) * dh:(h * g + j + 1) * dh] for j in range(g)], axis=0)
        sc = lax.dot_general(q4, kc, nt, preferred_element_type=F32)
        sp = lax.dot_general(q4, kp, nt, preferred_element_type=F32)
        s = jnp.where(mask_cur, sc, sp + prev_bias) * scale
        sink = jnp.zeros((rows, 1), F32)
        for j in range(g):
            sink = jnp.where(head_of_row == j, sink_ref[h * g + j], sink)
        m = jnp.maximum(jnp.max(s, axis=-1, keepdims=True), sink)
        p = jnp.exp(s - m)
        den = jnp.sum(p, axis=-1, keepdims=True) + jnp.exp(sink - m)
        p_cur = jnp.where(mask_cur, p, 0.0).astype(BF16)
        p_prev = jnp.where(mask_cur, 0.0, p).astype(BF16)
        o = (jnp.dot(p_cur, vc, preferred_element_type=F32)
             + jnp.dot(p_prev, vp, preferred_element_type=F32)) / den
        for j in range(g):
            o_ref[:, (h * g + j) * dh:(h * g + j + 1) * dh] = (
                o[j * blk:(j + 1) * blk, :].astype(o_ref.dtype))


def _attention(proj3, sinks, off_q, off_k, off_v, attn_w, kv_w):
    b, s, _ = proj3.shape
    blk = ATTN_BLOCK
    qb, kb, vb = off_q // attn_w, off_k // kv_w, off_v // kv_w
    prev = lambda n: jnp.maximum(n - 1, 0)
    return pl.pallas_call(
        _attn_kernel,
        grid=(b, s // blk),
        in_specs=[
            pl.BlockSpec(memory_space=pltpu.SMEM),
            pl.BlockSpec((None, blk, attn_w), lambda i, n: (i, n, qb)),
            pl.BlockSpec((None, blk, kv_w), lambda i, n: (i, n, kb)),
            pl.BlockSpec((None, blk, kv_w), lambda i, n: (i, prev(n), kb)),
            pl.BlockSpec((None, blk, kv_w), lambda i, n: (i, n, vb)),
            pl.BlockSpec((None, blk, kv_w), lambda i, n: (i, prev(n), vb)),
        ],
        out_specs=pl.BlockSpec((None, blk, attn_w), lambda i, n: (i, n, 0)),
        out_shape=jax.ShapeDtypeStruct((b, s, attn_w), BF16),
        compiler_params=_cparams(("arbitrary", "arbitrary")),
        name="swa_attention",
    )(sinks, proj3, proj3, proj3, proj3, proj3)


def _hgrn_kernel(q_ref, f_ref, i_ref, og_ref, lb_ref, gn_ref, o_ref,
                 st_ref, gh_s, f_s, b_s, qt_s, kt_s, qe_s, kd_s, dl_s, oi_s, u_s,
                 *, n_heads, n_chunks):
    c, dk = HGRN_CHUNK, HGRN_HEAD_DIM
    w = n_heads * dk

    @pl.when(pl.program_id(1) == 0)
    def _():
        st_ref[...] = jnp.zeros_like(st_ref)

    ti = lax.broadcasted_iota(jnp.int32, (c, c), 0)
    si = lax.broadcasted_iota(jnp.int32, (c, c), 1)
    causal = si <= ti
    tri = causal.astype(BF16)
    nt = (((1,), (1,)), ((), ()))
    tn = (((0,), (0,)), ((), ()))
    qscale = dk ** -0.5
    chunk_rows = [slice(ci * c, (ci + 1) * c) for ci in range(n_chunks)]
    head_cols = [slice(h * dk, (h + 1) * dk) for h in range(n_heads)]

    lb = lb_ref[...]
    f = lb + (1.0 - lb) * _sigmoid(f_ref[...].astype(F32))
    f_s[...] = f
    gl = jnp.log(f)
    g_hi = gl.astype(BF16)
    gh_s[:, :w] = g_hi
    gh_s[:, w:] = (gl - g_hi.astype(F32)).astype(BF16)

    for rows in chunk_rows:
        bcat = jnp.dot(tri, gh_s[rows, :], preferred_element_type=F32)
        b_s[rows, :] = bcat[:, :w] + bcat[:, w:]

    for ci, rows in enumerate(chunk_rows):
        bc = b_s[rows, :]
        b_last = bc[c - 1:c, :]
        r = 0.5 * b_last
        qv = q_ref[rows, :].astype(F32)
        qt = qv * _sigmoid(qv) * qscale * jnp.exp(bc - r)
        kt = (1.0 - f_s[rows, :]) * jnp.exp(r - bc)
        qt_s[rows, :] = qt.astype(BF16)
        kt_s[rows, :] = kt.astype(BF16)
        qe_s[rows, :] = (qt * jnp.exp(r)).astype(BF16)
        kd_s[rows, :] = (kt * jnp.exp(b_last - r)).astype(BF16)
        dl_s[ci:ci + 1, :] = jnp.exp(b_last)

    for ci, rows in enumerate(chunk_rows):
        for h, cols in enumerate(head_cols):
            a = lax.dot_general(qt_s[rows, cols], kt_s[rows, cols], nt,
                                preferred_element_type=F32)
            a = jnp.where(causal, a, 0.0).astype(BF16)
            vv = i_ref[rows, cols]
            oi_s[rows, cols] = jnp.dot(a, vv, preferred_element_type=F32)
            u_s[ci, h] = lax.dot_general(vv, kd_s[rows, cols], tn, preferred_element_type=F32)

    gn = gn_ref[...]
    for ci, rows in enumerate(chunk_rows):
        for h, cols in enumerate(head_cols):
            st = st_ref[h]
            o = oi_s[rows, cols] + lax.dot_general(
                qe_s[rows, cols], st.astype(BF16), nt, preferred_element_type=F32)
            st_ref[h] = st * dl_s[ci:ci + 1, cols] + u_s[ci, h]
            ms = jnp.mean(o * o, axis=-1, keepdims=True)
            ogv = og_ref[rows, cols].astype(F32)
            o_ref[rows, cols] = (
                o * lax.rsqrt(ms + EPS) * gn * (ogv * _sigmoid(ogv))).astype(o_ref.dtype)


def _hgrn(proj3, lb, gn, off_q, off_f, off_i, off_og, width, ts):
    b, s, _ = proj3.shape
    n_heads = width // HGRN_HEAD_DIM
    n_chunks = ts // HGRN_CHUNK
    dk = HGRN_HEAD_DIM
    spec = lambda off: pl.BlockSpec((None, ts, width), lambda i, t: (i, t, off // width))
    return pl.pallas_call(
        functools.partial(_hgrn_kernel, n_heads=n_heads, n_chunks=n_chunks),
        grid=(b, s // ts),
        in_specs=[
            spec(off_q), spec(off_f), spec(off_i), spec(off_og),
            pl.BlockSpec((1, width), lambda i, t: (0, 0)),
            pl.BlockSpec((1, dk), lambda i, t: (0, 0)),
        ],
        out_specs=pl.BlockSpec((None, ts, width), lambda i, t: (i, t, 0)),
        out_shape=jax.ShapeDtypeStruct((b, s, width), BF16),
        scratch_shapes=[
            pltpu.VMEM((n_heads, dk, dk), F32),
            pltpu.VMEM((ts, 2 * width), BF16),
            pltpu.VMEM((ts, width), F32),
            pltpu.VMEM((ts, width), F32),
            pltpu.VMEM((ts, width), BF16),
            pltpu.VMEM((ts, width), BF16),
            pltpu.VMEM((ts, width), BF16),
            pltpu.VMEM((ts, width), BF16),
            pltpu.VMEM((max(n_chunks, 8), width), F32),
            pltpu.VMEM((ts, width), F32),
            pltpu.VMEM((n_chunks, n_heads, dk, dk), F32),
        ],
        compiler_params=_cparams(("arbitrary", "arbitrary")),
        name="hgrn2",
    )(proj3, proj3, proj3, proj3, lb, gn)


def _merge_kernel(oa_ref, oh_ref, g0_ref, g1_ref, x_ref, wa_ref, wh_ref, wo_ref, gm_ref,
                  wrh_ref, wrl_ref, br_ref, h_ref, hn_ref, route_ref, plan_ref, cnt_ref,
                  hn_s):
    tm, d = x_ref.shape
    half = d // 2
    halves = (slice(0, half), slice(half, d))
    step = pl.program_id(0)

    @pl.when(step == 0)
    def _():
        cnt_ref[...] = jnp.zeros_like(cnt_ref)
        hn_s[...] = jnp.zeros_like(hn_s)

    hn_prev = hn_s[...]
    hn_hi = hn_prev.astype(BF16)
    hn_lo = (hn_prev - hn_hi.astype(F32)).astype(BF16)

    ya = [jnp.dot(oa_ref[...], wa_ref[:, cs], preferred_element_type=F32) for cs in halves]
    yh = [jnp.dot(oh_ref[...], wh_ref[:, cs], preferred_element_type=F32) for cs in halves]

    logits = (jnp.dot(hn_hi, wrh_ref[...], preferred_element_type=F32)
              + jnp.dot(hn_hi, wrl_ref[...], preferred_element_type=F32)
              + jnp.dot(hn_lo, wrh_ref[...], preferred_element_type=F32)
              + br_ref[...])
    lane = lax.broadcasted_iota(jnp.int32, (tm, ROUTE_LANES), 1)
    neg = -jnp.inf

    def first_argmax(v):
        m = jnp.max(v, axis=-1, keepdims=True)
        idx = jnp.min(jnp.where(v == m, lane, ROUTE_LANES), axis=-1, keepdims=True)
        return m, idx

    is_group = lane < N_GROUPS
    gmax, gidx = first_argmax(jnp.where(is_group, logits, neg))
    p_sel = 1.0 / jnp.sum(jnp.where(is_group, jnp.exp(logits - gmax), 0.0), axis=-1, keepdims=True)
    eidx = lane - EXPERT_LANE0
    in_group = jnp.logical_and(
        jnp.logical_and(eidx >= 0, eidx < N_EXPERTS), (eidx // EXPERTS_PER_GROUP) == gidx)
    el = jnp.where(in_group, logits, neg)
    m1, i1 = first_argmax(el)
    m2, i2 = first_argmax(jnp.where(lane == i1, neg, el))
    t = jnp.exp(m2 - m1)
    w1 = p_sel / (1.0 + t)
    w2 = p_sel * t / (1.0 + t)

    sel1 = lane == i1
    sel2 = lane == i2
    onehot = jnp.logical_or(sel1, sel2).astype(BF16)
    ri = lax.broadcasted_iota(jnp.int32, (tm, tm), 0)
    ci = lax.broadcasted_iota(jnp.int32, (tm, tm), 1)
    before = (ci < ri).astype(BF16)

    h = x_ref[...]
    for k, cs in enumerate(halves):
        merged = (_sigmoid(g0_ref[:, cs].astype(F32)) * ya[k]
                  + _sigmoid(g1_ref[:, cs].astype(F32)) * yh[k]).astype(BF16)
        h = h + jnp.dot(merged, wo_ref[cs, :], preferred_element_type=F32)
        if k == 0:
            cum = jnp.dot(before, onehot, preferred_element_type=F32) + cnt_ref[...]

    r1 = jnp.sum(jnp.where(sel1, cum, 0.0), axis=-1, keepdims=True)
    r2 = jnp.sum(jnp.where(sel2, cum, 0.0), axis=-1, keepdims=True)
    routed = jnp.where(step > 0, 1.0, 0.0)
    cnt_ref[...] += routed * jnp.sum(onehot.astype(F32), axis=0, keepdims=True)

    e1 = (i1 - EXPERT_LANE0).astype(F32)
    e2 = (i2 - EXPERT_LANE0).astype(F32)
    out = jnp.zeros((tm, ROUTE_LANES), F32)
    for k, v in enumerate((e1, e2, w1, w2, r1, r2)):
        out = jnp.where(lane == k, v, out)
    route_ref[...] = out
    plan_ref[...] = out.T[:PLAN_ROWS, :].astype(jnp.int32)

    h_ref[...] = h
    ms = jnp.mean(h * h, axis=-1, keepdims=True)
    hn = h * lax.rsqrt(ms + EPS) * gm_ref[...]
    hn_ref[...] = _to_row_tiles(hn)
    hn_s[...] = hn


def _merge_route(o_a, o_h, proj, x2, wa, wh, wo, gm, wr_hi, wr_lo, br, off_g0, off_g1, tm):
    t, d = x2.shape
    nt = t // tm
    aw, hw = o_a.shape[1], o_h.shape[1]
    const = lambda shape: pl.BlockSpec(shape, lambda i: (0, 0), pipeline_mode=pl.Buffered(1))
    cur = lambda i: jnp.minimum(i, nt - 1)
    prv = lambda i: jnp.maximum(i - 1, 0)
    row = lambda w: pl.BlockSpec((tm, w), lambda i: (cur(i), 0))
    return pl.pallas_call(
        _merge_kernel,
        grid=(nt + 1,),
        in_specs=[
            row(aw), row(hw),
            pl.BlockSpec((tm, d), lambda i: (cur(i), off_g0 // d)),
            pl.BlockSpec((tm, d), lambda i: (cur(i), off_g1 // d)),
            row(d),
            const((aw, d)), const((hw, d)), const((d, d)), const((1, d)),
            const((d, ROUTE_LANES)), const((d, ROUTE_LANES)), const((1, ROUTE_LANES)),
        ],
        out_specs=[row(d), pl.BlockSpec((tm, d // LANES, LANES), lambda i: (cur(i), 0, 0)),
                   pl.BlockSpec((tm, ROUTE_LANES), lambda i: (prv(i), 0)),
                   pl.BlockSpec((None, PLAN_ROWS, tm), lambda i: (prv(i), 0, 0)),
                   pl.BlockSpec((1, ROUTE_LANES), lambda i: (0, 0))],
        out_shape=[
            jax.ShapeDtypeStruct((t, d), F32),
            jax.ShapeDtypeStruct((t, d // LANES, LANES), BF16),
            jax.ShapeDtypeStruct((t, ROUTE_LANES), F32),
            jax.ShapeDtypeStruct((nt, PLAN_ROWS, tm), jnp.int32),
            jax.ShapeDtypeStruct((1, ROUTE_LANES), F32),
        ],
        scratch_shapes=[pltpu.VMEM((tm, d), F32)],
        compiler_params=_cparams(("arbitrary",)),
        name="merge_route",
    )(o_a, o_h, proj, proj, x2, wa, wh, wo, gm, wr_hi, wr_lo, br)


def _dest_kernel(pst_ref, plan_ref, dest_ref):
    e = plan_ref[:, PLAN_E:PLAN_E + 2, :]
    start = jnp.zeros(e.shape, jnp.int32)
    for j in range(N_EXPERTS):
        start = jnp.where(e == j, pst_ref[j], start)
    dest_ref[...] = start + plan_ref[:, PLAN_R:PLAN_R + 2, :]


def _dest_rows(plan, pst, tiles):
    nt, _, tm = plan.shape
    grid_spec = pltpu.PrefetchScalarGridSpec(
        num_scalar_prefetch=1,
        grid=(nt // tiles,),
        in_specs=[pl.BlockSpec((tiles, PLAN_ROWS, tm), lambda i, *_: (i, 0, 0))],
        out_specs=pl.BlockSpec((tiles, 2, tm), lambda i, *_: (i, 0, 0)),
    )
    return pl.pallas_call(
        _dest_kernel,
        grid_spec=grid_spec,
        out_shape=jax.ShapeDtypeStruct((nt, 2, tm), jnp.int32),
        compiler_params=_cparams(("arbitrary",)),
        name="moe_dest",
    )(pst, plan)


ISSUE_UNROLL = 8


def _dispatch_kernel(fill_ref, nfill_ref, dest_ref, hn_ref, rows_ref, zbuf, sem):
    tm = hn_ref.shape[0]
    bm = zbuf.shape[0]

    @pl.when(pl.program_id(0) == 0)
    def _():
        zbuf[...] = jnp.zeros_like(zbuf)

        def fill(j, carry):
            start = pl.multiple_of(fill_ref[j], bm)
            pltpu.make_async_copy(zbuf, rows_ref.at[pl.ds(start, bm)], sem).start()
            return carry

        def drain(j, carry):
            pltpu.make_async_copy(zbuf, rows_ref.at[pl.ds(0, bm)], sem).wait()
            return carry

        lax.fori_loop(0, nfill_ref[0], fill, 0)
        lax.fori_loop(0, nfill_ref[0], drain, 0)

    def issue(r, carry):
        for k in range(2):
            pltpu.make_async_copy(
                hn_ref.at[r], rows_ref.at[dest_ref[k, r]], sem).start(priority=k)
        return carry

    lax.fori_loop(0, tm, issue, 0, unroll=ISSUE_UNROLL)
    for _ in range(2):
        pltpu.make_async_copy(hn_ref, rows_ref.at[pl.ds(0, tm)], sem).wait()


def _dispatch(hn, dest, fill, nfill, n_rows, tm, bm):
    t, c, l = hn.shape
    grid_spec = pltpu.PrefetchScalarGridSpec(
        num_scalar_prefetch=2,
        grid=(t // tm,),
        in_specs=[
            pl.BlockSpec((None, 2, tm), lambda i, *_: (i, 0, 0), memory_space=pltpu.SMEM),
            pl.BlockSpec((tm, c, l), lambda i, *_: (i, 0, 0)),
        ],
        out_specs=pl.BlockSpec(memory_space=pl.ANY),
        scratch_shapes=[pltpu.VMEM((bm, c, l), hn.dtype), pltpu.SemaphoreType.DMA],
    )
    return pl.pallas_call(
        _dispatch_kernel,
        grid_spec=grid_spec,
        out_shape=jax.ShapeDtypeStruct((n_rows, c, l), hn.dtype),
        compiler_params=_cparams(("arbitrary",)),
        name="moe_dispatch",
    )(fill, nfill, dest, hn)


def _expert_kernel(be_ref, meta_ref, slot_ref, nxt_ref, x_ref, w1_hbm, w3_hbm, w2_hbm, y_ref,
                   wf1, wf3, wf2, w1b, w3b, w2b, sems):
    i = pl.program_id(0)
    e = be_ref[i]
    new_expert = jnp.logical_or(i == 0, e != be_ref[jnp.maximum(i - 1, 0)])

    def weight_copies(expert, slot):
        return [pltpu.make_async_copy(src.at[expert], dst.at[slot], sems.at[slot])
                for src, dst in ((w1_hbm, wf1), (w3_hbm, wf3), (w2_hbm, wf2))]

    @pl.when(jnp.logical_and(i < meta_ref[0], new_expert))
    def _():
        slot = slot_ref[e]
        nxt = nxt_ref[e]

        @pl.when(i == 0)
        def _():
            for c in weight_copies(e, slot):
                c.start()

        @pl.when(nxt >= 0)
        def _():
            for c in weight_copies(nxt, 1 - slot):
                c.start()

        for c in weight_copies(e, slot):
            c.wait()
        w1b[...] = wf1[slot].astype(BF16)
        w3b[...] = wf3[slot].astype(BF16)
        w2b[...] = wf2[slot].astype(BF16)

    @pl.when(i < meta_ref[0])
    def _():
        x = _from_row_tiles(x_ref[...])
        h1 = jnp.dot(x, w1b[...], preferred_element_type=F32)
        h3 = jnp.dot(x, w3b[...], preferred_element_type=F32)
        hb = (h1 * _sigmoid(h1) * h3).astype(BF16)
        y_ref[...] = _to_row_tiles(jnp.dot(hb, w2b[...], preferred_element_type=F32))

    @pl.when(i >= meta_ref[0])
    def _():
        y_ref[...] = jnp.zeros_like(y_ref)


def _experts(rows, block_expert, meta, slot_e, nxt_e, w1, w3, w2, bm):
    n_rows, c, l = rows.shape
    d = c * l
    de = w1.shape[-1]
    n_blocks = n_rows // bm
    hbm = pl.BlockSpec(memory_space=pl.ANY)
    grid_spec = pltpu.PrefetchScalarGridSpec(
        num_scalar_prefetch=4,
        grid=(n_blocks,),
        in_specs=[
            pl.BlockSpec((bm, c, l), lambda i, be, meta, *_: (jnp.minimum(i, meta[0] - 1), 0, 0)),
            hbm, hbm, hbm,
        ],
        out_specs=pl.BlockSpec((bm, c, l), lambda i, *_: (i, 0, 0)),
        scratch_shapes=[
            pltpu.VMEM((2, d, de), w1.dtype), pltpu.VMEM((2, d, de), w3.dtype),
            pltpu.VMEM((2, de, d), w2.dtype),
            pltpu.VMEM((d, de), BF16), pltpu.VMEM((d, de), BF16), pltpu.VMEM((de, d), BF16),
            pltpu.SemaphoreType.DMA((2,)),
        ],
    )
    return pl.pallas_call(
        _expert_kernel,
        grid_spec=grid_spec,
        out_shape=jax.ShapeDtypeStruct((n_rows, c, l), BF16),
        compiler_params=_cparams(("arbitrary",)),
        name="moe_experts",
    )(block_expert, meta, slot_e, nxt_e, rows, w1, w3, w2)


def _combine_kernel(dcur_ref, dnext_ref, h_ref, route_ref, g_ref, y_ref, o_ref, ybuf, sems):
    i = pl.program_id(0)
    nt = pl.num_programs(0)
    tm, d = h_ref.shape
    slot = i % 2

    def gather(dest_ref, s):
        def issue(r, carry):
            for k in range(2):
                pltpu.make_async_copy(
                    y_ref.at[dest_ref[k, r]], ybuf.at[s, k, r], sems.at[s]).start(priority=k)
            return carry
        lax.fori_loop(0, tm, issue, 0, unroll=ISSUE_UNROLL)

    @pl.when(i == 0)
    def _():
        gather(dcur_ref, slot)

    @pl.when(i + 1 < nt)
    def _():
        gather(dnext_ref, 1 - slot)

    for k in range(2):
        pltpu.make_async_copy(y_ref.at[pl.ds(0, tm)], ybuf.at[slot, k], sems.at[slot]).wait()

    route = route_ref[...]
    w1 = route[:, 2:3]
    w2 = route[:, 3:4]
    ya = _from_row_tiles(ybuf[slot, 0]).astype(F32)
    yb = _from_row_tiles(ybuf[slot, 1]).astype(F32)
    h = h_ref[...] + w1 * ya + w2 * yb
    ms = jnp.mean(h * h, axis=-1, keepdims=True)
    o_ref[...] = h * lax.rsqrt(ms + EPS) * g_ref[...]


def _combine(h, route, dest, y_rows, g, tm):
    t, d = h.shape
    nt = t // tm
    dest_spec = lambda f: pl.BlockSpec((None, 2, tm), f, memory_space=pltpu.SMEM)
    return pl.pallas_call(
        _combine_kernel,
        grid=(nt,),
        in_specs=[
            dest_spec(lambda i: (i, 0, 0)),
            dest_spec(lambda i: (jnp.minimum(i + 1, nt - 1), 0, 0)),
            pl.BlockSpec((tm, d), lambda i: (i, 0)),
            pl.BlockSpec((tm, ROUTE_LANES), lambda i: (i, 0)),
            pl.BlockSpec((1, d), lambda i: (0, 0)),
            pl.BlockSpec(memory_space=pl.ANY),
        ],
        out_specs=pl.BlockSpec((tm, d), lambda i: (i, 0)),
        out_shape=jax.ShapeDtypeStruct((t, d), F32),
        scratch_shapes=[pltpu.VMEM((2, 2, tm, d // LANES, LANES), BF16),
                        pltpu.SemaphoreType.DMA((2,))],
        compiler_params=_cparams(("arbitrary",)),
        name="moe_combine",
    )(dest, dest, h, route, g, y_rows)


def kernel(x, w_in, attn_sinks, hgrn_lb_logits, hgrn_norm_g, w_br_attn, w_br_hgrn, w_out,
           mix_norm_g, moe_norm_g, w_router_group, b_router_group, w_router_expert,
           b_router_expert, w1, w3, w2, final_norm_g):
    b, s, d = x.shape
    t = b * s
    depth = w_in.shape[0]
    assert depth == 1
    attn_w = (d // 128) * ATTN_HEAD_DIM
    kv_w = attn_w // Q_PER_KV
    hg_w = (d // 256) * HGRN_HEAD_DIM
    assert kv_w == N_KV_HEADS * ATTN_HEAD_DIM

    src = {}
    off = 0
    for name, width in (("qa", attn_w), ("k", kv_w), ("v", kv_w), ("qh", hg_w), ("f", hg_w),
                        ("i", hg_w), ("og", hg_w), ("g0", d), ("g1", d)):
        src[name] = (off, width)
        off += width
    order = ("g0", "g1", "qa", "qh", "f", "i", "og", "k", "v")
    dst = {}
    off = 0
    for name in order:
        dst[name] = off
        off += src[name][1]
    n_in = off
    l = 0
    tn_in = 512
    assert all(src[n][0] % tn_in == 0 for n in order if n != "v") and (2 * kv_w) % tn_in == 0
    col_blocks = []
    for n in order[:-1]:
        width = src[n][1] if n != "k" else 2 * kv_w
        col_blocks += [src[n][0] // tn_in + c for c in range(width // tn_in)]
    col_blocks = jnp.asarray(col_blocks, jnp.int32)
    w_bf = w_in[l].astype(BF16)

    lb = jnp.cumsum(jax.nn.softmax(hgrn_lb_logits.astype(F32), axis=0), axis=0)[l].reshape(1, hg_w)

    x2 = x.reshape(t, d)
    tm_in = min(2048, t)
    proj = _in_proj(x2, mix_norm_g[l].reshape(1, d), w_bf, col_blocks, tm_in, tn_in)
    proj3 = proj.reshape(b, s, n_in)

    o_a = _attention(proj3, attn_sinks[l].astype(F32), dst["qa"], dst["k"], dst["v"], attn_w, kv_w)
    o_h = _hgrn(proj3, lb, hgrn_norm_g[l].reshape(1, HGRN_HEAD_DIM).astype(F32),
                dst["qh"], dst["f"], dst["i"], dst["og"], hg_w, min(256, s))

    w_r = jnp.zeros((d, ROUTE_LANES), F32)
    w_r = w_r.at[:, :N_GROUPS].set(w_router_group[l])
    w_r = w_r.at[:, EXPERT_LANE0:EXPERT_LANE0 + N_EXPERTS].set(w_router_expert[l])
    wr_hi = w_r.astype(BF16)
    wr_lo = (w_r - wr_hi.astype(F32)).astype(BF16)
    b_r = jnp.zeros((1, ROUTE_LANES), F32)
    b_r = b_r.at[0, :N_GROUPS].set(b_router_group[l])
    b_r = b_r.at[0, EXPERT_LANE0:EXPERT_LANE0 + N_EXPERTS].set(b_router_expert[l])

    tm = min(256, t)
    h, hn, route, plan, cnt = _merge_route(
        o_a.reshape(t, attn_w), o_h.reshape(t, hg_w), proj, x2,
        w_br_attn[l].astype(BF16), w_br_hgrn[l].astype(BF16), w_out[l].astype(BF16),
        moe_norm_g[l].reshape(1, d), wr_hi, wr_lo, b_r, dst["g0"], dst["g1"], tm)

    bm = MOE_ROWS
    counts = cnt[0, EXPERT_LANE0:EXPERT_LANE0 + N_EXPERTS].astype(jnp.int32)
    padded = ((counts + bm - 1) // bm) * bm
    pad_end = jnp.cumsum(padded)
    pad_start = (pad_end - padded).astype(jnp.int32)
    n_blocks = (2 * t) // bm + N_EXPERTS
    n_used = pad_end[-1] // bm
    blk_ids = jnp.minimum(jnp.arange(n_blocks, dtype=jnp.int32), n_used - 1)
    block_expert = jnp.minimum(
        jnp.sum((pad_end[None, :] <= (blk_ids * bm)[:, None]).astype(jnp.int32), axis=1),
        N_EXPERTS - 1)
    meta = jnp.stack([n_used, n_used]).astype(jnp.int32)

    blk_all = jnp.arange(n_blocks, dtype=jnp.int32)
    fill = jnp.concatenate([
        jnp.maximum(pad_end - bm, 0).astype(jnp.int32),
        jnp.minimum(n_used + blk_all, n_blocks - 1) * bm])
    nfill = (N_EXPERTS + n_blocks - n_used).astype(jnp.int32).reshape(1)

    eid = jnp.arange(N_EXPERTS, dtype=jnp.int32)
    nonempty = counts > 0
    slot_e = ((jnp.cumsum(nonempty.astype(jnp.int32)) - 1) % 2).astype(jnp.int32)
    later = jnp.logical_and(nonempty[None, :], eid[None, :] > eid[:, None])
    nxt_e = jnp.min(jnp.where(later, eid[None, :], N_EXPERTS), axis=1)
    nxt_e = jnp.where(nxt_e == N_EXPERTS, -1, nxt_e).astype(jnp.int32)

    nt = t // tm
    dest = _dest_rows(plan, pad_start, min(8, nt))
    rows = _dispatch(hn, dest, fill, nfill, n_blocks * bm, tm, bm)
    y_rows = _experts(rows, block_expert, meta, slot_e, nxt_e, w1[l], w3[l], w2[l], bm)
    out = _combine(h, route, dest, y_rows, final_norm_g.reshape(1, d), tm)
    return out.reshape(b, s, d)
```

```python
import functools

import jax
import jax.numpy as jnp
from jax import lax
from jax.experimental import pallas as pl
from jax.experimental.pallas import tpu as pltpu

F32 = jnp.float32
BF16 = jnp.bfloat16

EPS = 1e-6
ATTN_HEAD_DIM = 64
Q_PER_KV = 4
N_KV_HEADS = 4
ATTN_BLOCK = 128
HGRN_HEAD_DIM = 128
HGRN_CHUNK = 64
N_GROUPS = 4
EXPERTS_PER_GROUP = 8
N_EXPERTS = N_GROUPS * EXPERTS_PER_GROUP
LANES = 128
ROUTE_LANES = LANES
EXPERT_LANE0 = N_GROUPS
MOE_ROWS = 256
PLAN_ROWS = 8
PLAN_E, PLAN_R = 0, 4
VMEM_LIMIT = 56 * 1024 * 1024


def _sigmoid(x):
    return 0.5 * jnp.tanh(0.5 * x) + 0.5


def _cparams(sem, vmem=VMEM_LIMIT):
    return pltpu.CompilerParams(dimension_semantics=sem, vmem_limit_bytes=vmem)


def _to_row_tiles(v):
    m, d = v.shape
    return v.reshape(m, d // LANES, LANES).astype(BF16)


def _from_row_tiles(p):
    m, c, l = p.shape
    return p.reshape(m, c * l)


def _inproj_kernel(cols_ref, x_ref, g_ref, w_ref, o_ref, xn_ref):
    del cols_ref

    @pl.when(pl.program_id(1) == 0)
    def _():
        x = x_ref[...]
        ms = jnp.mean(x * x, axis=-1, keepdims=True)
        xn_ref[...] = (x * lax.rsqrt(ms + EPS) * g_ref[...]).astype(BF16)

    o_ref[...] = jnp.dot(xn_ref[...], w_ref[...], preferred_element_type=F32).astype(o_ref.dtype)


def _in_proj(x2, g, w, col_blocks, tm, tn):
    t, d = x2.shape
    n = w.shape[1]
    grid_spec = pltpu.PrefetchScalarGridSpec(
        num_scalar_prefetch=1,
        grid=(t // tm, n // tn),
        in_specs=[
            pl.BlockSpec((tm, d), lambda i, j, cols: (i, 0)),
            pl.BlockSpec((1, d), lambda i, j, cols: (0, 0)),
            pl.BlockSpec((d, tn), lambda i, j, cols: (0, cols[j])),
        ],
        out_specs=pl.BlockSpec((tm, tn), lambda i, j, cols: (i, j)),
        scratch_shapes=[pltpu.VMEM((tm, d), BF16)],
    )
    return pl.pallas_call(
        _inproj_kernel,
        grid_spec=grid_spec,
        out_shape=jax.ShapeDtypeStruct((t, n), BF16),
        compiler_params=_cparams(("arbitrary", "arbitrary")),
        name="in_proj",
    )(col_blocks, x2, g, w)


def _attn_kernel(sink_ref, q_ref, kc_ref, kp_ref, vc_ref, vp_ref, o_ref):
    n = pl.program_id(1)
    blk, dh, g = ATTN_BLOCK, ATTN_HEAD_DIM, Q_PER_KV
    rows = g * blk
    qi = lax.broadcasted_iota(jnp.int32, (rows, blk), 0) % blk
    kj = lax.broadcasted_iota(jnp.int32, (rows, blk), 1)
    mask_cur = kj <= qi
    prev_bias = jnp.where(n > 0, 0.0, -jnp.inf)
    head_of_row = lax.broadcasted_iota(jnp.int32, (rows, 1), 0) // blk
    nt = (((1,), (1,)), ((), ()))
    scale = dh ** -0.5
    scores = []
    for h in range(N_KV_HEADS):
        kc = kc_ref[:, h * dh:(h + 1) * dh]
        kp = kp_ref[:, h * dh:(h + 1) * dh]
        q4 = jnp.concatenate(
            [q_ref[:, (h * g + j) * dh:(h * g + j + 1) * dh] for j in range(g)], axis=0)
        scores.append((lax.dot_general(q4, kc, nt, preferred_element_type=F32),
                       lax.dot_general(q4, kp, nt, preferred_element_type=F32)))
    for h in range(N_KV_HEADS):
        vc = vc_ref[:, h * dh:(h + 1) * dh]
        vp = vp_ref[:, h * dh:(h + 1) * dh]
        sc, sp = scores[h]
        s = jnp.where(mask_cur, sc, sp + prev_bias) * scale
        sink = jnp.zeros((rows, 1), F32)
        for j in range(g):
            sink = jnp.where(head_of_row == j, sink_ref[h * g + j], sink)
        m = jnp.maximum(jnp.max(s, axis=-1, keepdims=True), sink)
        p = jnp.exp(s - m)
        den = jnp.sum(p, axis=-1, keepdims=True) + jnp.exp(sink - m)
        p_cur = jnp.where(mask_cur, p, 0.0).astype(BF16)
        p_prev = jnp.where(mask_cur, 0.0, p).astype(BF16)
        o = (jnp.dot(p_cur, vc, preferred_element_type=F32)
             + jnp.dot(p_prev, vp, preferred_element_type=F32)) / den
        for j in range(g):
            o_ref[:, (h * g + j) * dh:(h * g + j + 1) * dh] = (
                o[j * blk:(j + 1) * blk, :].astype(o_ref.dtype))


def _attention(proj3, sinks, off_q, off_k, off_v, attn_w, kv_w):
    b, s, _ = proj3.shape
    blk = ATTN_BLOCK
    qb, kb, vb = off_q // attn_w, off_k // kv_w, off_v // kv_w
    prev = lambda n: jnp.maximum(n - 1, 0)
    return pl.pallas_call(
        _attn_kernel,
        grid=(b, s // blk),
        in_specs=[
            pl.BlockSpec(memory_space=pltpu.SMEM),
            pl.BlockSpec((None, blk, attn_w), lambda i, n: (i, n, qb)),
            pl.BlockSpec((None, blk, kv_w), lambda i, n: (i, n, kb)),
            pl.BlockSpec((None, blk, kv_w), lambda i, n: (i, prev(n), kb)),
            pl.BlockSpec((None, blk, kv_w), lambda i, n: (i, n, vb)),
            pl.BlockSpec((None, blk, kv_w), lambda i, n: (i, prev(n), vb)),
        ],
        out_specs=pl.BlockSpec((None, blk, attn_w), lambda i, n: (i, n, 0)),
        out_shape=jax.ShapeDtypeStruct((b, s, attn_w), BF16),
        compiler_params=_cparams(("arbitrary", "arbitrary")),
        name="swa_attention",
    )(sinks, proj3, proj3, proj3, proj3, proj3)


def _hgrn_kernel(q_ref, f_ref, i_ref, og_ref, lb_ref, gn_ref, o_ref,
                 st_ref, gh_s, f_s, b_s, qt_s, kt_s, qe_s, kd_s, dl_s, oi_s, u_s, a_s,
                 *, n_heads, n_chunks):
    c, dk = HGRN_CHUNK, HGRN_HEAD_DIM
    w = n_heads * dk

    @pl.when(pl.program_id(1) == 0)
    def _():
        st_ref[...] = jnp.zeros_like(st_ref)

    ti = lax.broadcasted_iota(jnp.int32, (c, c), 0)
    si = lax.broadcasted_iota(jnp.int32, (c, c), 1)
    causal = si <= ti
    tri = causal.astype(BF16)
    nt = (((1,), (1,)), ((), ()))
    tn = (((0,), (0,)), ((), ()))
    qscale = dk ** -0.5
    chunk_rows = [slice(ci * c, (ci + 1) * c) for ci in range(n_chunks)]
    head_cols = [slice(h * dk, (h + 1) * dk) for h in range(n_heads)]

    lb = lb_ref[...]
    f = lb + (1.0 - lb) * _sigmoid(f_ref[...].astype(F32))
    f_s[...] = f
    gl = jnp.log(f)
    g_hi = gl.astype(BF16)
    gh_s[:, :w] = g_hi
    gh_s[:, w:] = (gl - g_hi.astype(F32)).astype(BF16)

    for rows in chunk_rows:
        bcat = jnp.dot(tri, gh_s[rows, :], preferred_element_type=F32)
        b_s[rows, :] = bcat[:, :w] + bcat[:, w:]

    for ci, rows in enumerate(chunk_rows):
        bc = b_s[rows, :]
        b_last = bc[c - 1:c, :]
        r = 0.5 * b_last
        qv = q_ref[rows, :].astype(F32)
        qt = qv * _sigmoid(qv) * qscale * jnp.exp(bc - r)
        kt = (1.0 - f_s[rows, :]) * jnp.exp(r - bc)
        qt_s[rows, :] = qt.astype(BF16)
        kt_s[rows, :] = kt.astype(BF16)
        qe_s[rows, :] = (qt * jnp.exp(r)).astype(BF16)
        kd_s[rows, :] = (kt * jnp.exp(b_last - r)).astype(BF16)
        dl_s[ci:ci + 1, :] = jnp.exp(b_last)

    units = [(ci, rows, h, cols) for ci, rows in enumerate(chunk_rows)
             for h, cols in enumerate(head_cols)]
    for ci, rows, h, cols in units:
        a = lax.dot_general(qt_s[rows, cols], kt_s[rows, cols], nt, preferred_element_type=F32)
        a_s[ci * n_heads + h] = jnp.where(causal, a, 0.0).astype(BF16)
    for ci, rows, h, cols in units:
        u_s[ci, h] = lax.dot_general(
            i_ref[rows, cols], kd_s[rows, cols], tn, preferred_element_type=F32)
    for ci, rows, h, cols in units:
        oi_s[rows, cols] = jnp.dot(
            a_s[ci * n_heads + h], i_ref[rows, cols], preferred_element_type=F32)

    gn = gn_ref[...]
    for ci, rows in enumerate(chunk_rows):
        for h, cols in enumerate(head_cols):
            st = st_ref[h]
            o = oi_s[rows, cols] + lax.dot_general(
                qe_s[rows, cols], st.astype(BF16), nt, preferred_element_type=F32)
            st_ref[h] = st * dl_s[ci:ci + 1, cols] + u_s[ci, h]
            ms = jnp.mean(o * o, axis=-1, keepdims=True)
            ogv = og_ref[rows, cols].astype(F32)
            o_ref[rows, cols] = (
                o * lax.rsqrt(ms + EPS) * gn * (ogv * _sigmoid(ogv))).astype(o_ref.dtype)


def _hgrn(proj3, lb, gn, off_q, off_f, off_i, off_og, width, ts):
    b, s, _ = proj3.shape
    n_heads = width // HGRN_HEAD_DIM
    n_chunks = ts // HGRN_CHUNK
    dk = HGRN_HEAD_DIM
    spec = lambda off: pl.BlockSpec((None, ts, width), lambda i, t: (i, t, off // width))
    return pl.pallas_call(
        functools.partial(_hgrn_kernel, n_heads=n_heads, n_chunks=n_chunks),
        grid=(b, s // ts),
        in_specs=[
            spec(off_q), spec(off_f), spec(off_i), spec(off_og),
            pl.BlockSpec((1, width), lambda i, t: (0, 0)),
            pl.BlockSpec((1, dk), lambda i, t: (0, 0)),
        ],
        out_specs=pl.BlockSpec((None, ts, width), lambda i, t: (i, t, 0)),
        out_shape=jax.ShapeDtypeStruct((b, s, width), BF16),
        scratch_shapes=[
            pltpu.VMEM((n_heads, dk, dk), F32),
            pltpu.VMEM((ts, 2 * width), BF16),
            pltpu.VMEM((ts, width), F32),
            pltpu.VMEM((ts, width), F32),
            pltpu.VMEM((ts, width), BF16),
            pltpu.VMEM((ts, width), BF16),
            pltpu.VMEM((ts, width), BF16),
            pltpu.VMEM((ts, width), BF16),
            pltpu.VMEM((max(n_chunks, 8), width), F32),
            pltpu.VMEM((ts, width), F32),
            pltpu.VMEM((n_chunks, n_heads, dk, dk), F32),
            pltpu.VMEM((n_chunks * n_heads, HGRN_CHUNK, HGRN_CHUNK), BF16),
        ],
        compiler_params=_cparams(("arbitrary", "arbitrary")),
        name="hgrn2",
    )(proj3, proj3, proj3, proj3, lb, gn)


def _merge_kernel(oa_ref, oh_ref, g0_ref, g1_ref, x_ref, wa_ref, wh_ref, wo_ref, gm_ref,
                  wrh_ref, wrl_ref, br_ref, h_ref, hn_ref, route_ref, plan_ref, cnt_ref,
                  hn_s):
    tm, d = x_ref.shape
    half = d // 2
    halves = (slice(0, half), slice(half, d))
    step = pl.program_id(0)

    @pl.when(step == 0)
    def _():
        cnt_ref[...] = jnp.zeros_like(cnt_ref)
        hn_s[...] = jnp.zeros_like(hn_s)

    hn_prev = hn_s[...]
    hn_hi = hn_prev.astype(BF16)
    hn_lo = (hn_prev - hn_hi.astype(F32)).astype(BF16)

    ya = [jnp.dot(oa_ref[...], wa_ref[:, cs], preferred_element_type=F32) for cs in halves]
    yh = [jnp.dot(oh_ref[...], wh_ref[:, cs], preferred_element_type=F32) for cs in halves]

    logits = (jnp.dot(hn_hi, wrh_ref[...], preferred_element_type=F32)
              + jnp.dot(hn_hi, wrl_ref[...], preferred_element_type=F32)
              + jnp.dot(hn_lo, wrh_ref[...], preferred_element_type=F32)
              + br_ref[...])
    lane = lax.broadcasted_iota(jnp.int32, (tm, ROUTE_LANES), 1)
    neg = -jnp.inf

    def first_argmax(v):
        m = jnp.max(v, axis=-1, keepdims=True)
        idx = jnp.min(jnp.where(v == m, lane, ROUTE_LANES), axis=-1, keepdims=True)
        return m, idx

    is_group = lane < N_GROUPS
    gmax, gidx = first_argmax(jnp.where(is_group, logits, neg))
    p_sel = 1.0 / jnp.sum(jnp.where(is_group, jnp.exp(logits - gmax), 0.0), axis=-1, keepdims=True)
    eidx = lane - EXPERT_LANE0
    in_group = jnp.logical_and(
        jnp.logical_and(eidx >= 0, eidx < N_EXPERTS), (eidx // EXPERTS_PER_GROUP) == gidx)
    el = jnp.where(in_group, logits, neg)
    m1, i1 = first_argmax(el)
    m2, i2 = first_argmax(jnp.where(lane == i1, neg, el))
    t = jnp.exp(m2 - m1)
    w1 = p_sel / (1.0 + t)
    w2 = p_sel * t / (1.0 + t)

    sel1 = lane == i1
    sel2 = lane == i2
    onehot = jnp.logical_or(sel1, sel2).astype(BF16)
    ri = lax.broadcasted_iota(jnp.int32, (tm, tm), 0)
    ci = lax.broadcasted_iota(jnp.int32, (tm, tm), 1)
    before = (ci < ri).astype(BF16)

    h = x_ref[...]
    for k, cs in enumerate(halves):
        merged = (_sigmoid(g0_ref[:, cs].astype(F32)) * ya[k]
                  + _sigmoid(g1_ref[:, cs].astype(F32)) * yh[k]).astype(BF16)
        h = h + jnp.dot(merged, wo_ref[cs, :], preferred_element_type=F32)
        if k == 0:
            cum = jnp.dot(before, onehot, preferred_element_type=F32) + cnt_ref[...]

    r1 = jnp.sum(jnp.where(sel1, cum, 0.0), axis=-1, keepdims=True)
    r2 = jnp.sum(jnp.where(sel2, cum, 0.0), axis=-1, keepdims=True)
    routed = jnp.where(step > 0, 1.0, 0.0)
    cnt_ref[...] += routed * jnp.sum(onehot.astype(F32), axis=0, keepdims=True)

    e1 = (i1 - EXPERT_LANE0).astype(F32)
    e2 = (i2 - EXPERT_LANE0).astype(F32)
    out = jnp.zeros((tm, ROUTE_LANES), F32)
    for k, v in enumerate((e1, e2, w1, w2, r1, r2)):
        out = jnp.where(lane == k, v, out)
    route_ref[...] = out
    plan_ref[...] = out.T[:PLAN_ROWS, :].astype(jnp.int32)

    h_ref[...] = h
    ms = jnp.mean(h * h, axis=-1, keepdims=True)
    hn = h * lax.rsqrt(ms + EPS) * gm_ref[...]
    hn_ref[...] = _to_row_tiles(hn)
    hn_s[...] = hn


def _merge_route(o_a, o_h, proj, x2, wa, wh, wo, gm, wr_hi, wr_lo, br, off_g0, off_g1, tm):
    t, d = x2.shape
    nt = t // tm
    aw, hw = o_a.shape[1], o_h.shape[1]
    const = lambda shape: pl.BlockSpec(shape, lambda i: (0, 0), pipeline_mode=pl.Buffered(1))
    cur = lambda i: jnp.minimum(i, nt - 1)
    prv = lambda i: jnp.maximum(i - 1, 0)
    row = lambda w: pl.BlockSpec((tm, w), lambda i: (cur(i), 0))
    return pl.pallas_call(
        _merge_kernel,
        grid=(nt + 1,),
        in_specs=[
            row(aw), row(hw),
            pl.BlockSpec((tm, d), lambda i: (cur(i), off_g0 // d)),
            pl.BlockSpec((tm, d), lambda i: (cur(i), off_g1 // d)),
            row(d),
            const((aw, d)), const((hw, d)), const((d, d)), const((1, d)),
            const((d, ROUTE_LANES)), const((d, ROUTE_LANES)), const((1, ROUTE_LANES)),
        ],
        out_specs=[row(d), pl.BlockSpec((tm, d // LANES, LANES), lambda i: (cur(i), 0, 0)),
                   pl.BlockSpec((tm, ROUTE_LANES), lambda i: (prv(i), 0)),
                   pl.BlockSpec((None, PLAN_ROWS, tm), lambda i: (prv(i), 0, 0)),
                   pl.BlockSpec((1, ROUTE_LANES), lambda i: (0, 0))],
        out_shape=[
            jax.ShapeDtypeStruct((t, d), F32),
            jax.ShapeDtypeStruct((t, d // LANES, LANES), BF16),
            jax.ShapeDtypeStruct((t, ROUTE_LANES), F32),
            jax.ShapeDtypeStruct((nt, PLAN_ROWS, tm), jnp.int32),
            jax.ShapeDtypeStruct((1, ROUTE_LANES), F32),
        ],
        scratch_shapes=[pltpu.VMEM((tm, d), F32)],
        compiler_params=_cparams(("arbitrary",)),
        name="merge_route",
    )(o_a, o_h, proj, proj, x2, wa, wh, wo, gm, wr_hi, wr_lo, br)


def _dest_kernel(pst_ref, plan_ref, dest_ref):
    e = plan_ref[:, PLAN_E:PLAN_E + 2, :]
    start = jnp.zeros(e.shape, jnp.int32)
    for j in range(N_EXPERTS):
        start = jnp.where(e == j, pst_ref[j], start)
    dest_ref[...] = start + plan_ref[:, PLAN_R:PLAN_R + 2, :]


def _dest_rows(plan, pst, tiles):
    nt, _, tm = plan.shape
    grid_spec = pltpu.PrefetchScalarGridSpec(
        num_scalar_prefetch=1,
        grid=(nt // tiles,),
        in_specs=[pl.BlockSpec((tiles, PLAN_ROWS, tm), lambda i, *_: (i, 0, 0))],
        out_specs=pl.BlockSpec((tiles, 2, tm), lambda i, *_: (i, 0, 0)),
    )
    return pl.pallas_call(
        _dest_kernel,
        grid_spec=grid_spec,
        out_shape=jax.ShapeDtypeStruct((nt, 2, tm), jnp.int32),
        compiler_params=_cparams(("arbitrary",)),
        name="moe_dest",
    )(pst, plan)


ISSUE_UNROLL = 8


def _dispatch_kernel(fill_ref, nfill_ref, dest_ref, hn_ref, rows_ref, zbuf, sem):
    tm = hn_ref.shape[0]
    bm = zbuf.shape[0]

    @pl.when(pl.program_id(0) == 0)
    def _():
        zbuf[...] = jnp.zeros_like(zbuf)

        def fill(j, carry):
            start = pl.multiple_of(fill_ref[j], bm)
            pltpu.make_async_copy(zbuf, rows_ref.at[pl.ds(start, bm)], sem).start()
            return carry

        def drain(j, carry):
            pltpu.make_async_copy(zbuf, rows_ref.at[pl.ds(0, bm)], sem).wait()
            return carry

        lax.fori_loop(0, nfill_ref[0], fill, 0)
        lax.fori_loop(0, nfill_ref[0], drain, 0)

    def issue(r, carry):
        for k in range(2):
            pltpu.make_async_copy(
                hn_ref.at[r], rows_ref.at[dest_ref[k, r]], sem).start(priority=k)
        return carry

    lax.fori_loop(0, tm, issue, 0, unroll=ISSUE_UNROLL)
    for _ in range(2):
        pltpu.make_async_copy(hn_ref, rows_ref.at[pl.ds(0, tm)], sem).wait()


def _dispatch(hn, dest, fill, nfill, n_rows, tm, bm):
    t, c, l = hn.shape
    grid_spec = pltpu.PrefetchScalarGridSpec(
        num_scalar_prefetch=2,
        grid=(t // tm,),
        in_specs=[
            pl.BlockSpec((None, 2, tm), lambda i, *_: (i, 0, 0), memory_space=pltpu.SMEM),
            pl.BlockSpec((tm, c, l), lambda i, *_: (i, 0, 0)),
        ],
        out_specs=pl.BlockSpec(memory_space=pl.ANY),
        scratch_shapes=[pltpu.VMEM((bm, c, l), hn.dtype), pltpu.SemaphoreType.DMA],
    )
    return pl.pallas_call(
        _dispatch_kernel,
        grid_spec=grid_spec,
        out_shape=jax.ShapeDtypeStruct((n_rows, c, l), hn.dtype),
        compiler_params=_cparams(("arbitrary",)),
        name="moe_dispatch",
    )(fill, nfill, dest, hn)


def _expert_kernel(be_ref, meta_ref, slot_ref, nxt_ref, x_ref, w1_hbm, w3_hbm, w2_hbm, y_ref,
                   wf1, wf3, wf2, w1b, w3b, w2b, sems):
    i = pl.program_id(0)
    e = be_ref[i]
    new_expert = jnp.logical_or(i == 0, e != be_ref[jnp.maximum(i - 1, 0)])

    def weight_copies(expert, slot):
        return [pltpu.make_async_copy(src.at[expert], dst.at[slot], sems.at[slot])
                for src, dst in ((w1_hbm, wf1), (w3_hbm, wf3), (w2_hbm, wf2))]

    @pl.when(jnp.logical_and(i < meta_ref[0], new_expert))
    def _():
        slot = slot_ref[e]
        nxt = nxt_ref[e]

        @pl.when(i == 0)
        def _():
            for c in weight_copies(e, slot):
                c.start()

        @pl.when(nxt >= 0)
        def _():
            for c in weight_copies(nxt, 1 - slot):
                c.start()

        for c in weight_copies(e, slot):
            c.wait()
        w1b[...] = wf1[slot].astype(BF16)
        w3b[...] = wf3[slot].astype(BF16)
        w2b[...] = wf2[slot].astype(BF16)

    @pl.when(i < meta_ref[0])
    def _():
        x = _from_row_tiles(x_ref[...])
        h1 = jnp.dot(x, w1b[...], preferred_element_type=F32)
        h3 = jnp.dot(x, w3b[...], preferred_element_type=F32)
        hb = (h1 * _sigmoid(h1) * h3).astype(BF16)
        y_ref[...] = _to_row_tiles(jnp.dot(hb, w2b[...], preferred_element_type=F32))

    @pl.when(i >= meta_ref[0])
    def _():
        y_ref[...] = jnp.zeros_like(y_ref)


def _experts(rows, block_expert, meta, slot_e, nxt_e, w1, w3, w2, bm):
    n_rows, c, l = rows.shape
    d = c * l
    de = w1.shape[-1]
    n_blocks = n_rows // bm
    hbm = pl.BlockSpec(memory_space=pl.ANY)
    grid_spec = pltpu.PrefetchScalarGridSpec(
        num_scalar_prefetch=4,
        grid=(n_blocks,),
        in_specs=[
            pl.BlockSpec((bm, c, l), lambda i, be, meta, *_: (jnp.minimum(i, meta[0] - 1), 0, 0)),
            hbm, hbm, hbm,
        ],
        out_specs=pl.BlockSpec((bm, c, l), lambda i, *_: (i, 0, 0)),
        scratch_shapes=[
            pltpu.VMEM((2, d, de), w1.dtype), pltpu.VMEM((2, d, de), w3.dtype),
            pltpu.VMEM((2, de, d), w2.dtype),
            pltpu.VMEM((d, de), BF16), pltpu.VMEM((d, de), BF16), pltpu.VMEM((de, d), BF16),
            pltpu.SemaphoreType.DMA((2,)),
        ],
    )
    return pl.pallas_call(
        _expert_kernel,
        grid_spec=grid_spec,
        out_shape=jax.ShapeDtypeStruct((n_rows, c, l), BF16),
        compiler_params=_cparams(("arbitrary",)),
        name="moe_experts",
    )(block_expert, meta, slot_e, nxt_e, rows, w1, w3, w2)


def _combine_kernel(dcur_ref, dnext_ref, h_ref, route_ref, g_ref, y_ref, o_ref, ybuf, sems):
    i = pl.program_id(0)
    nt = pl.num_programs(0)
    tm, d = h_ref.shape
    slot = i % 2

    def gather(dest_ref, s):
        def issue(r, carry):
            for k in range(2):
                pltpu.make_async_copy(
                    y_ref.at[dest_ref[k, r]], ybuf.at[s, k, r], sems.at[s]).start(priority=k)
            return carry
        lax.fori_loop(0, tm, issue, 0, unroll=ISSUE_UNROLL)

    @pl.when(i == 0)
    def _():
        gather(dcur_ref, slot)

    @pl.when(i + 1 < nt)
    def _():
        gather(dnext_ref, 1 - slot)

    for k in range(2):
        pltpu.make_async_copy(y_ref.at[pl.ds(0, tm)], ybuf.at[slot, k], sems.at[slot]).wait()

    route = route_ref[...]
    w1 = route[:, 2:3]
    w2 = route[:, 3:4]
    ya = _from_row_tiles(ybuf[slot, 0]).astype(F32)
    yb = _from_row_tiles(ybuf[slot, 1]).astype(F32)
    h = h_ref[...] + w1 * ya + w2 * yb
    ms = jnp.mean(h * h, axis=-1, keepdims=True)
    o_ref[...] = h * lax.rsqrt(ms + EPS) * g_ref[...]


def _combine(h, route, dest, y_rows, g, tm):
    t, d = h.shape
    nt = t // tm
    dest_spec = lambda f: pl.BlockSpec((None, 2, tm), f, memory_space=pltpu.SMEM)
    return pl.pallas_call(
        _combine_kernel,
        grid=(nt,),
        in_specs=[
            dest_spec(lambda i: (i, 0, 0)),
            dest_spec(lambda i: (jnp.minimum(i + 1, nt - 1), 0, 0)),
            pl.BlockSpec((tm, d), lambda i: (i, 0)),
            pl.BlockSpec((tm, ROUTE_LANES), lambda i: (i, 0)),
            pl.BlockSpec((1, d), lambda i: (0, 0)),
            pl.BlockSpec(memory_space=pl.ANY),
        ],
        out_specs=pl.BlockSpec((tm, d), lambda i: (i, 0)),
        out_shape=jax.ShapeDtypeStruct((t, d), F32),
        scratch_shapes=[pltpu.VMEM((2, 2, tm, d // LANES, LANES), BF16),
                        pltpu.SemaphoreType.DMA((2,))],
        compiler_params=_cparams(("arbitrary",)),
        name="moe_combine",
    )(dest, dest, h, route, g, y_rows)


def kernel(x, w_in, attn_sinks, hgrn_lb_logits, hgrn_norm_g, w_br_attn, w_br_hgrn, w_out,
           mix_norm_g, moe_norm_g, w_router_group, b_router_group, w_router_expert,
           b_router_expert, w1, w3, w2, final_norm_g):
    b, s, d = x.shape
    t = b * s
    depth = w_in.shape[0]
    assert depth == 1
    attn_w = (d // 128) * ATTN_HEAD_DIM
    kv_w = attn_w // Q_PER_KV
    hg_w = (d // 256) * HGRN_HEAD_DIM
    assert kv_w == N_KV_HEADS * ATTN_HEAD_DIM

    src = {}
    off = 0
    for name, width in (("qa", attn_w), ("k", kv_w), ("v", kv_w), ("qh", hg_w), ("f", hg_w),
                        ("i", hg_w), ("og", hg_w), ("g0", d), ("g1", d)):
        src[name] = (off, width)
        off += width
    order = ("g0", "g1", "qa", "qh", "f", "i", "og", "k", "v")
    dst = {}
    off = 0
    for name in order:
        dst[name] = off
        off += src[name][1]
    n_in = off
    l = 0
    tn_in = 512
    assert all(src[n][0] % tn_in == 0 for n in order if n != "v") and (2 * kv_w) % tn_in == 0
    col_blocks = []
    for n in order[:-1]:
        width = src[n][1] if n != "k" else 2 * kv_w
        col_blocks += [src[n][0] // tn_in + c for c in range(width // tn_in)]
    col_blocks = jnp.asarray(col_blocks, jnp.int32)
    w_bf = w_in[l].astype(BF16)

    lb = jnp.cumsum(jax.nn.softmax(hgrn_lb_logits.astype(F32), axis=0), axis=0)[l].reshape(1, hg_w)

    x2 = x.reshape(t, d)
    tm_in = min(2048, t)
    proj = _in_proj(x2, mix_norm_g[l].reshape(1, d), w_bf, col_blocks, tm_in, tn_in)
    proj3 = proj.reshape(b, s, n_in)

    o_a = _attention(proj3, attn_sinks[l].astype(F32), dst["qa"], dst["k"], dst["v"], attn_w, kv_w)
    o_h = _hgrn(proj3, lb, hgrn_norm_g[l].reshape(1, HGRN_HEAD_DIM).astype(F32),
                dst["qh"], dst["f"], dst["i"], dst["og"], hg_w, min(256, s))

    w_r = jnp.zeros((d, ROUTE_LANES), F32)
    w_r = w_r.at[:, :N_GROUPS].set(w_router_group[l])
    w_r = w_r.at[:, EXPERT_LANE0:EXPERT_LANE0 + N_EXPERTS].set(w_router_expert[l])
    wr_hi = w_r.astype(BF16)
    wr_lo = (w_r - wr_hi.astype(F32)).astype(BF16)
    b_r = jnp.zeros((1, ROUTE_LANES), F32)
    b_r = b_r.at[0, :N_GROUPS].set(b_router_group[l])
    b_r = b_r.at[0, EXPERT_LANE0:EXPERT_LANE0 + N_EXPERTS].set(b_router_expert[l])

    tm = min(256, t)
    h, hn, route, plan, cnt = _merge_route(
        o_a.reshape(t, attn_w), o_h.reshape(t, hg_w), proj, x2,
        w_br_attn[l].astype(BF16), w_br_hgrn[l].astype(BF16), w_out[l].astype(BF16),
        moe_norm_g[l].reshape(1, d), wr_hi, wr_lo, b_r, dst["g0"], dst["g1"], tm)

    bm = MOE_ROWS
    counts = cnt[0, EXPERT_LANE0:EXPERT_LANE0 + N_EXPERTS].astype(jnp.int32)
    padded = ((counts + bm - 1) // bm) * bm
    pad_end = jnp.cumsum(padded)
    pad_start = (pad_end - padded).astype(jnp.int32)
    n_blocks = (2 * t) // bm + N_EXPERTS
    n_used = pad_end[-1] // bm
    blk_ids = jnp.minimum(jnp.arange(n_blocks, dtype=jnp.int32), n_used - 1)
    block_expert = jnp.minimum(
        jnp.sum((pad_end[None, :] <= (blk_ids * bm)[:, None]).astype(jnp.int32), axis=1),
        N_EXPERTS - 1)
    meta = jnp.stack([n_used, n_used]).astype(jnp.int32)

    blk_all = jnp.arange(n_blocks, dtype=jnp.int32)
    fill = jnp.concatenate([
        jnp.maximum(pad_end - bm, 0).astype(jnp.int32),
        jnp.minimum(n_used + blk_all, n_blocks - 1) * bm])
    nfill = (N_EXPERTS + n_blocks - n_used).astype(jnp.int32).reshape(1)

    eid = jnp.arange(N_EXPERTS, dtype=jnp.int32)
    nonempty = counts > 0
    slot_e = ((jnp.cumsum(nonempty.astype(jnp.int32)) - 1) % 2).astype(jnp.int32)
    later = jnp.logical_and(nonempty[None, :], eid[None, :] > eid[:, None])
    nxt_e = jnp.min(jnp.where(later, eid[None, :], N_EXPERTS), axis=1)
    nxt_e = jnp.where(nxt_e == N_EXPERTS, -1, nxt_e).astype(jnp.int32)

    nt = t // tm
    dest = _dest_rows(plan, pad_start, min(8, nt))
    rows = _dispatch(hn, dest, fill, nfill, n_blocks * bm, tm, bm)
    y_rows = _experts(rows, block_expert, meta, slot_e, nxt_e, w1[l], w3[l], w2[l], bm)
    out = _combine(h, route, dest, y_rows, final_norm_g.reshape(1, d), tm)
    return out.reshape(b, s, d)
```

```python
import functools

import jax
import jax.numpy as jnp
from jax import lax
from jax.experimental import pallas as pl
from jax.experimental.pallas import tpu as pltpu

F32 = jnp.float32
BF16 = jnp.bfloat16

EPS = 1e-6
ATTN_HEAD_DIM = 64
Q_PER_KV = 4
N_KV_HEADS = 4
ATTN_BLOCK = 128
HGRN_HEAD_DIM = 128
HGRN_CHUNK = 64
N_GROUPS = 4
EXPERTS_PER_GROUP = 8
N_EXPERTS = N_GROUPS * EXPERTS_PER_GROUP
LANES = 128
ROUTE_LANES = LANES
EXPERT_LANE0 = N_GROUPS
MOE_ROWS = 256
PLAN_ROWS = 8
PLAN_E, PLAN_R = 0, 4
VMEM_LIMIT = 56 * 1024 * 1024


def _sigmoid(x):
    return 0.5 * jnp.tanh(0.5 * x) + 0.5


def _cparams(sem, vmem=VMEM_LIMIT):
    return pltpu.CompilerParams(dimension_semantics=sem, vmem_limit_bytes=vmem)


def _to_row_tiles(v):
    m, d = v.shape
    return v.reshape(m, d // LANES, LANES).astype(BF16)


def _from_row_tiles(p):
    m, c, l = p.shape
    return p.reshape(m, c * l)


def _inproj_kernel(cols_ref, x_ref, g_ref, w_ref, o_ref, xn_ref):
    del cols_ref

    @pl.when(pl.program_id(1) == 0)
    def _():
        x = x_ref[...]
        ms = jnp.mean(x * x, axis=-1, keepdims=True)
        xn_ref[...] = (x * lax.rsqrt(ms + EPS) * g_ref[...]).astype(BF16)

    o_ref[...] = jnp.dot(xn_ref[...], w_ref[...], preferred_element_type=F32).astype(o_ref.dtype)


def _in_proj(x2, g, w, col_blocks, tm, tn):
    t, d = x2.shape
    n = w.shape[1]
    grid_spec = pltpu.PrefetchScalarGridSpec(
        num_scalar_prefetch=1,
        grid=(t // tm, n // tn),
        in_specs=[
            pl.BlockSpec((tm, d), lambda i, j, cols: (i, 0)),
            pl.BlockSpec((1, d), lambda i, j, cols: (0, 0)),
            pl.BlockSpec((d, tn), lambda i, j, cols: (0, cols[j])),
        ],
        out_specs=pl.BlockSpec((tm, tn), lambda i, j, cols: (i, j)),
        scratch_shapes=[pltpu.VMEM((tm, d), BF16)],
    )
    return pl.pallas_call(
        _inproj_kernel,
        grid_spec=grid_spec,
        out_shape=jax.ShapeDtypeStruct((t, n), BF16),
        compiler_params=_cparams(("arbitrary", "arbitrary")),
        name="in_proj",
    )(col_blocks, x2, g, w)


def _attn_kernel(sink_ref, q_ref, kc_ref, kp_ref, vc_ref, vp_ref, o_ref):
    n = pl.program_id(1)
    blk, dh, g = ATTN_BLOCK, ATTN_HEAD_DIM, Q_PER_KV
    rows = g * blk
    qi = lax.broadcasted_iota(jnp.int32, (rows, blk), 0) % blk
    kj = lax.broadcasted_iota(jnp.int32, (rows, blk), 1)
    mask_cur = kj <= qi
    prev_bias = jnp.where(n > 0, 0.0, -jnp.inf)
    head_of_row = lax.broadcasted_iota(jnp.int32, (rows, 1), 0) // blk
    nt = (((1,), (1,)), ((), ()))
    scale = dh ** -0.5
    scores = []
    for h in range(N_KV_HEADS):
        kc = kc_ref[:, h * dh:(h + 1) * dh]
        kp = kp_ref[:, h * dh:(h + 1) * dh]
        q4 = jnp.concatenate(
            [q_ref[:, (h * g + j) * dh:(h * g + j + 1) * dh] for j in range(g)], axis=0)
        scores.append((lax.dot_general(q4, kc, nt, preferred_element_type=F32),
                       lax.dot_general(q4, kp, nt, preferred_element_type=F32)))
    for h in range(N_KV_HEADS):
        vc = vc_ref[:, h * dh:(h + 1) * dh]
        vp = vp_ref[:, h * dh:(h + 1) * dh]
        sc, sp = scores[h]
        s = jnp.where(mask_cur, sc, sp + prev_bias) * scale
        sink = jnp.zeros((rows, 1), F32)
        for j in range(g):
            sink = jnp.where(head_of_row == j, sink_ref[h * g + j], sink)
        m = jnp.maximum(jnp.max(s, axis=-1, keepdims=True), sink)
        p = jnp.exp(s - m)
        den = jnp.sum(p, axis=-1, keepdims=True) + jnp.exp(sink - m)
        p_cur = jnp.where(mask_cur, p, 0.0).astype(BF16)
        p_prev = jnp.where(mask_cur, 0.0, p).astype(BF16)
        o = (jnp.dot(p_cur, vc, preferred_element_type=F32)
             + jnp.dot(p_prev, vp, preferred_element_type=F32)) / den
        for j in range(g):
            o_ref[:, (h * g + j) * dh:(h * g + j + 1) * dh] = (
                o[j * blk:(j + 1) * blk, :].astype(o_ref.dtype))


def _attention(proj3, sinks, off_q, off_k, off_v, attn_w, kv_w):
    b, s, _ = proj3.shape
    blk = ATTN_BLOCK
    qb, kb, vb = off_q // attn_w, off_k // kv_w, off_v // kv_w
    prev = lambda n: jnp.maximum(n - 1, 0)
    return pl.pallas_call(
        _attn_kernel,
        grid=(b, s // blk),
        in_specs=[
            pl.BlockSpec(memory_space=pltpu.SMEM),
            pl.BlockSpec((None, blk, attn_w), lambda i, n: (i, n, qb)),
            pl.BlockSpec((None, blk, kv_w), lambda i, n: (i, n, kb)),
            pl.BlockSpec((None, blk, kv_w), lambda i, n: (i, prev(n), kb)),
            pl.BlockSpec((None, blk, kv_w), lambda i, n: (i, n, vb)),
            pl.BlockSpec((None, blk, kv_w), lambda i, n: (i, prev(n), vb)),
        ],
        out_specs=pl.BlockSpec((None, blk, attn_w), lambda i, n: (i, n, 0)),
        out_shape=jax.ShapeDtypeStruct((b, s, attn_w), BF16),
        compiler_params=_cparams(("arbitrary", "arbitrary")),
        name="swa_attention",
    )(sinks, proj3, proj3, proj3, proj3, proj3)


def _hgrn_kernel(q_ref, f_ref, i_ref, og_ref, lb_ref, gn_ref, o_ref,
                 st_ref, gh_s, f_s, b_s, qt_s, kt_s, qe_s, kd_s, dl_s, oi_s, u_s, a_s,
                 *, n_heads, n_chunks):
    c, dk = HGRN_CHUNK, HGRN_HEAD_DIM
    w = n_heads * dk

    @pl.when(pl.program_id(1) == 0)
    def _():
        st_ref[...] = jnp.zeros_like(st_ref)

    ti = lax.broadcasted_iota(jnp.int32, (c, c), 0)
    si = lax.broadcasted_iota(jnp.int32, (c, c), 1)
    causal = si <= ti
    tri = causal.astype(BF16)
    nt = (((1,), (1,)), ((), ()))
    tn = (((0,), (0,)), ((), ()))
    qscale = dk ** -0.5
    chunk_rows = [slice(ci * c, (ci + 1) * c) for ci in range(n_chunks)]
    head_cols = [slice(h * dk, (h + 1) * dk) for h in range(n_heads)]

    lb = lb_ref[...]
    f = lb + (1.0 - lb) * _sigmoid(f_ref[...].astype(F32))
    f_s[...] = f
    gl = jnp.log(f)
    g_hi = gl.astype(BF16)
    gh_s[:, :w] = g_hi
    gh_s[:, w:] = (gl - g_hi.astype(F32)).astype(BF16)

    for rows in chunk_rows:
        bcat = jnp.dot(tri, gh_s[rows, :], preferred_element_type=F32)
        b_s[rows, :] = bcat[:, :w] + bcat[:, w:]

    for ci, rows in enumerate(chunk_rows):
        bc = b_s[rows, :]
        b_last = bc[c - 1:c, :]
        r = 0.5 * b_last
        qv = q_ref[rows, :].astype(F32)
        qt = qv * _sigmoid(qv) * qscale * jnp.exp(bc - r)
        kt = (1.0 - f_s[rows, :]) * jnp.exp(r - bc)
        qt_s[rows, :] = qt.astype(BF16)
        kt_s[rows, :] = kt.astype(BF16)
        qe_s[rows, :] = (qt * jnp.exp(r)).astype(BF16)
        kd_s[rows, :] = (kt * jnp.exp(b_last - r)).astype(BF16)
        dl_s[ci:ci + 1, :] = jnp.exp(b_last)

    units = [(ci, rows, h, cols) for ci, rows in enumerate(chunk_rows)
             for h, cols in enumerate(head_cols)]
    for ci, rows, h, cols in units:
        a = lax.dot_general(qt_s[rows, cols], kt_s[rows, cols], nt, preferred_element_type=F32)
        a_s[ci * n_heads + h] = jnp.where(causal, a, 0.0).astype(BF16)
    for ci, rows, h, cols in units:
        u_s[ci, h] = lax.dot_general(
            i_ref[rows, cols], kd_s[rows, cols], tn, preferred_element_type=F32)
    for ci, rows, h, cols in units:
        oi_s[rows, cols] = jnp.dot(
            a_s[ci * n_heads + h], i_ref[rows, cols], preferred_element_type=F32)

    gn = gn_ref[...]
    for ci, rows in enumerate(chunk_rows):
        for h, cols in enumerate(head_cols):
            st = st_ref[h]
            o = oi_s[rows, cols] + lax.dot_general(
                qe_s[rows, cols], st.astype(BF16), nt, preferred_element_type=F32)
            st_ref[h] = st * dl_s[ci:ci + 1, cols] + u_s[ci, h]
            ms = jnp.mean(o * o, axis=-1, keepdims=True)
            ogv = og_ref[rows, cols].astype(F32)
            o_ref[rows, cols] = (
                o * lax.rsqrt(ms + EPS) * gn * (ogv * _sigmoid(ogv))).astype(o_ref.dtype)


def _hgrn(proj3, lb, gn, off_q, off_f, off_i, off_og, width, ts):
    b, s, _ = proj3.shape
    n_heads = width // HGRN_HEAD_DIM
    n_chunks = ts // HGRN_CHUNK
    dk = HGRN_HEAD_DIM
    spec = lambda off: pl.BlockSpec((None, ts, width), lambda i, t: (i, t, off // width))
    return pl.pallas_call(
        functools.partial(_hgrn_kernel, n_heads=n_heads, n_chunks=n_chunks),
        grid=(b, s // ts),
        in_specs=[
            spec(off_q), spec(off_f), spec(off_i), spec(off_og),
            pl.BlockSpec((1, width), lambda i, t: (0, 0)),
            pl.BlockSpec((1, dk), lambda i, t: (0, 0)),
        ],
        out_specs=pl.BlockSpec((None, ts, width), lambda i, t: (i, t, 0)),
        out_shape=jax.ShapeDtypeStruct((b, s, width), BF16),
        scratch_shapes=[
            pltpu.VMEM((n_heads, dk, dk), F32),
            pltpu.VMEM((ts, 2 * width), BF16),
            pltpu.VMEM((ts, width), F32),
            pltpu.VMEM((ts, width), F32),
            pltpu.VMEM((ts, width), BF16),
            pltpu.VMEM((ts, width), BF16),
            pltpu.VMEM((ts, width), BF16),
            pltpu.VMEM((ts, width), BF16),
            pltpu.VMEM((max(n_chunks, 8), width), F32),
            pltpu.VMEM((ts, width), F32),
            pltpu.VMEM((n_chunks, n_heads, dk, dk), F32),
            pltpu.VMEM((n_chunks * n_heads, HGRN_CHUNK, HGRN_CHUNK), BF16),
        ],
        compiler_params=_cparams(("arbitrary", "arbitrary")),
        name="hgrn2",
    )(proj3, proj3, proj3, proj3, lb, gn)


def _merge_kernel(oa_ref, oh_ref, g0_ref, g1_ref, x_ref, wa_ref, wh_ref, wo_ref, gm_ref,
                  wrh_ref, wrl_ref, br_ref, h_ref, hn_ref, route_ref, plan_ref, cnt_ref,
                  hn_s):
    tm, d = x_ref.shape
    half = d // 2
    halves = (slice(0, half), slice(half, d))
    step = pl.program_id(0)

    @pl.when(step == 0)
    def _():
        cnt_ref[...] = jnp.zeros_like(cnt_ref)
        hn_s[...] = jnp.zeros_like(hn_s)

    hn_prev = hn_s[...]
    hn_hi = hn_prev.astype(BF16)
    hn_lo = (hn_prev - hn_hi.astype(F32)).astype(BF16)

    ya = [jnp.dot(oa_ref[...], wa_ref[:, cs], preferred_element_type=F32) for cs in halves]
    yh = [jnp.dot(oh_ref[...], wh_ref[:, cs], preferred_element_type=F32) for cs in halves]

    logits = (jnp.dot(hn_hi, wrh_ref[...], preferred_element_type=F32)
              + jnp.dot(hn_hi, wrl_ref[...], preferred_element_type=F32)
              + jnp.dot(hn_lo, wrh_ref[...], preferred_element_type=F32)
              + br_ref[...])
    lane = lax.broadcasted_iota(jnp.int32, (tm, ROUTE_LANES), 1)
    neg = -jnp.inf

    def first_argmax(v):
        m = jnp.max(v, axis=-1, keepdims=True)
        idx = jnp.min(jnp.where(v == m, lane, ROUTE_LANES), axis=-1, keepdims=True)
        return m, idx

    is_group = lane < N_GROUPS
    gmax, gidx = first_argmax(jnp.where(is_group, logits, neg))
    p_sel = 1.0 / jnp.sum(jnp.where(is_group, jnp.exp(logits - gmax), 0.0), axis=-1, keepdims=True)
    eidx = lane - EXPERT_LANE0
    in_group = jnp.logical_and(
        jnp.logical_and(eidx >= 0, eidx < N_EXPERTS), (eidx // EXPERTS_PER_GROUP) == gidx)
    el = jnp.where(in_group, logits, neg)
    m1, i1 = first_argmax(el)
    m2, i2 = first_argmax(jnp.where(lane == i1, neg, el))
    t = jnp.exp(m2 - m1)
    w1 = p_sel / (1.0 + t)
    w2 = p_sel * t / (1.0 + t)

    sel1 = lane == i1
    sel2 = lane == i2
    onehot = jnp.logical_or(sel1, sel2).astype(BF16)
    ri = lax.broadcasted_iota(jnp.int32, (tm, tm), 0)
    ci = lax.broadcasted_iota(jnp.int32, (tm, tm), 1)
    before = (ci < ri).astype(BF16)

    h = x_ref[...]
    for k, cs in enumerate(halves):
        merged = (_sigmoid(g0_ref[:, cs].astype(F32)) * ya[k]
                  + _sigmoid(g1_ref[:, cs].astype(F32)) * yh[k]).astype(BF16)
        h = h + jnp.dot(merged, wo_ref[cs, :], preferred_element_type=F32)
        if k == 0:
            cum = jnp.dot(before, onehot, preferred_element_type=F32) + cnt_ref[...]

    r1 = jnp.sum(jnp.where(sel1, cum, 0.0), axis=-1, keepdims=True)
    r2 = jnp.sum(jnp.where(sel2, cum, 0.0), axis=-1, keepdims=True)
    routed = jnp.where(step > 0, 1.0, 0.0)
    cnt_ref[...] += routed * jnp.sum(onehot.astype(F32), axis=0, keepdims=True)

    e1 = (i1 - EXPERT_LANE0).astype(F32)
    e2 = (i2 - EXPERT_LANE0).astype(F32)
    out = jnp.zeros((tm, ROUTE_LANES), F32)
    for k, v in enumerate((e1, e2, w1, w2, r1, r2)):
        out = jnp.where(lane == k, v, out)
    route_ref[...] = out
    plan_ref[...] = out.T[:PLAN_ROWS, :].astype(jnp.int32)

    h_ref[...] = h
    ms = jnp.mean(h * h, axis=-1, keepdims=True)
    hn = h * lax.rsqrt(ms + EPS) * gm_ref[...]
    hn_ref[...] = _to_row_tiles(hn)
    hn_s[...] = hn


def _merge_route(o_a, o_h, proj, x2, wa, wh, wo, gm, wr_hi, wr_lo, br, off_g0, off_g1, tm):
    t, d = x2.shape
    nt = t // tm
    aw, hw = o_a.shape[1], o_h.shape[1]
    const = lambda shape: pl.BlockSpec(shape, lambda i: (0, 0), pipeline_mode=pl.Buffered(1))
    cur = lambda i: jnp.minimum(i, nt - 1)
    prv = lambda i: jnp.maximum(i - 1, 0)
    row = lambda w: pl.BlockSpec((tm, w), lambda i: (cur(i), 0))
    return pl.pallas_call(
        _merge_kernel,
        grid=(nt + 1,),
        in_specs=[
            row(aw), row(hw),
            pl.BlockSpec((tm, d), lambda i: (cur(i), off_g0 // d)),
            pl.BlockSpec((tm, d), lambda i: (cur(i), off_g1 // d)),
            row(d),
            const((aw, d)), const((hw, d)), const((d, d)), const((1, d)),
            const((d, ROUTE_LANES)), const((d, ROUTE_LANES)), const((1, ROUTE_LANES)),
        ],
        out_specs=[row(d), pl.BlockSpec((tm, d // LANES, LANES), lambda i: (cur(i), 0, 0)),
                   pl.BlockSpec((tm, ROUTE_LANES), lambda i: (prv(i), 0)),
                   pl.BlockSpec((None, PLAN_ROWS, tm), lambda i: (prv(i), 0, 0)),
                   pl.BlockSpec((1, ROUTE_LANES), lambda i: (0, 0))],
        out_shape=[
            jax.ShapeDtypeStruct((t, d), F32),
            jax.ShapeDtypeStruct((t, d // LANES, LANES), BF16),
            jax.ShapeDtypeStruct((t, ROUTE_LANES), F32),
            jax.ShapeDtypeStruct((nt, PLAN_ROWS, tm), jnp.int32),
            jax.ShapeDtypeStruct((1, ROUTE_LANES), F32),
        ],
        scratch_shapes=[pltpu.VMEM((tm, d), F32)],
        compiler_params=_cparams(("arbitrary",)),
        name="merge_route",
    )(o_a, o_h, proj, proj, x2, wa, wh, wo, gm, wr_hi, wr_lo, br)


def _dest_kernel(pst_ref, plan_ref, dest_ref):
    e = plan_ref[:, PLAN_E:PLAN_E + 2, :]
    start = jnp.zeros(e.shape, jnp.int32)
    for j in range(N_EXPERTS):
        start = jnp.where(e == j, pst_ref[j], start)
    dest_ref[...] = start + plan_ref[:, PLAN_R:PLAN_R + 2, :]


def _dest_rows(plan, pst, tiles):
    nt, _, tm = plan.shape
    grid_spec = pltpu.PrefetchScalarGridSpec(
        num_scalar_prefetch=1,
        grid=(nt // tiles,),
        in_specs=[pl.BlockSpec((tiles, PLAN_ROWS, tm), lambda i, *_: (i, 0, 0))],
        out_specs=pl.BlockSpec((tiles, 2, tm), lambda i, *_: (i, 0, 0)),
    )
    return pl.pallas_call(
        _dest_kernel,
        grid_spec=grid_spec,
        out_shape=jax.ShapeDtypeStruct((nt, 2, tm), jnp.int32),
        compiler_params=_cparams(("arbitrary",)),
        name="moe_dest",
    )(pst, plan)


ISSUE_UNROLL = 8


def _dispatch_kernel(fill_ref, nfill_ref, dest_ref, hn_ref, rows_ref, zbuf, sem):
    tm = hn_ref.shape[0]
    bm = zbuf.shape[0]

    @pl.when(pl.program_id(0) == 0)
    def _():
        zbuf[...] = jnp.zeros_like(zbuf)

        def fill(j, carry):
            start = pl.multiple_of(fill_ref[j], bm)
            pltpu.make_async_copy(zbuf, rows_ref.at[pl.ds(start, bm)], sem).start()
            return carry

        def drain(j, carry):
            pltpu.make_async_copy(zbuf, rows_ref.at[pl.ds(0, bm)], sem).wait()
            return carry

        lax.fori_loop(0, nfill_ref[0], fill, 0)
        lax.fori_loop(0, nfill_ref[0], drain, 0)

    sub, _, tmd = dest_ref.shape

    for j in range(sub):
        def issue(r, carry, j=j):
            for k in range(2):
                pltpu.make_async_copy(
                    hn_ref.at[j * tmd + r], rows_ref.at[dest_ref[j, k, r]], sem
                ).start(priority=k)
            return carry

        lax.fori_loop(0, tmd, issue, 0, unroll=ISSUE_UNROLL)
    for _ in range(2):
        pltpu.make_async_copy(hn_ref, rows_ref.at[pl.ds(0, tm)], sem).wait()


def _dispatch(hn, dest, fill, nfill, n_rows, sub, bm):
    t, c, l = hn.shape
    nt, _, tmd = dest.shape
    tm = sub * tmd
    grid_spec = pltpu.PrefetchScalarGridSpec(
        num_scalar_prefetch=2,
        grid=(nt // sub,),
        in_specs=[
            pl.BlockSpec((sub, 2, tmd), lambda i, *_: (i, 0, 0), memory_space=pltpu.SMEM),
            pl.BlockSpec((tm, c, l), lambda i, *_: (i, 0, 0)),
        ],
        out_specs=pl.BlockSpec(memory_space=pl.ANY),
        scratch_shapes=[pltpu.VMEM((bm, c, l), hn.dtype), pltpu.SemaphoreType.DMA],
    )
    return pl.pallas_call(
        _dispatch_kernel,
        grid_spec=grid_spec,
        out_shape=jax.ShapeDtypeStruct((n_rows, c, l), hn.dtype),
        compiler_params=_cparams(("arbitrary",)),
        name="moe_dispatch",
    )(fill, nfill, dest, hn)


def _expert_kernel(be_ref, meta_ref, slot_ref, nxt_ref, x_ref, w1_hbm, w3_hbm, w2_hbm, y_ref,
                   wf1, wf3, wf2, w1b, w3b, w2b, sems):
    i = pl.program_id(0)
    e = be_ref[i]
    new_expert = jnp.logical_or(i == 0, e != be_ref[jnp.maximum(i - 1, 0)])

    def weight_copies(expert, slot):
        return [pltpu.make_async_copy(src.at[expert], dst.at[slot], sems.at[slot])
                for src, dst in ((w1_hbm, wf1), (w3_hbm, wf3), (w2_hbm, wf2))]

    @pl.when(jnp.logical_and(i < meta_ref[0], new_expert))
    def _():
        slot = slot_ref[e]
        nxt = nxt_ref[e]

        @pl.when(i == 0)
        def _():
            for c in weight_copies(e, slot):
                c.start()

        @pl.when(nxt >= 0)
        def _():
            for c in weight_copies(nxt, 1 - slot):
                c.start(priority=1)

        for c in weight_copies(e, slot):
            c.wait()
        w1b[...] = wf1[slot].astype(BF16)
        w3b[...] = wf3[slot].astype(BF16)
        w2b[...] = wf2[slot].astype(BF16)

    @pl.when(i < meta_ref[0])
    def _():
        bm = x_ref.shape[0]
        parts = (slice(0, bm // 2), slice(bm // 2, bm))
        xs = [_from_row_tiles(x_ref[rs]) for rs in parts]
        ups = [(jnp.dot(x, w1b[...], preferred_element_type=F32),
                jnp.dot(x, w3b[...], preferred_element_type=F32)) for x in xs]
        for rs, (h1, h3) in zip(parts, ups):
            hb = (h1 * _sigmoid(h1) * h3).astype(BF16)
            y_ref[rs] = _to_row_tiles(jnp.dot(hb, w2b[...], preferred_element_type=F32))

    @pl.when(i >= meta_ref[0])
    def _():
        y_ref[...] = jnp.zeros_like(y_ref)


def _experts(rows, block_expert, meta, slot_e, nxt_e, w1, w3, w2, bm):
    n_rows, c, l = rows.shape
    d = c * l
    de = w1.shape[-1]
    n_blocks = n_rows // bm
    hbm = pl.BlockSpec(memory_space=pl.ANY)
    grid_spec = pltpu.PrefetchScalarGridSpec(
        num_scalar_prefetch=4,
        grid=(n_blocks,),
        in_specs=[
            pl.BlockSpec((bm, c, l), lambda i, be, meta, *_: (jnp.minimum(i, meta[0] - 1), 0, 0)),
            hbm, hbm, hbm,
        ],
        out_specs=pl.BlockSpec((bm, c, l), lambda i, *_: (i, 0, 0)),
        scratch_shapes=[
            pltpu.VMEM((2, d, de), w1.dtype), pltpu.VMEM((2, d, de), w3.dtype),
            pltpu.VMEM((2, de, d), w2.dtype),
            pltpu.VMEM((d, de), BF16), pltpu.VMEM((d, de), BF16), pltpu.VMEM((de, d), BF16),
            pltpu.SemaphoreType.DMA((2,)),
        ],
    )
    return pl.pallas_call(
        _expert_kernel,
        grid_spec=grid_spec,
        out_shape=jax.ShapeDtypeStruct((n_rows, c, l), BF16),
        compiler_params=_cparams(("arbitrary",)),
        name="moe_experts",
    )(block_expert, meta, slot_e, nxt_e, rows, w1, w3, w2)


def _combine_kernel(dcur_ref, dnext_ref, h_ref, route_ref, g_ref, y_ref, o_ref, ybuf, sems):
    i = pl.program_id(0)
    nt = pl.num_programs(0)
    tm, d = h_ref.shape
    slot = i % 2

    def gather(dest_ref, s):
        def issue(r, carry):
            for k in range(2):
                pltpu.make_async_copy(
                    y_ref.at[dest_ref[k, r]], ybuf.at[s, k, r], sems.at[s]).start(priority=k)
            return carry
        lax.fori_loop(0, tm, issue, 0, unroll=ISSUE_UNROLL)

    @pl.when(i == 0)
    def _():
        gather(dcur_ref, slot)

    @pl.when(i + 1 < nt)
    def _():
        gather(dnext_ref, 1 - slot)

    for k in range(2):
        pltpu.make_async_copy(y_ref.at[pl.ds(0, tm)], ybuf.at[slot, k], sems.at[slot]).wait()

    route = route_ref[...]
    w1 = route[:, 2:3]
    w2 = route[:, 3:4]
    ya = _from_row_tiles(ybuf[slot, 0]).astype(F32)
    yb = _from_row_tiles(ybuf[slot, 1]).astype(F32)
    h = h_ref[...] + w1 * ya + w2 * yb
    ms = jnp.mean(h * h, axis=-1, keepdims=True)
    o_ref[...] = h * lax.rsqrt(ms + EPS) * g_ref[...]


def _combine(h, route, dest, y_rows, g, tm):
    t, d = h.shape
    nt = t // tm
    dest_spec = lambda f: pl.BlockSpec((None, 2, tm), f, memory_space=pltpu.SMEM)
    return pl.pallas_call(
        _combine_kernel,
        grid=(nt,),
        in_specs=[
            dest_spec(lambda i: (i, 0, 0)),
            dest_spec(lambda i: (jnp.minimum(i + 1, nt - 1), 0, 0)),
            pl.BlockSpec((tm, d), lambda i: (i, 0)),
            pl.BlockSpec((tm, ROUTE_LANES), lambda i: (i, 0)),
            pl.BlockSpec((1, d), lambda i: (0, 0)),
            pl.BlockSpec(memory_space=pl.ANY),
        ],
        out_specs=pl.BlockSpec((tm, d), lambda i: (i, 0)),
        out_shape=jax.ShapeDtypeStruct((t, d), F32),
        scratch_shapes=[pltpu.VMEM((2, 2, tm, d // LANES, LANES), BF16),
                        pltpu.SemaphoreType.DMA((2,))],
        compiler_params=_cparams(("arbitrary",)),
        name="moe_combine",
    )(dest, dest, h, route, g, y_rows)


def kernel(x, w_in, attn_sinks, hgrn_lb_logits, hgrn_norm_g, w_br_attn, w_br_hgrn, w_out,
           mix_norm_g, moe_norm_g, w_router_group, b_router_group, w_router_expert,
           b_router_expert, w1, w3, w2, final_norm_g):
    b, s, d = x.shape
    t = b * s
    depth = w_in.shape[0]
    assert depth == 1
    attn_w = (d // 128) * ATTN_HEAD_DIM
    kv_w = attn_w // Q_PER_KV
    hg_w = (d // 256) * HGRN_HEAD_DIM
    assert kv_w == N_KV_HEADS * ATTN_HEAD_DIM

    src = {}
    off = 0
    for name, width in (("qa", attn_w), ("k", kv_w), ("v", kv_w), ("qh", hg_w), ("f", hg_w),
                        ("i", hg_w), ("og", hg_w), ("g0", d), ("g1", d)):
        src[name] = (off, width)
        off += width
    order = ("g0", "g1", "qa", "qh", "f", "i", "og", "k", "v")
    dst = {}
    off = 0
    for name in order:
        dst[name] = off
        off += src[name][1]
    n_in = off
    l = 0
    tn_in = 512
    assert all(src[n][0] % tn_in == 0 for n in order if n != "v") and (2 * kv_w) % tn_in == 0
    col_blocks = []
    for n in order[:-1]:
        width = src[n][1] if n != "k" else 2 * kv_w
        col_blocks += [src[n][0] // tn_in + c for c in range(width // tn_in)]
    col_blocks = jnp.asarray(col_blocks, jnp.int32)
    w_bf = w_in[l].astype(BF16)

    lb = jnp.cumsum(jax.nn.softmax(hgrn_lb_logits.astype(F32), axis=0), axis=0)[l].reshape(1, hg_w)

    x2 = x.reshape(t, d)
    tm_in = min(2048, t)
    proj = _in_proj(x2, mix_norm_g[l].reshape(1, d), w_bf, col_blocks, tm_in, tn_in)
    proj3 = proj.reshape(b, s, n_in)

    o_a = _attention(proj3, attn_sinks[l].astype(F32), dst["qa"], dst["k"], dst["v"], attn_w, kv_w)
    o_h = _hgrn(proj3, lb, hgrn_norm_g[l].reshape(1, HGRN_HEAD_DIM).astype(F32),
                dst["qh"], dst["f"], dst["i"], dst["og"], hg_w, min(256, s))

    w_r = jnp.zeros((d, ROUTE_LANES), F32)
    w_r = w_r.at[:, :N_GROUPS].set(w_router_group[l])
    w_r = w_r.at[:, EXPERT_LANE0:EXPERT_LANE0 + N_EXPERTS].set(w_router_expert[l])
    wr_hi = w_r.astype(BF16)
    wr_lo = (w_r - wr_hi.astype(F32)).astype(BF16)
    b_r = jnp.zeros((1, ROUTE_LANES), F32)
    b_r = b_r.at[0, :N_GROUPS].set(b_router_group[l])
    b_r = b_r.at[0, EXPERT_LANE0:EXPERT_LANE0 + N_EXPERTS].set(b_router_expert[l])

    tm = min(256, t)
    h, hn, route, plan, cnt = _merge_route(
        o_a.reshape(t, attn_w), o_h.reshape(t, hg_w), proj, x2,
        w_br_attn[l].astype(BF16), w_br_hgrn[l].astype(BF16), w_out[l].astype(BF16),
        moe_norm_g[l].reshape(1, d), wr_hi, wr_lo, b_r, dst["g0"], dst["g1"], tm)

    bm = MOE_ROWS
    counts = cnt[0, EXPERT_LANE0:EXPERT_LANE0 + N_EXPERTS].astype(jnp.int32)
    padded = ((counts + bm - 1) // bm) * bm
    pad_end = jnp.cumsum(padded)
    pad_start = (pad_end - padded).astype(jnp.int32)
    n_blocks = (2 * t) // bm + N_EXPERTS
    n_used = pad_end[-1] // bm
    blk_ids = jnp.minimum(jnp.arange(n_blocks, dtype=jnp.int32), n_used - 1)
    block_expert = jnp.minimum(
        jnp.sum((pad_end[None, :] <= (blk_ids * bm)[:, None]).astype(jnp.int32), axis=1),
        N_EXPERTS - 1)
    meta = jnp.stack([n_used, n_used]).astype(jnp.int32)

    blk_all = jnp.arange(n_blocks, dtype=jnp.int32)
    fill = jnp.concatenate([
        jnp.maximum(pad_end - bm, 0).astype(jnp.int32),
        jnp.minimum(n_used + blk_all, n_blocks - 1) * bm])
    nfill = (N_EXPERTS + n_blocks - n_used).astype(jnp.int32).reshape(1)

    eid = jnp.arange(N_EXPERTS, dtype=jnp.int32)
    nonempty = counts > 0
    slot_e = ((jnp.cumsum(nonempty.astype(jnp.int32)) - 1) % 2).astype(jnp.int32)
    later = jnp.logical_and(nonempty[None, :], eid[None, :] > eid[:, None])
    nxt_e = jnp.min(jnp.where(later, eid[None, :], N_EXPERTS), axis=1)
    nxt_e = jnp.where(nxt_e == N_EXPERTS, -1, nxt_e).astype(jnp.int32)

    nt = t // tm
    dest = _dest_rows(plan, pad_start, min(8, nt))
    rows = _dispatch(hn, dest, fill, nfill, n_blocks * bm, min(4, nt), bm)
    y_rows = _experts(rows, block_expert, meta, slot_e, nxt_e, w1[l], w3[l], w2[l], bm)
    out = _combine(h, route, dest, y_rows, final_norm_g.reshape(1, d), tm)
    return out.reshape(b, s, d)
```

```python
import functools

import jax
import jax.numpy as jnp
from jax import lax
from jax.experimental import pallas as pl
from jax.experimental.pallas import tpu as pltpu

F32 = jnp.float32
BF16 = jnp.bfloat16

EPS = 1e-6
ATTN_HEAD_DIM = 64
Q_PER_KV = 4
N_KV_HEADS = 4
ATTN_BLOCK = 128
HGRN_HEAD_DIM = 128
HGRN_CHUNK = 64
N_GROUPS = 4
EXPERTS_PER_GROUP = 8
N_EXPERTS = N_GROUPS * EXPERTS_PER_GROUP
LANES = 128
ROUTE_LANES = LANES
EXPERT_LANE0 = N_GROUPS
MOE_ROWS = 256
PLAN_ROWS = 8
PLAN_E, PLAN_R = 0, 4
VMEM_LIMIT = 56 * 1024 * 1024
VMEM_LIMIT_IN_PROJ = 60 * 1024 * 1024


def _sigmoid(x):
    return 0.5 * jnp.tanh(0.5 * x) + 0.5


def _cparams(sem, vmem=VMEM_LIMIT):
    return pltpu.CompilerParams(dimension_semantics=sem, vmem_limit_bytes=vmem)


def _to_row_tiles(v):
    m, d = v.shape
    return v.reshape(m, d // LANES, LANES).astype(BF16)


def _from_row_tiles(p):
    m, c, l = p.shape
    return p.reshape(m, c * l)


def _inproj_kernel(cols_ref, x_ref, g_ref, w_ref, o_ref, xn_ref):
    del cols_ref

    @pl.when(pl.program_id(1) == 0)
    def _():
        x = x_ref[...]
        ms = jnp.mean(x * x, axis=-1, keepdims=True)
        xn_ref[...] = (x * lax.rsqrt(ms + EPS) * g_ref[...]).astype(BF16)

    o_ref[...] = jnp.dot(xn_ref[...], w_ref[...].astype(BF16),
                         preferred_element_type=F32).astype(o_ref.dtype)


def _in_proj(x2, g, w, col_blocks, tm, tn):
    t, d = x2.shape
    n = w.shape[1]
    grid_spec = pltpu.PrefetchScalarGridSpec(
        num_scalar_prefetch=1,
        grid=(t // tm, n // tn),
        in_specs=[
            pl.BlockSpec((tm, d), lambda i, j, cols: (i, 0)),
            pl.BlockSpec((1, d), lambda i, j, cols: (0, 0)),
            pl.BlockSpec((d, tn), lambda i, j, cols: (0, cols[j])),
        ],
        out_specs=pl.BlockSpec((tm, tn), lambda i, j, cols: (i, j)),
        scratch_shapes=[pltpu.VMEM((tm, d), BF16)],
    )
    return pl.pallas_call(
        _inproj_kernel,
        grid_spec=grid_spec,
        out_shape=jax.ShapeDtypeStruct((t, n), BF16),
        compiler_params=_cparams(("arbitrary", "arbitrary"), VMEM_LIMIT_IN_PROJ),
        name="in_proj",
    )(col_blocks, x2, g, w)


def _attn_kernel(sink_ref, q_ref, kc_ref, kp_ref, vc_ref, vp_ref, o_ref):
    n = pl.program_id(1)
    blk, dh, g = ATTN_BLOCK, ATTN_HEAD_DIM, Q_PER_KV
    rows = g * blk
    qi = lax.broadcasted_iota(jnp.int32, (rows, blk), 0) % blk
    kj = lax.broadcasted_iota(jnp.int32, (rows, blk), 1)
    mask_cur = kj <= qi
    prev_bias = jnp.where(n > 0, 0.0, -jnp.inf)
    head_of_row = lax.broadcasted_iota(jnp.int32, (rows, 1), 0) // blk
    nt = (((1,), (1,)), ((), ()))
    scale = dh ** -0.5
    scores = []
    for h in range(N_KV_HEADS):
        kc = kc_ref[:, h * dh:(h + 1) * dh]
        kp = kp_ref[:, h * dh:(h + 1) * dh]
        q4 = jnp.concatenate(
            [q_ref[:, (h * g + j) * dh:(h * g + j + 1) * dh] for j in range(g)], axis=0)
        scores.append((lax.dot_general(q4, kc, nt, preferred_element_type=F32),
                       lax.dot_general(q4, kp, nt, preferred_element_type=F32)))
    for h in range(N_KV_HEADS):
        vc = vc_ref[:, h * dh:(h + 1) * dh]
        vp = vp_ref[:, h * dh:(h + 1) * dh]
        sc, sp = scores[h]
        s = jnp.where(mask_cur, sc, sp + prev_bias) * scale
        sink = jnp.zeros((rows, 1), F32)
        for j in range(g):
            sink = jnp.where(head_of_row == j, sink_ref[h * g + j], sink)
        m = jnp.maximum(jnp.max(s, axis=-1, keepdims=True), sink)
        p = jnp.exp(s - m)
        den = jnp.sum(p, axis=-1, keepdims=True) + jnp.exp(sink - m)
        p_cur = jnp.where(mask_cur, p, 0.0).astype(BF16)
        p_prev = jnp.where(mask_cur, 0.0, p).astype(BF16)
        o = (jnp.dot(p_cur, vc, preferred_element_type=F32)
             + jnp.dot(p_prev, vp, preferred_element_type=F32)) / den
        for j in range(g):
            o_ref[:, (h * g + j) * dh:(h * g + j + 1) * dh] = (
                o[j * blk:(j + 1) * blk, :].astype(o_ref.dtype))


def _attention(proj3, sinks, off_q, off_k, off_v, attn_w, kv_w):
    b, s, _ = proj3.shape
    blk = ATTN_BLOCK
    qb, kb, vb = off_q // attn_w, off_k // kv_w, off_v // kv_w
    prev = lambda n: jnp.maximum(n - 1, 0)
    return pl.pallas_call(
        _attn_kernel,
        grid=(b, s // blk),
        in_specs=[
            pl.BlockSpec(memory_space=pltpu.SMEM),
            pl.BlockSpec((None, blk, attn_w), lambda i, n: (i, n, qb)),
            pl.BlockSpec((None, blk, kv_w), lambda i, n: (i, n, kb)),
            pl.BlockSpec((None, blk, kv_w), lambda i, n: (i, prev(n), kb)),
            pl.BlockSpec((None, blk, kv_w), lambda i, n: (i, n, vb)),
            pl.BlockSpec((None, blk, kv_w), lambda i, n: (i, prev(n), vb)),
        ],
        out_specs=pl.BlockSpec((None, blk, attn_w), lambda i, n: (i, n, 0)),
        out_shape=jax.ShapeDtypeStruct((b, s, attn_w), BF16),
        compiler_params=_cparams(("arbitrary", "arbitrary")),
        name="swa_attention",
    )(sinks, proj3, proj3, proj3, proj3, proj3)


def _hgrn_kernel(q_ref, f_ref, i_ref, og_ref, lb_ref, gn_ref, o_ref,
                 st_ref, gh_s, f_s, b_s, qt_s, kt_s, qe_s, kd_s, dl_s, oi_s, u_s, a_s,
                 *, n_heads, n_chunks):
    c, dk = HGRN_CHUNK, HGRN_HEAD_DIM
    w = n_heads * dk

    @pl.when(pl.program_id(1) == 0)
    def _():
        st_ref[...] = jnp.zeros_like(st_ref)

    ti = lax.broadcasted_iota(jnp.int32, (c, c), 0)
    si = lax.broadcasted_iota(jnp.int32, (c, c), 1)
    causal = si <= ti
    tri = causal.astype(BF16)
    nt = (((1,), (1,)), ((), ()))
    tn = (((0,), (0,)), ((), ()))
    qscale = dk ** -0.5
    chunk_rows = [slice(ci * c, (ci + 1) * c) for ci in range(n_chunks)]
    head_cols = [slice(h * dk, (h + 1) * dk) for h in range(n_heads)]

    lb = lb_ref[...]
    c0 = 0.5 * (1.0 + lb)
    c1 = 0.5 * (1.0 - lb)
    tf = jnp.tanh(0.5 * f_ref[...].astype(F32))
    f_s[...] = c1 * (1.0 - tf)
    gl = jnp.log(c0 + c1 * tf)
    g_hi = gl.astype(BF16)
    gh_s[:, :w] = g_hi
    gh_s[:, w:] = (gl - g_hi.astype(F32)).astype(BF16)

    for rows in chunk_rows:
        bcat = jnp.dot(tri, gh_s[rows, :], preferred_element_type=F32)
        b_s[rows, :] = bcat[:, :w] + bcat[:, w:]

    for ci, rows in enumerate(chunk_rows):
        bc = b_s[rows, :]
        b_last = bc[c - 1:c, :]
        r = 0.5 * b_last
        qv = q_ref[rows, :].astype(F32)
        qt = (qv * (0.5 * qscale)) * (1.0 + jnp.tanh(0.5 * qv)) * jnp.exp(bc - r)
        kt = f_s[rows, :] * jnp.exp(r - bc)
        qt_s[rows, :] = qt.astype(BF16)
        kt_s[rows, :] = kt.astype(BF16)
        qe_s[rows, :] = (qt * jnp.exp(r)).astype(BF16)
        kd_s[rows, :] = (kt * jnp.exp(b_last - r)).astype(BF16)
        dl_s[ci:ci + 1, :] = jnp.exp(b_last)

    units = [(ci, rows, h, cols) for ci, rows in enumerate(chunk_rows)
             for h, cols in enumerate(head_cols)]
    for ci, rows, h, cols in units:
        a = lax.dot_general(qt_s[rows, cols], kt_s[rows, cols], nt, preferred_element_type=F32)
        a_s[ci * n_heads + h] = jnp.where(causal, a, 0.0).astype(BF16)
    for ci, rows, h, cols in units:
        u_s[ci, h] = lax.dot_general(
            i_ref[rows, cols], kd_s[rows, cols], tn, preferred_element_type=F32)
    for ci, rows, h, cols in units:
        oi_s[rows, cols] = jnp.dot(
            a_s[ci * n_heads + h], i_ref[rows, cols], preferred_element_type=F32)

    gn_half = 0.5 * gn_ref[...]
    for ci, rows in enumerate(chunk_rows):
        for h, cols in enumerate(head_cols):
            st = st_ref[h]
            o = oi_s[rows, cols] + lax.dot_general(
                qe_s[rows, cols], st.astype(BF16), nt, preferred_element_type=F32)
            st_ref[h] = st * dl_s[ci:ci + 1, cols] + u_s[ci, h]
            ms = jnp.mean(o * o, axis=-1, keepdims=True)
            ogv = og_ref[rows, cols].astype(F32)
            o_ref[rows, cols] = (
                (o * lax.rsqrt(ms + EPS) * gn_half)
                * (ogv * (1.0 + jnp.tanh(0.5 * ogv)))).astype(o_ref.dtype)


def _hgrn(proj3, lb, gn, off_q, off_f, off_i, off_og, width, ts):
    b, s, _ = proj3.shape
    n_heads = width // HGRN_HEAD_DIM
    n_chunks = ts // HGRN_CHUNK
    dk = HGRN_HEAD_DIM
    spec = lambda off: pl.BlockSpec((None, ts, width), lambda i, t: (i, t, off // width))
    return pl.pallas_call(
        functools.partial(_hgrn_kernel, n_heads=n_heads, n_chunks=n_chunks),
        grid=(b, s // ts),
        in_specs=[
            spec(off_q), spec(off_f), spec(off_i), spec(off_og),
            pl.BlockSpec((1, width), lambda i, t: (0, 0)),
            pl.BlockSpec((1, dk), lambda i, t: (0, 0)),
        ],
        out_specs=pl.BlockSpec((None, ts, width), lambda i, t: (i, t, 0)),
        out_shape=jax.ShapeDtypeStruct((b, s, width), BF16),
        scratch_shapes=[
            pltpu.VMEM((n_heads, dk, dk), F32),
            pltpu.VMEM((ts, 2 * width), BF16),
            pltpu.VMEM((ts, width), F32),
            pltpu.VMEM((ts, width), F32),
            pltpu.VMEM((ts, width), BF16),
            pltpu.VMEM((ts, width), BF16),
            pltpu.VMEM((ts, width), BF16),
            pltpu.VMEM((ts, width), BF16),
            pltpu.VMEM((max(n_chunks, 8), width), F32),
            pltpu.VMEM((ts, width), F32),
            pltpu.VMEM((n_chunks, n_heads, dk, dk), F32),
            pltpu.VMEM((n_chunks * n_heads, HGRN_CHUNK, HGRN_CHUNK), BF16),
        ],
        compiler_params=_cparams(("arbitrary", "arbitrary")),
        name="hgrn2",
    )(proj3, proj3, proj3, proj3, lb, gn)


def _merge_kernel(oa_ref, oh_ref, g0_ref, g1_ref, x_ref, wa_ref, wh_ref, wo_ref, gm_ref,
                  wrh_ref, wrl_ref, br_ref, h_ref, hn_ref, route_ref, plan_ref, cnt_ref,
                  hn_s):
    tm, d = x_ref.shape
    half = d // 2
    halves = (slice(0, half), slice(half, d))
    step = pl.program_id(0)

    @pl.when(step == 0)
    def _():
        cnt_ref[...] = jnp.zeros_like(cnt_ref)
        hn_s[...] = jnp.zeros_like(hn_s)

    hn_prev = hn_s[...]
    hn_hi = hn_prev.astype(BF16)
    hn_lo = (hn_prev - hn_hi.astype(F32)).astype(BF16)

    ya = [jnp.dot(oa_ref[...], wa_ref[:, cs], preferred_element_type=F32) for cs in halves]
    yh = [jnp.dot(oh_ref[...], wh_ref[:, cs], preferred_element_type=F32) for cs in halves]

    logits = (jnp.dot(hn_hi, wrh_ref[...], preferred_element_type=F32)
              + jnp.dot(hn_hi, wrl_ref[...], preferred_element_type=F32)
              + jnp.dot(hn_lo, wrh_ref[...], preferred_element_type=F32)
              + br_ref[...])
    lane = lax.broadcasted_iota(jnp.int32, (tm, ROUTE_LANES), 1)
    neg = -jnp.inf

    def first_argmax(v):
        m = jnp.max(v, axis=-1, keepdims=True)
        idx = jnp.min(jnp.where(v == m, lane, ROUTE_LANES), axis=-1, keepdims=True)
        return m, idx

    is_group = lane < N_GROUPS
    gmax, gidx = first_argmax(jnp.where(is_group, logits, neg))
    p_sel = 1.0 / jnp.sum(jnp.where(is_group, jnp.exp(logits - gmax), 0.0), axis=-1, keepdims=True)
    eidx = lane - EXPERT_LANE0
    in_group = jnp.logical_and(
        jnp.logical_and(eidx >= 0, eidx < N_EXPERTS), (eidx // EXPERTS_PER_GROUP) == gidx)
    el = jnp.where(in_group, logits, neg)
    m1, i1 = first_argmax(el)
    m2, i2 = first_argmax(jnp.where(lane == i1, neg, el))
    t = jnp.exp(m2 - m1)
    w1 = p_sel / (1.0 + t)
    w2 = p_sel * t / (1.0 + t)

    sel1 = lane == i1
    sel2 = lane == i2
    onehot = jnp.logical_or(sel1, sel2).astype(BF16)
    ri = lax.broadcasted_iota(jnp.int32, (tm, tm), 0)
    ci = lax.broadcasted_iota(jnp.int32, (tm, tm), 1)
    before = (ci < ri).astype(BF16)

    h = x_ref[...]
    for k, cs in enumerate(halves):
        merged = (_sigmoid(g0_ref[:, cs].astype(F32)) * ya[k]
                  + _sigmoid(g1_ref[:, cs].astype(F32)) * yh[k]).astype(BF16)
        h = h + jnp.dot(merged, wo_ref[cs, :], preferred_element_type=F32)
        if k == 0:
            cum = jnp.dot(before, onehot, preferred_element_type=F32) + cnt_ref[...]

    r1 = jnp.sum(jnp.where(sel1, cum, 0.0), axis=-1, keepdims=True)
    r2 = jnp.sum(jnp.where(sel2, cum, 0.0), axis=-1, keepdims=True)
    routed = jnp.where(step > 0, 1.0, 0.0)
    cnt_ref[...] += routed * jnp.sum(onehot.astype(F32), axis=0, keepdims=True)

    e1 = (i1 - EXPERT_LANE0).astype(F32)
    e2 = (i2 - EXPERT_LANE0).astype(F32)
    out = jnp.zeros((tm, ROUTE_LANES), F32)
    for k, v in enumerate((e1, e2, w1, w2, r1, r2)):
        out = jnp.where(lane == k, v, out)
    route_ref[...] = out
    plan_ref[...] = out.T[:PLAN_ROWS, :].astype(jnp.int32)

    h_ref[...] = h
    ms = jnp.mean(h * h, axis=-1, keepdims=True)
    hn = h * lax.rsqrt(ms + EPS) * gm_ref[...]
    hn_ref[...] = _to_row_tiles(hn)
    hn_s[...] = hn


def _merge_route(o_a, o_h, proj, x2, wa, wh, wo, gm, wr_hi, wr_lo, br, off_g0, off_g1, tm):
    t, d = x2.shape
    nt = t // tm
    aw, hw = o_a.shape[1], o_h.shape[1]
    const = lambda shape: pl.BlockSpec(shape, lambda i: (0, 0), pipeline_mode=pl.Buffered(1))
    cur = lambda i: jnp.minimum(i, nt - 1)
    prv = lambda i: jnp.maximum(i - 1, 0)
    row = lambda w: pl.BlockSpec((tm, w), lambda i: (cur(i), 0))
    return pl.pallas_call(
        _merge_kernel,
        grid=(nt + 1,),
        in_specs=[
            row(aw), row(hw),
            pl.BlockSpec((tm, d), lambda i: (cur(i), off_g0 // d)),
            pl.BlockSpec((tm, d), lambda i: (cur(i), off_g1 // d)),
            row(d),
            const((aw, d)), const((hw, d)), const((d, d)), const((1, d)),
            const((d, ROUTE_LANES)), const((d, ROUTE_LANES)), const((1, ROUTE_LANES)),
        ],
        out_specs=[row(d), pl.BlockSpec((tm, d // LANES, LANES), lambda i: (cur(i), 0, 0)),
                   pl.BlockSpec((tm, ROUTE_LANES), lambda i: (prv(i), 0)),
                   pl.BlockSpec((None, PLAN_ROWS, tm), lambda i: (prv(i), 0, 0)),
                   pl.BlockSpec((1, ROUTE_LANES), lambda i: (0, 0))],
        out_shape=[
            jax.ShapeDtypeStruct((t, d), F32),
            jax.ShapeDtypeStruct((t, d // LANES, LANES), BF16),
            jax.ShapeDtypeStruct((t, ROUTE_LANES), F32),
            jax.ShapeDtypeStruct((nt, PLAN_ROWS, tm), jnp.int32),
            jax.ShapeDtypeStruct((1, ROUTE_LANES), F32),
        ],
        scratch_shapes=[pltpu.VMEM((tm, d), F32)],
        compiler_params=_cparams(("arbitrary",)),
        name="merge_route",
    )(o_a, o_h, proj, proj, x2, wa, wh, wo, gm, wr_hi, wr_lo, br)


def _dest_kernel(pst_ref, plan_ref, dest_ref):
    e = plan_ref[:, PLAN_E:PLAN_E + 2, :]
    start = jnp.zeros(e.shape, jnp.int32)
    for j in range(N_EXPERTS):
        start = jnp.where(e == j, pst_ref[j], start)
    dest_ref[...] = start + plan_ref[:, PLAN_R:PLAN_R + 2, :]


def _dest_rows(plan, pst, tiles):
    nt, _, tm = plan.shape
    grid_spec = pltpu.PrefetchScalarGridSpec(
        num_scalar_prefetch=1,
        grid=(nt // tiles,),
        in_specs=[pl.BlockSpec((tiles, PLAN_ROWS, tm), lambda i, *_: (i, 0, 0))],
        out_specs=pl.BlockSpec((tiles, 2, tm), lambda i, *_: (i, 0, 0)),
    )
    return pl.pallas_call(
        _dest_kernel,
        grid_spec=grid_spec,
        out_shape=jax.ShapeDtypeStruct((nt, 2, tm), jnp.int32),
        compiler_params=_cparams(("arbitrary",)),
        name="moe_dest",
    )(pst, plan)


ISSUE_UNROLL = 8


def _dispatch_kernel(fill_ref, nfill_ref, dest_ref, hn_ref, rows_ref, zbuf, sem):
    tm = hn_ref.shape[0]
    bm = zbuf.shape[0]

    @pl.when(pl.program_id(0) == 0)
    def _():
        zbuf[...] = jnp.zeros_like(zbuf)

        def fill(j, carry):
            start = pl.multiple_of(fill_ref[j], bm)
            pltpu.make_async_copy(zbuf, rows_ref.at[pl.ds(start, bm)], sem).start()
            return carry

        def drain(j, carry):
            pltpu.make_async_copy(zbuf, rows_ref.at[pl.ds(0, bm)], sem).wait()
            return carry

        lax.fori_loop(0, nfill_ref[0], fill, 0)
        lax.fori_loop(0, nfill_ref[0], drain, 0)

    sub, _, tmd = dest_ref.shape

    for j in range(sub):
        def issue(r, carry, j=j):
            for k in range(2):
                pltpu.make_async_copy(
                    hn_ref.at[j * tmd + r], rows_ref.at[dest_ref[j, k, r]], sem
                ).start(priority=k)
            return carry

        lax.fori_loop(0, tmd, issue, 0, unroll=ISSUE_UNROLL)
    for _ in range(2):
        pltpu.make_async_copy(hn_ref, rows_ref.at[pl.ds(0, tm)], sem).wait()


def _dispatch(hn, dest, fill, nfill, n_rows, sub, bm):
    t, c, l = hn.shape
    nt, _, tmd = dest.shape
    tm = sub * tmd
    grid_spec = pltpu.PrefetchScalarGridSpec(
        num_scalar_prefetch=2,
        grid=(nt // sub,),
        in_specs=[
            pl.BlockSpec((sub, 2, tmd), lambda i, *_: (i, 0, 0), memory_space=pltpu.SMEM),
            pl.BlockSpec((tm, c, l), lambda i, *_: (i, 0, 0)),
        ],
        out_specs=pl.BlockSpec(memory_space=pl.ANY),
        scratch_shapes=[pltpu.VMEM((bm, c, l), hn.dtype), pltpu.SemaphoreType.DMA],
    )
    return pl.pallas_call(
        _dispatch_kernel,
        grid_spec=grid_spec,
        out_shape=jax.ShapeDtypeStruct((n_rows, c, l), hn.dtype),
        compiler_params=_cparams(("arbitrary",)),
        name="moe_dispatch",
    )(fill, nfill, dest, hn)


def _expert_kernel(be_ref, meta_ref, slot_ref, nxt_ref, x_ref, w1_hbm, w3_hbm, w2_hbm, y_ref,
                   wf1, wf3, wf2, w1b, w3b, w2b, sems):
    i = pl.program_id(0)
    e = be_ref[i]
    new_expert = jnp.logical_or(i == 0, e != be_ref[jnp.maximum(i - 1, 0)])

    def weight_copies(expert, slot):
        return [pltpu.make_async_copy(src.at[expert], dst.at[slot], sems.at[slot])
                for src, dst in ((w1_hbm, wf1), (w3_hbm, wf3), (w2_hbm, wf2))]

    @pl.when(jnp.logical_and(i < meta_ref[0], new_expert))
    def _():
        slot = slot_ref[e]
        nxt = nxt_ref[e]

        @pl.when(i == 0)
        def _():
            for c in weight_copies(e, slot):
                c.start()

        @pl.when(nxt >= 0)
        def _():
            for c in weight_copies(nxt, 1 - slot):
                c.start(priority=1)

        for c in weight_copies(e, slot):
            c.wait()
        w1b[...] = wf1[slot].astype(BF16)
        w3b[...] = wf3[slot].astype(BF16)
        w2b[...] = wf2[slot].astype(BF16)

    @pl.when(i < meta_ref[0])
    def _():
        bm = x_ref.shape[0]
        parts = (slice(0, bm // 2), slice(bm // 2, bm))
        xs = [_from_row_tiles(x_ref[rs]) for rs in parts]
        ups = [(jnp.dot(x, w1b[...], preferred_element_type=F32),
                jnp.dot(x, w3b[...], preferred_element_type=F32)) for x in xs]
        for rs, (h1, h3) in zip(parts, ups):
            hb = (h1 * _sigmoid(h1) * h3).astype(BF16)
            y_ref[rs] = _to_row_tiles(jnp.dot(hb, w2b[...], preferred_element_type=F32))

    @pl.when(i >= meta_ref[0])
    def _():
        y_ref[...] = jnp.zeros_like(y_ref)


def _experts(rows, block_expert, meta, slot_e, nxt_e, w1, w3, w2, bm):
    n_rows, c, l = rows.shape
    d = c * l
    de = w1.shape[-1]
    n_blocks = n_rows // bm
    hbm = pl.BlockSpec(memory_space=pl.ANY)
    grid_spec = pltpu.PrefetchScalarGridSpec(
        num_scalar_prefetch=4,
        grid=(n_blocks,),
        in_specs=[
            pl.BlockSpec((bm, c, l), lambda i, be, meta, *_: (jnp.minimum(i, meta[0] - 1), 0, 0)),
            hbm, hbm, hbm,
        ],
        out_specs=pl.BlockSpec((bm, c, l), lambda i, *_: (i, 0, 0)),
        scratch_shapes=[
            pltpu.VMEM((2, d, de), w1.dtype), pltpu.VMEM((2, d, de), w3.dtype),
            pltpu.VMEM((2, de, d), w2.dtype),
            pltpu.VMEM((d, de), BF16), pltpu.VMEM((d, de), BF16), pltpu.VMEM((de, d), BF16),
            pltpu.SemaphoreType.DMA((2,)),
        ],
    )
    return pl.pallas_call(
        _expert_kernel,
        grid_spec=grid_spec,
        out_shape=jax.ShapeDtypeStruct((n_rows, c, l), BF16),
        compiler_params=_cparams(("arbitrary",)),
        name="moe_experts",
    )(block_expert, meta, slot_e, nxt_e, rows, w1, w3, w2)


def _combine_kernel(dcur_ref, dnext_ref, h_ref, route_ref, g_ref, y_ref, o_ref, ybuf, sems):
    i = pl.program_id(0)
    nt = pl.num_programs(0)
    tm, d = h_ref.shape
    slot = i % 2

    def gather(dest_ref, s):
        def issue(r, carry):
            for k in range(2):
                pltpu.make_async_copy(
                    y_ref.at[dest_ref[k, r]], ybuf.at[s, k, r], sems.at[s]).start(priority=k)
            return carry
        lax.fori_loop(0, tm, issue, 0, unroll=ISSUE_UNROLL)

    @pl.when(i == 0)
    def _():
        gather(dcur_ref, slot)

    @pl.when(i + 1 < nt)
    def _():
        gather(dnext_ref, 1 - slot)

    for k in range(2):
        pltpu.make_async_copy(y_ref.at[pl.ds(0, tm)], ybuf.at[slot, k], sems.at[slot]).wait()

    route = route_ref[...]
    w1 = route[:, 2:3]
    w2 = route[:, 3:4]
    ya = _from_row_tiles(ybuf[slot, 0]).astype(F32)
    yb = _from_row_tiles(ybuf[slot, 1]).astype(F32)
    h = h_ref[...] + w1 * ya + w2 * yb
    ms = jnp.mean(h * h, axis=-1, keepdims=True)
    o_ref[...] = h * lax.rsqrt(ms + EPS) * g_ref[...]


def _combine(h, route, dest, y_rows, g, tm):
    t, d = h.shape
    nt = t // tm
    dest_spec = lambda f: pl.BlockSpec((None, 2, tm), f, memory_space=pltpu.SMEM)
    return pl.pallas_call(
        _combine_kernel,
        grid=(nt,),
        in_specs=[
            dest_spec(lambda i: (i, 0, 0)),
            dest_spec(lambda i: (jnp.minimum(i + 1, nt - 1), 0, 0)),
            pl.BlockSpec((tm, d), lambda i: (i, 0)),
            pl.BlockSpec((tm, ROUTE_LANES), lambda i: (i, 0)),
            pl.BlockSpec((1, d), lambda i: (0, 0)),
            pl.BlockSpec(memory_space=pl.ANY),
        ],
        out_specs=pl.BlockSpec((tm, d), lambda i: (i, 0)),
        out_shape=jax.ShapeDtypeStruct((t, d), F32),
        scratch_shapes=[pltpu.VMEM((2, 2, tm, d // LANES, LANES), BF16),
                        pltpu.SemaphoreType.DMA((2,))],
        compiler_params=_cparams(("arbitrary",)),
        name="moe_combine",
    )(dest, dest, h, route, g, y_rows)


def kernel(x, w_in, attn_sinks, hgrn_lb_logits, hgrn_norm_g, w_br_attn, w_br_hgrn, w_out,
           mix_norm_g, moe_norm_g, w_router_group, b_router_group, w_router_expert,
           b_router_expert, w1, w3, w2, final_norm_g):
    b, s, d = x.shape
    t = b * s
    depth = w_in.shape[0]
    assert depth == 1
    attn_w = (d // 128) * ATTN_HEAD_DIM
    kv_w = attn_w // Q_PER_KV
    hg_w = (d // 256) * HGRN_HEAD_DIM
    assert kv_w == N_KV_HEADS * ATTN_HEAD_DIM

    src = {}
    off = 0
    for name, width in (("qa", attn_w), ("k", kv_w), ("v", kv_w), ("qh", hg_w), ("f", hg_w),
                        ("i", hg_w), ("og", hg_w), ("g0", d), ("g1", d)):
        src[name] = (off, width)
        off += width
    order = ("g0", "g1", "qa", "qh", "f", "i", "og", "k", "v")
    dst = {}
    off = 0
    for name in order:
        dst[name] = off
        off += src[name][1]
    n_in = off
    l = 0
    tn_in = 512
    assert all(src[n][0] % tn_in == 0 for n in order if n != "v") and (2 * kv_w) % tn_in == 0
    col_blocks = []
    for n in order[:-1]:
        width = src[n][1] if n != "k" else 2 * kv_w
        col_blocks += [src[n][0] // tn_in + c for c in range(width // tn_in)]
    col_blocks = jnp.asarray(col_blocks, jnp.int32)
    w_bf = w_in[l]

    lb = jnp.cumsum(jax.nn.softmax(hgrn_lb_logits.astype(F32), axis=0), axis=0)[l].reshape(1, hg_w)

    x2 = x.reshape(t, d)
    tm_in = min(2048, t)
    proj = _in_proj(x2, mix_norm_g[l].reshape(1, d), w_bf, col_blocks, tm_in, tn_in)
    proj3 = proj.reshape(b, s, n_in)

    o_a = _attention(proj3, attn_sinks[l].astype(F32), dst["qa"], dst["k"], dst["v"], attn_w, kv_w)
    o_h = _hgrn(proj3, lb, hgrn_norm_g[l].reshape(1, HGRN_HEAD_DIM).astype(F32),
                dst["qh"], dst["f"], dst["i"], dst["og"], hg_w, min(256, s))

    w_r = jnp.zeros((d, ROUTE_LANES), F32)
    w_r = w_r.at[:, :N_GROUPS].set(w_router_group[l])
    w_r = w_r.at[:, EXPERT_LANE0:EXPERT_LANE0 + N_EXPERTS].set(w_router_expert[l])
    wr_hi = w_r.astype(BF16)
    wr_lo = (w_r - wr_hi.astype(F32)).astype(BF16)
    b_r = jnp.zeros((1, ROUTE_LANES), F32)
    b_r = b_r.at[0, :N_GROUPS].set(b_router_group[l])
    b_r = b_r.at[0, EXPERT_LANE0:EXPERT_LANE0 + N_EXPERTS].set(b_router_expert[l])

    tm = min(256, t)
    h, hn, route, plan, cnt = _merge_route(
        o_a.reshape(t, attn_w), o_h.reshape(t, hg_w), proj, x2,
        w_br_attn[l].astype(BF16), w_br_hgrn[l].astype(BF16), w_out[l].astype(BF16),
        moe_norm_g[l].reshape(1, d), wr_hi, wr_lo, b_r, dst["g0"], dst["g1"], tm)

    bm = MOE_ROWS
    counts = cnt[0, EXPERT_LANE0:EXPERT_LANE0 + N_EXPERTS].astype(jnp.int32)
    padded = ((counts + bm - 1) // bm) * bm
    pad_end = jnp.cumsum(padded)
    pad_start = (pad_end - padded).astype(jnp.int32)
    n_blocks = (2 * t) // bm + N_EXPERTS
    n_used = pad_end[-1] // bm
    blk_ids = jnp.minimum(jnp.arange(n_blocks, dtype=jnp.int32), n_used - 1)
    block_expert = jnp.minimum(
        jnp.sum((pad_end[None, :] <= (blk_ids * bm)[:, None]).astype(jnp.int32), axis=1),
        N_EXPERTS - 1)
    meta = jnp.stack([n_used, n_used]).astype(jnp.int32)

    blk_all = jnp.arange(n_blocks, dtype=jnp.int32)
    fill = jnp.concatenate([
        jnp.maximum(pad_end - bm, 0).astype(jnp.int32),
        jnp.minimum(n_used + blk_all, n_blocks - 1) * bm])
    nfill = (N_EXPERTS + n_blocks - n_used).astype(jnp.int32).reshape(1)

    eid = jnp.arange(N_EXPERTS, dtype=jnp.int32)
    nonempty = counts > 0
    slot_e = ((jnp.cumsum(nonempty.astype(jnp.int32)) - 1) % 2).astype(jnp.int32)
    later = jnp.logical_and(nonempty[None, :], eid[None, :] > eid[:, None])
    nxt_e = jnp.min(jnp.where(later, eid[None, :], N_EXPERTS), axis=1)
    nxt_e = jnp.where(nxt_e == N_EXPERTS, -1, nxt_e).astype(jnp.int32)

    nt = t // tm
    dest = _dest_rows(plan, pad_start, min(8, nt))
    rows = _dispatch(hn, dest, fill, nfill, n_blocks * bm, min(4, nt), bm)
    y_rows = _experts(rows, block_expert, meta, slot_e, nxt_e, w1[l], w3[l], w2[l], bm)
    out = _combine(h, route, dest, y_rows, final_norm_g.reshape(1, d), tm)
    return out.reshape(b, s, d)
```

```python
import functools

import jax
import jax.numpy as jnp
from jax import lax
from jax.experimental import pallas as pl
from jax.experimental.pallas import tpu as pltpu

F32 = jnp.float32
BF16 = jnp.bfloat16

EPS = 1e-6
ATTN_HEAD_DIM = 64
Q_PER_KV = 4
N_KV_HEADS = 4
ATTN_BLOCK = 128
HGRN_HEAD_DIM = 128
HGRN_CHUNK = 64
N_GROUPS = 4
EXPERTS_PER_GROUP = 8
N_EXPERTS = N_GROUPS * EXPERTS_PER_GROUP
LANES = 128
ROUTE_LANES = LANES
EXPERT_LANE0 = N_GROUPS
MOE_ROWS = 256
PLAN_ROWS = 8
PLAN_E, PLAN_R = 0, 4
VMEM_LIMIT = 56 * 1024 * 1024
VMEM_LIMIT_IN_PROJ = 60 * 1024 * 1024


def _sigmoid(x):
    return 0.5 * jnp.tanh(0.5 * x) + 0.5


def _cparams(sem, vmem=VMEM_LIMIT):
    return pltpu.CompilerParams(dimension_semantics=sem, vmem_limit_bytes=vmem)


def _to_row_tiles(v):
    m, d = v.shape
    return v.reshape(m, d // LANES, LANES).astype(BF16)


def _from_row_tiles(p):
    m, c, l = p.shape
    return p.reshape(m, c * l)


def _inproj_kernel(cols_ref, x_ref, g_ref, w_ref, o_ref, xn_ref):
    del cols_ref

    @pl.when(pl.program_id(1) == 0)
    def _():
        x = x_ref[...]
        ms = jnp.mean(x * x, axis=-1, keepdims=True)
        xn_ref[...] = (x * lax.rsqrt(ms + EPS) * g_ref[...]).astype(BF16)

    o_ref[...] = jnp.dot(xn_ref[...], w_ref[...].astype(BF16),
                         preferred_element_type=F32).astype(o_ref.dtype)


def _in_proj(x2, g, w, col_blocks, tm, tn):
    t, d = x2.shape
    n = w.shape[1]
    grid_spec = pltpu.PrefetchScalarGridSpec(
        num_scalar_prefetch=1,
        grid=(t // tm, n // tn),
        in_specs=[
            pl.BlockSpec((tm, d), lambda i, j, cols: (i, 0)),
            pl.BlockSpec((1, d), lambda i, j, cols: (0, 0)),
            pl.BlockSpec((d, tn), lambda i, j, cols: (0, cols[j])),
        ],
        out_specs=pl.BlockSpec((tm, tn), lambda i, j, cols: (i, j)),
        scratch_shapes=[pltpu.VMEM((tm, d), BF16)],
    )
    return pl.pallas_call(
        _inproj_kernel,
        grid_spec=grid_spec,
        out_shape=jax.ShapeDtypeStruct((t, n), BF16),
        compiler_params=_cparams(("arbitrary", "arbitrary"), VMEM_LIMIT_IN_PROJ),
        name="in_proj",
    )(col_blocks, x2, g, w)


def _attn_scores(q_ref, rows, k_cur, k_prev):
    dh, g = ATTN_HEAD_DIM, Q_PER_KV
    nt = (((1,), (1,)), ((), ()))
    scores = []
    for h in range(N_KV_HEADS):
        q4 = jnp.concatenate(
            [q_ref[rows, (h * g + j) * dh:(h * g + j + 1) * dh] for j in range(g)], axis=0)
        scores.append(
            (lax.dot_general(q4, k_cur[:, h * dh:(h + 1) * dh], nt, preferred_element_type=F32),
             lax.dot_general(q4, k_prev[:, h * dh:(h + 1) * dh], nt, preferred_element_type=F32)))
    return scores


def _attn_finish(scores, v_cur, v_prev, sink_ref, prev_bias, o_ref, rows):
    blk, dh, g = ATTN_BLOCK, ATTN_HEAD_DIM, Q_PER_KV
    n_rows = g * blk
    qi = lax.broadcasted_iota(jnp.int32, (n_rows, blk), 0) % blk
    kj = lax.broadcasted_iota(jnp.int32, (n_rows, blk), 1)
    mask_cur = kj <= qi
    head_of_row = lax.broadcasted_iota(jnp.int32, (n_rows, 1), 0) // blk
    scale = dh ** -0.5
    for h in range(N_KV_HEADS):
        sc, sp = scores[h]
        s = jnp.where(mask_cur, sc, sp + prev_bias) * scale
        sink = jnp.zeros((n_rows, 1), F32)
        for j in range(g):
            sink = jnp.where(head_of_row == j, sink_ref[h * g + j], sink)
        m = jnp.maximum(jnp.max(s, axis=-1, keepdims=True), sink)
        p = jnp.exp(s - m)
        den = jnp.sum(p, axis=-1, keepdims=True) + jnp.exp(sink - m)
        p_cur = jnp.where(mask_cur, p, 0.0).astype(BF16)
        p_prev = jnp.where(mask_cur, 0.0, p).astype(BF16)
        o = (jnp.dot(p_cur, v_cur[:, h * dh:(h + 1) * dh], preferred_element_type=F32)
             + jnp.dot(p_prev, v_prev[:, h * dh:(h + 1) * dh], preferred_element_type=F32)) / den
        for j in range(g):
            o_ref[rows, (h * g + j) * dh:(h * g + j + 1) * dh] = (
                o[j * blk:(j + 1) * blk, :].astype(o_ref.dtype))


def _hgrn_kernel(q_ref, f_ref, i_ref, og_ref, lb_ref, gn_ref, o_ref,
                 st_ref, gh_s, f_s, b_s, qt_s, kt_s, qe_s, kd_s, dl_s, oi_s, u_s, a_s,
                 *, n_heads, n_chunks):
    c, dk = HGRN_CHUNK, HGRN_HEAD_DIM
    w = n_heads * dk

    @pl.when(pl.program_id(1) == 0)
    def _():
        st_ref[...] = jnp.zeros_like(st_ref)

    ti = lax.broadcasted_iota(jnp.int32, (c, c), 0)
    si = lax.broadcasted_iota(jnp.int32, (c, c), 1)
    causal = si <= ti
    tri = causal.astype(BF16)
    nt = (((1,), (1,)), ((), ()))
    tn = (((0,), (0,)), ((), ()))
    qscale = dk ** -0.5
    chunk_rows = [slice(ci * c, (ci + 1) * c) for ci in range(n_chunks)]
    head_cols = [slice(h * dk, (h + 1) * dk) for h in range(n_heads)]

    lb = lb_ref[...]
    c0 = 0.5 * (1.0 + lb)
    c1 = 0.5 * (1.0 - lb)
    tf = jnp.tanh(0.5 * f_ref[...].astype(F32))
    f_s[...] = c1 * (1.0 - tf)
    gl = jnp.log(c0 + c1 * tf)
    g_hi = gl.astype(BF16)
    gh_s[:, :w] = g_hi
    gh_s[:, w:] = (gl - g_hi.astype(F32)).astype(BF16)

    for rows in chunk_rows:
        bcat = jnp.dot(tri, gh_s[rows, :], preferred_element_type=F32)
        b_s[rows, :] = bcat[:, :w] + bcat[:, w:]

    for ci, rows in enumerate(chunk_rows):
        bc = b_s[rows, :]
        b_last = bc[c - 1:c, :]
        r = 0.5 * b_last
        qv = q_ref[rows, :].astype(F32)
        qt = (qv * (0.5 * qscale)) * (1.0 + jnp.tanh(0.5 * qv)) * jnp.exp(bc - r)
        kt = f_s[rows, :] * jnp.exp(r - bc)
        qt_s[rows, :] = qt.astype(BF16)
        kt_s[rows, :] = kt.astype(BF16)
        qe_s[rows, :] = (qt * jnp.exp(r)).astype(BF16)
        kd_s[rows, :] = (kt * jnp.exp(b_last - r)).astype(BF16)
        dl_s[ci:ci + 1, :] = jnp.exp(b_last)

    units = [(ci, rows, h, cols) for ci, rows in enumerate(chunk_rows)
             for h, cols in enumerate(head_cols)]
    for ci, rows, h, cols in units:
        a = lax.dot_general(qt_s[rows, cols], kt_s[rows, cols], nt, preferred_element_type=F32)
        a_s[ci * n_heads + h] = jnp.where(causal, a, 0.0).astype(BF16)
    for ci, rows, h, cols in units:
        u_s[ci, h] = lax.dot_general(
            i_ref[rows, cols], kd_s[rows, cols], tn, preferred_element_type=F32)
    for ci, rows, h, cols in units:
        oi_s[rows, cols] = jnp.dot(
            a_s[ci * n_heads + h], i_ref[rows, cols], preferred_element_type=F32)

    gn_half = 0.5 * gn_ref[...]
    for ci, rows in enumerate(chunk_rows):
        for h, cols in enumerate(head_cols):
            st = st_ref[h]
            o = oi_s[rows, cols] + lax.dot_general(
                qe_s[rows, cols], st.astype(BF16), nt, preferred_element_type=F32)
            st_ref[h] = st * dl_s[ci:ci + 1, cols] + u_s[ci, h]
            ms = jnp.mean(o * o, axis=-1, keepdims=True)
            ogv = og_ref[rows, cols].astype(F32)
            o_ref[rows, cols] = (
                (o * lax.rsqrt(ms + EPS) * gn_half)
                * (ogv * (1.0 + jnp.tanh(0.5 * ogv)))).astype(o_ref.dtype)


def _hgrn(proj3, lb, gn, off_q, off_f, off_i, off_og, width, ts):
    b, s, _ = proj3.shape
    n_heads = width // HGRN_HEAD_DIM
    n_chunks = ts // HGRN_CHUNK
    dk = HGRN_HEAD_DIM
    spec = lambda off: pl.BlockSpec((None, ts, width), lambda i, t: (i, t, off // width))
    return pl.pallas_call(
        functools.partial(_hgrn_kernel, n_heads=n_heads, n_chunks=n_chunks),
        grid=(b, s // ts),
        in_specs=[
            spec(off_q), spec(off_f), spec(off_i), spec(off_og),
            pl.BlockSpec((1, width), lambda i, t: (0, 0)),
            pl.BlockSpec((1, dk), lambda i, t: (0, 0)),
        ],
        out_specs=pl.BlockSpec((None, ts, width), lambda i, t: (i, t, 0)),
        out_shape=jax.ShapeDtypeStruct((b, s, width), BF16),
        scratch_shapes=[
            pltpu.VMEM((n_heads, dk, dk), F32),
            pltpu.VMEM((ts, 2 * width), BF16),
            pltpu.VMEM((ts, width), F32),
            pltpu.VMEM((ts, width), F32),
            pltpu.VMEM((ts, width), BF16),
            pltpu.VMEM((ts, width), BF16),
            pltpu.VMEM((ts, width), BF16),
            pltpu.VMEM((ts, width), BF16),
            pltpu.VMEM((max(n_chunks, 8), width), F32),
            pltpu.VMEM((ts, width), F32),
            pltpu.VMEM((n_chunks, n_heads, dk, dk), F32),
            pltpu.VMEM((n_chunks * n_heads, HGRN_CHUNK, HGRN_CHUNK), BF16),
        ],
        compiler_params=_cparams(("arbitrary", "arbitrary")),
        name="hgrn2",
    )(proj3, proj3, proj3, proj3, lb, gn)


def _merge_kernel(sink_ref, q_ref, kc_ref, kp_ref, vc_ref, vp_ref,
                  oh_ref, g0_ref, g1_ref, x_ref, wa_ref, wh_ref, wo_ref, gm_ref,
                  wrh_ref, wrl_ref, br_ref, h_ref, hn_ref, route_ref, plan_ref, cnt_ref,
                  oa_s, hn_s, *, tiles_per_seq):
    tm, d = x_ref.shape
    half = d // 2
    halves = (slice(0, half), slice(half, d))
    step = pl.program_id(0)
    blk = ATTN_BLOCK
    assert tm == 2 * blk

    @pl.when(step == 0)
    def _():
        cnt_ref[...] = jnp.zeros_like(cnt_ref)
        oa_s[...] = jnp.zeros_like(oa_s)
        hn_s[...] = jnp.zeros_like(hn_s)

    q_blocks = (slice(0, blk), slice(blk, tm))
    k_blocks = [kc_ref[rs, :] for rs in q_blocks]
    v_blocks = [vc_ref[rs, :] for rs in q_blocks]
    scores = [_attn_scores(q_ref, q_blocks[0], k_blocks[0], kp_ref[...]),
              _attn_scores(q_ref, q_blocks[1], k_blocks[1], k_blocks[0])]
    attn_tile = jnp.minimum(step, pl.num_programs(0) - 3)
    prev_bias = jnp.where(attn_tile % tiles_per_seq == 0, -jnp.inf, 0.0)

    hn_prev = hn_s[...]
    hn_hi = hn_prev.astype(BF16)
    hn_lo = (hn_prev - hn_hi.astype(F32)).astype(BF16)

    oa = oa_s[...]
    ya = [jnp.dot(oa, wa_ref[:, cs], preferred_element_type=F32) for cs in halves]
    yh = [jnp.dot(oh_ref[...], wh_ref[:, cs], preferred_element_type=F32) for cs in halves]

    logits = (jnp.dot(hn_hi, wrh_ref[...], preferred_element_type=F32)
              + jnp.dot(hn_hi, wrl_ref[...], preferred_element_type=F32)
              + jnp.dot(hn_lo, wrh_ref[...], preferred_element_type=F32)
              + br_ref[...])

    _attn_finish(scores[0], v_blocks[0], vp_ref[...], sink_ref, prev_bias, oa_s, q_blocks[0])
    _attn_finish(scores[1], v_blocks[1], v_blocks[0], sink_ref, 0.0, oa_s, q_blocks[1])

    lane = lax.broadcasted_iota(jnp.int32, (tm, ROUTE_LANES), 1)
    neg = -jnp.inf

    def first_argmax(v):
        m = jnp.max(v, axis=-1, keepdims=True)
        idx = jnp.min(jnp.where(v == m, lane, ROUTE_LANES), axis=-1, keepdims=True)
        return m, idx

    is_group = lane < N_GROUPS
    gmax, gidx = first_argmax(jnp.where(is_group, logits, neg))
    p_sel = 1.0 / jnp.sum(jnp.where(is_group, jnp.exp(logits - gmax), 0.0), axis=-1, keepdims=True)
    eidx = lane - EXPERT_LANE0
    in_group = jnp.logical_and(
        jnp.logical_and(eidx >= 0, eidx < N_EXPERTS), (eidx // EXPERTS_PER_GROUP) == gidx)
    el = jnp.where(in_group, logits, neg)
    m1, i1 = first_argmax(el)
    m2, i2 = first_argmax(jnp.where(lane == i1, neg, el))
    t = jnp.exp(m2 - m1)
    w1 = p_sel / (1.0 + t)
    w2 = p_sel * t / (1.0 + t)

    sel1 = lane == i1
    sel2 = lane == i2
    onehot = jnp.logical_or(sel1, sel2).astype(BF16)
    ri = lax.broadcasted_iota(jnp.int32, (tm, tm), 0)
    ci = lax.broadcasted_iota(jnp.int32, (tm, tm), 1)
    before = (ci < ri).astype(BF16)

    h = x_ref[...]
    for k, cs in enumerate(halves):
        merged = (_sigmoid(g0_ref[:, cs].astype(F32)) * ya[k]
                  + _sigmoid(g1_ref[:, cs].astype(F32)) * yh[k]).astype(BF16)
        h = h + jnp.dot(merged, wo_ref[cs, :], preferred_element_type=F32)
        if k == 0:
            cum = jnp.dot(before, onehot, preferred_element_type=F32) + cnt_ref[...]


    r1 = jnp.sum(jnp.where(sel1, cum, 0.0), axis=-1, keepdims=True)
    r2 = jnp.sum(jnp.where(sel2, cum, 0.0), axis=-1, keepdims=True)
    routed = jnp.where(step > 1, 1.0, 0.0)
    cnt_ref[...] += routed * jnp.sum(onehot.astype(F32), axis=0, keepdims=True)

    e1 = (i1 - EXPERT_LANE0).astype(F32)
    e2 = (i2 - EXPERT_LANE0).astype(F32)
    out = jnp.zeros((tm, ROUTE_LANES), F32)
    for k, v in enumerate((e1, e2, w1, w2, r1, r2)):
        out = jnp.where(lane == k, v, out)
    route_ref[...] = out
    plan_ref[...] = out.T[:PLAN_ROWS, :].astype(jnp.int32)

    h_ref[...] = h
    ms = jnp.mean(h * h, axis=-1, keepdims=True)
    hn = h * lax.rsqrt(ms + EPS) * gm_ref[...]
    hn_ref[...] = _to_row_tiles(hn)
    hn_s[...] = hn


def _attn_merge_route(sinks, o_h, proj, x2, wa, wh, wo, gm, wr_hi, wr_lo, br,
                      off, attn_w, kv_w, seq, tm):
    t, d = x2.shape
    nt = t // tm
    hw = o_h.shape[1]
    blk = ATTN_BLOCK
    const = lambda shape: pl.BlockSpec(shape, lambda i: (0, 0), pipeline_mode=pl.Buffered(1))
    att = lambda i: jnp.minimum(i, nt - 1)
    cur = lambda i: jnp.clip(i - 1, 0, nt - 1)
    prv = lambda i: jnp.clip(i - 2, 0, nt - 1)
    row = lambda w: pl.BlockSpec((tm, w), lambda i: (cur(i), 0))
    kv_col = {"k": off["k"] // kv_w, "v": off["v"] // kv_w}
    kv_cur = lambda n: pl.BlockSpec((tm, kv_w), lambda i: (att(i), kv_col[n]))
    kv_prev = lambda n: pl.BlockSpec(
        (blk, kv_w), lambda i: (jnp.maximum(att(i) * (tm // blk) - 1, 0), kv_col[n]))
    return pl.pallas_call(
        functools.partial(_merge_kernel, tiles_per_seq=seq // tm),
        grid=(nt + 2,),
        in_specs=[
            pl.BlockSpec(memory_space=pltpu.SMEM),
            pl.BlockSpec((tm, attn_w), lambda i: (att(i), off["qa"] // attn_w)),
            kv_cur("k"), kv_prev("k"), kv_cur("v"), kv_prev("v"),
            row(hw),
            pl.BlockSpec((tm, d), lambda i: (cur(i), off["g0"] // d)),
            pl.BlockSpec((tm, d), lambda i: (cur(i), off["g1"] // d)),
            row(d),
            const((attn_w, d)), const((hw, d)), const((d, d)), const((1, d)),
            const((d, ROUTE_LANES)), const((d, ROUTE_LANES)), const((1, ROUTE_LANES)),
        ],
        out_specs=[row(d), pl.BlockSpec((tm, d // LANES, LANES), lambda i: (cur(i), 0, 0)),
                   pl.BlockSpec((tm, ROUTE_LANES), lambda i: (prv(i), 0)),
                   pl.BlockSpec((None, PLAN_ROWS, tm), lambda i: (prv(i), 0, 0)),
                   pl.BlockSpec((1, ROUTE_LANES), lambda i: (0, 0))],
        out_shape=[
            jax.ShapeDtypeStruct((t, d), F32),
            jax.ShapeDtypeStruct((t, d // LANES, LANES), BF16),
            jax.ShapeDtypeStruct((t, ROUTE_LANES), F32),
            jax.ShapeDtypeStruct((nt, PLAN_ROWS, tm), jnp.int32),
            jax.ShapeDtypeStruct((1, ROUTE_LANES), F32),
        ],
        scratch_shapes=[pltpu.VMEM((tm, attn_w), BF16), pltpu.VMEM((tm, d), F32)],
        compiler_params=_cparams(("arbitrary",)),
        name="attn_merge_route",
    )(sinks, proj, proj, proj, proj, proj, o_h, proj, proj, x2, wa, wh, wo, gm, wr_hi, wr_lo, br)


def _dest_kernel(pst_ref, plan_ref, dest_ref):
    e = plan_ref[:, PLAN_E:PLAN_E + 2, :]
    start = jnp.zeros(e.shape, jnp.int32)
    for j in range(N_EXPERTS):
        start = jnp.where(e == j, pst_ref[j], start)
    dest_ref[...] = start + plan_ref[:, PLAN_R:PLAN_R + 2, :]


def _dest_rows(plan, pst, tiles):
    nt, _, tm = plan.shape
    grid_spec = pltpu.PrefetchScalarGridSpec(
        num_scalar_prefetch=1,
        grid=(nt // tiles,),
        in_specs=[pl.BlockSpec((tiles, PLAN_ROWS, tm), lambda i, *_: (i, 0, 0))],
        out_specs=pl.BlockSpec((tiles, 2, tm), lambda i, *_: (i, 0, 0)),
    )
    return pl.pallas_call(
        _dest_kernel,
        grid_spec=grid_spec,
        out_shape=jax.ShapeDtypeStruct((nt, 2, tm), jnp.int32),
        compiler_params=_cparams(("arbitrary",)),
        name="moe_dest",
    )(pst, plan)


ISSUE_UNROLL = 8


def _dispatch_kernel(fill_ref, nfill_ref, dest_ref, hn_ref, rows_ref, zbuf, sem):
    tm = hn_ref.shape[0]
    bm = zbuf.shape[0]

    @pl.when(pl.program_id(0) == 0)
    def _():
        zbuf[...] = jnp.zeros_like(zbuf)

        def fill(j, carry):
            start = pl.multiple_of(fill_ref[j], bm)
            pltpu.make_async_copy(zbuf, rows_ref.at[pl.ds(start, bm)], sem).start()
            return carry

        def drain(j, carry):
            pltpu.make_async_copy(zbuf, rows_ref.at[pl.ds(0, bm)], sem).wait()
            return carry

        lax.fori_loop(0, nfill_ref[0], fill, 0)
        lax.fori_loop(0, nfill_ref[0], drain, 0)

    sub, _, tmd = dest_ref.shape

    for j in range(sub):
        def issue(r, carry, j=j):
            for k in range(2):
                pltpu.make_async_copy(
                    hn_ref.at[j * tmd + r], rows_ref.at[dest_ref[j, k, r]], sem
                ).start(priority=k)
            return carry

        lax.fori_loop(0, tmd, issue, 0, unroll=ISSUE_UNROLL)
    for _ in range(2):
        pltpu.make_async_copy(hn_ref, rows_ref.at[pl.ds(0, tm)], sem).wait()


def _dispatch(hn, dest, fill, nfill, n_rows, sub, bm):
    t, c, l = hn.shape
    nt, _, tmd = dest.shape
    tm = sub * tmd
    grid_spec = pltpu.PrefetchScalarGridSpec(
        num_scalar_prefetch=2,
        grid=(nt // sub,),
        in_specs=[
            pl.BlockSpec((sub, 2, tmd), lambda i, *_: (i, 0, 0), memory_space=pltpu.SMEM),
            pl.BlockSpec((tm, c, l), lambda i, *_: (i, 0, 0)),
        ],
        out_specs=pl.BlockSpec(memory_space=pl.ANY),
        scratch_shapes=[pltpu.VMEM((bm, c, l), hn.dtype), pltpu.SemaphoreType.DMA],
    )
    return pl.pallas_call(
        _dispatch_kernel,
        grid_spec=grid_spec,
        out_shape=jax.ShapeDtypeStruct((n_rows, c, l), hn.dtype),
        compiler_params=_cparams(("arbitrary",)),
        name="moe_dispatch",
    )(fill, nfill, dest, hn)


def _expert_kernel(be_ref, meta_ref, slot_ref, nxt_ref, x_ref, w1_hbm, w3_hbm, w2_hbm, y_ref,
                   wf1, wf3, wf2, w1b, w3b, w2b, sems):
    i = pl.program_id(0)
    e = be_ref[i]
    new_expert = jnp.logical_or(i == 0, e != be_ref[jnp.maximum(i - 1, 0)])

    def weight_copies(expert, slot):
        return [pltpu.make_async_copy(src.at[expert], dst.at[slot], sems.at[slot])
                for src, dst in ((w1_hbm, wf1), (w3_hbm, wf3), (w2_hbm, wf2))]

    @pl.when(jnp.logical_and(i < meta_ref[0], new_expert))
    def _():
        slot = slot_ref[e]
        nxt = nxt_ref[e]

        @pl.when(i == 0)
        def _():
            for c in weight_copies(e, slot):
                c.start()

        @pl.when(nxt >= 0)
        def _():
            for c in weight_copies(nxt, 1 - slot):
                c.start(priority=1)

        for c in weight_copies(e, slot):
            c.wait()
        w1b[...] = wf1[slot].astype(BF16)
        w3b[...] = wf3[slot].astype(BF16)
        w2b[...] = wf2[slot].astype(BF16)

    @pl.when(i < meta_ref[0])
    def _():
        bm = x_ref.shape[0]
        parts = (slice(0, bm // 2), slice(bm // 2, bm))
        xs = [_from_row_tiles(x_ref[rs]) for rs in parts]
        ups = [(jnp.dot(x, w1b[...], preferred_element_type=F32),
                jnp.dot(x, w3b[...], preferred_element_type=F32)) for x in xs]
        for rs, (h1, h3) in zip(parts, ups):
            hb = (h1 * _sigmoid(h1) * h3).astype(BF16)
            y_ref[rs] = _to_row_tiles(jnp.dot(hb, w2b[...], preferred_element_type=F32))

    @pl.when(i >= meta_ref[0])
    def _():
        y_ref[...] = jnp.zeros_like(y_ref)


def _experts(rows, block_expert, meta, slot_e, nxt_e, w1, w3, w2, bm):
    n_rows, c, l = rows.shape
    d = c * l
    de = w1.shape[-1]
    n_blocks = n_rows // bm
    hbm = pl.BlockSpec(memory_space=pl.ANY)
    grid_spec = pltpu.PrefetchScalarGridSpec(
        num_scalar_prefetch=4,
        grid=(n_blocks,),
        in_specs=[
            pl.BlockSpec((bm, c, l), lambda i, be, meta, *_: (jnp.minimum(i, meta[0] - 1), 0, 0)),
            hbm, hbm, hbm,
        ],
        out_specs=pl.BlockSpec((bm, c, l), lambda i, *_: (i, 0, 0)),
        scratch_shapes=[
            pltpu.VMEM((2, d, de), w1.dtype), pltpu.VMEM((2, d, de), w3.dtype),
            pltpu.VMEM((2, de, d), w2.dtype),
            pltpu.VMEM((d, de), BF16), pltpu.VMEM((d, de), BF16), pltpu.VMEM((de, d), BF16),
            pltpu.SemaphoreType.DMA((2,)),
        ],
    )
    return pl.pallas_call(
        _expert_kernel,
        grid_spec=grid_spec,
        out_shape=jax.ShapeDtypeStruct((n_rows, c, l), BF16),
        compiler_params=_cparams(("arbitrary",)),
        name="moe_experts",
    )(block_expert, meta, slot_e, nxt_e, rows, w1, w3, w2)


def _combine_kernel(dcur_ref, dnext_ref, h_ref, route_ref, g_ref, y_ref, o_ref, ybuf, sems):
    i = pl.program_id(0)
    nt = pl.num_programs(0)
    tm, d = h_ref.shape
    slot = i % 2

    def gather(dest_ref, s):
        def issue(r, carry):
            for k in range(2):
                pltpu.make_async_copy(
                    y_ref.at[dest_ref[k, r]], ybuf.at[s, k, r], sems.at[s]).start(priority=k)
            return carry
        lax.fori_loop(0, tm, issue, 0, unroll=ISSUE_UNROLL)

    @pl.when(i == 0)
    def _():
        gather(dcur_ref, slot)

    @pl.when(i + 1 < nt)
    def _():
        gather(dnext_ref, 1 - slot)

    for k in range(2):
        pltpu.make_async_copy(y_ref.at[pl.ds(0, tm)], ybuf.at[slot, k], sems.at[slot]).wait()

    route = route_ref[...]
    w1 = route[:, 2:3]
    w2 = route[:, 3:4]
    ya = _from_row_tiles(ybuf[slot, 0]).astype(F32)
    yb = _from_row_tiles(ybuf[slot, 1]).astype(F32)
    h = h_ref[...] + w1 * ya + w2 * yb
    ms = jnp.mean(h * h, axis=-1, keepdims=True)
    o_ref[...] = h * lax.rsqrt(ms + EPS) * g_ref[...]


def _combine(h, route, dest, y_rows, g, tm):
    t, d = h.shape
    nt = t // tm
    dest_spec = lambda f: pl.BlockSpec((None, 2, tm), f, memory_space=pltpu.SMEM)
    return pl.pallas_call(
        _combine_kernel,
        grid=(nt,),
        in_specs=[
            dest_spec(lambda i: (i, 0, 0)),
            dest_spec(lambda i: (jnp.minimum(i + 1, nt - 1), 0, 0)),
            pl.BlockSpec((tm, d), lambda i: (i, 0)),
            pl.BlockSpec((tm, ROUTE_LANES), lambda i: (i, 0)),
            pl.BlockSpec((1, d), lambda i: (0, 0)),
            pl.BlockSpec(memory_space=pl.ANY),
        ],
        out_specs=pl.BlockSpec((tm, d), lambda i: (i, 0)),
        out_shape=jax.ShapeDtypeStruct((t, d), F32),
        scratch_shapes=[pltpu.VMEM((2, 2, tm, d // LANES, LANES), BF16),
                        pltpu.SemaphoreType.DMA((2,))],
        compiler_params=_cparams(("arbitrary",)),
        name="moe_combine",
    )(dest, dest, h, route, g, y_rows)


def kernel(x, w_in, attn_sinks, hgrn_lb_logits, hgrn_norm_g, w_br_attn, w_br_hgrn, w_out,
           mix_norm_g, moe_norm_g, w_router_group, b_router_group, w_router_expert,
           b_router_expert, w1, w3, w2, final_norm_g):
    b, s, d = x.shape
    t = b * s
    depth = w_in.shape[0]
    assert depth == 1
    attn_w = (d // 128) * ATTN_HEAD_DIM
    kv_w = attn_w // Q_PER_KV
    hg_w = (d // 256) * HGRN_HEAD_DIM
    assert kv_w == N_KV_HEADS * ATTN_HEAD_DIM

    src = {}
    off = 0
    for name, width in (("qa", attn_w), ("k", kv_w), ("v", kv_w), ("qh", hg_w), ("f", hg_w),
                        ("i", hg_w), ("og", hg_w), ("g0", d), ("g1", d)):
        src[name] = (off, width)
        off += width
    order = ("g0", "g1", "qa", "qh", "f", "i", "og", "k", "v")
    dst = {}
    off = 0
    for name in order:
        dst[name] = off
        off += src[name][1]
    n_in = off
    l = 0
    tn_in = 512
    assert all(src[n][0] % tn_in == 0 for n in order if n != "v") and (2 * kv_w) % tn_in == 0
    col_blocks = []
    for n in order[:-1]:
        width = src[n][1] if n != "k" else 2 * kv_w
        col_blocks += [src[n][0] // tn_in + c for c in range(width // tn_in)]
    col_blocks = jnp.asarray(col_blocks, jnp.int32)
    w_bf = w_in[l]

    lb = jnp.cumsum(jax.nn.softmax(hgrn_lb_logits.astype(F32), axis=0), axis=0)[l].reshape(1, hg_w)

    x2 = x.reshape(t, d)
    tm_in = min(2048, t)
    proj = _in_proj(x2, mix_norm_g[l].reshape(1, d), w_bf, col_blocks, tm_in, tn_in)
    proj3 = proj.reshape(b, s, n_in)

    o_h = _hgrn(proj3, lb, hgrn_norm_g[l].reshape(1, HGRN_HEAD_DIM).astype(F32),
                dst["qh"], dst["f"], dst["i"], dst["og"], hg_w, min(256, s))

    w_r = jnp.zeros((d, ROUTE_LANES), F32)
    w_r = w_r.at[:, :N_GROUPS].set(w_router_group[l])
    w_r = w_r.at[:, EXPERT_LANE0:EXPERT_LANE0 + N_EXPERTS].set(w_router_expert[l])
    wr_hi = w_r.astype(BF16)
    wr_lo = (w_r - wr_hi.astype(F32)).astype(BF16)
    b_r = jnp.zeros((1, ROUTE_LANES), F32)
    b_r = b_r.at[0, :N_GROUPS].set(b_router_group[l])
    b_r = b_r.at[0, EXPERT_LANE0:EXPERT_LANE0 + N_EXPERTS].set(b_router_expert[l])

    tm = 2 * ATTN_BLOCK
    h, hn, route, plan, cnt = _attn_merge_route(
        attn_sinks[l].astype(F32), o_h.reshape(t, hg_w), proj, x2,
        w_br_attn[l].astype(BF16), w_br_hgrn[l].astype(BF16), w_out[l].astype(BF16),
        moe_norm_g[l].reshape(1, d), wr_hi, wr_lo, b_r, dst, attn_w, kv_w, s, tm)

    bm = MOE_ROWS
    counts = cnt[0, EXPERT_LANE0:EXPERT_LANE0 + N_EXPERTS].astype(jnp.int32)
    padded = ((counts + bm - 1) // bm) * bm
    pad_end = jnp.cumsum(padded)
    pad_start = (pad_end - padded).astype(jnp.int32)
    n_blocks = (2 * t) // bm + N_EXPERTS
    n_used = pad_end[-1] // bm
    blk_ids = jnp.minimum(jnp.arange(n_blocks, dtype=jnp.int32), n_used - 1)
    block_expert = jnp.minimum(
        jnp.sum((pad_end[None, :] <= (blk_ids * bm)[:, None]).astype(jnp.int32), axis=1),
        N_EXPERTS - 1)
    meta = jnp.stack([n_used, n_used]).astype(jnp.int32)

    blk_all = jnp.arange(n_blocks, dtype=jnp.int32)
    fill = jnp.concatenate([
        jnp.maximum(pad_end - bm, 0).astype(jnp.int32),
        jnp.minimum(n_used + blk_all, n_blocks - 1) * bm])
    nfill = (N_EXPERTS + n_blocks - n_used).astype(jnp.int32).reshape(1)

    eid = jnp.arange(N_EXPERTS, dtype=jnp.int32)
    nonempty = counts > 0
    slot_e = ((jnp.cumsum(nonempty.astype(jnp.int32)) - 1) % 2).astype(jnp.int32)
    later = jnp.logical_and(nonempty[None, :], eid[None, :] > eid[:, None])
    nxt_e = jnp.min(jnp.where(later, eid[None, :], N_EXPERTS), axis=1)
    nxt_e = jnp.where(nxt_e == N_EXPERTS, -1, nxt_e).astype(jnp.int32)

    nt = t // tm
    dest = _dest_rows(plan, pad_start, min(8, nt))
    rows = _dispatch(hn, dest, fill, nfill, n_blocks * bm, min(4, nt), bm)
    y_rows = _experts(rows, block_expert, meta, slot_e, nxt_e, w1[l], w3[l], w2[l], bm)
    out = _combine(h, route, dest, y_rows, final_norm_g.reshape(1, d), tm)
    return out.reshape(b, s, d)
```

```python
import functools

import jax
import jax.numpy as jnp
from jax import lax
from jax.experimental import pallas as pl
from jax.experimental.pallas import tpu as pltpu

F32 = jnp.float32
BF16 = jnp.bfloat16

EPS = 1e-6
ATTN_HEAD_DIM = 64
Q_PER_KV = 4
N_KV_HEADS = 4
ATTN_BLOCK = 128
HGRN_HEAD_DIM = 128
HGRN_CHUNK = 64
N_GROUPS = 4
EXPERTS_PER_GROUP = 8
N_EXPERTS = N_GROUPS * EXPERTS_PER_GROUP
LANES = 128
ROUTE_LANES = LANES
EXPERT_LANE0 = N_GROUPS
MOE_ROWS = 256
PLAN_ROWS = 8
PLAN_E, PLAN_R = 0, 4
VMEM_LIMIT = 56 * 1024 * 1024
VMEM_LIMIT_IN_PROJ = 60 * 1024 * 1024


def _sigmoid(x):
    return 0.5 * jnp.tanh(0.5 * x) + 0.5


def _cparams(sem, vmem=VMEM_LIMIT):
    return pltpu.CompilerParams(dimension_semantics=sem, vmem_limit_bytes=vmem)


def _to_row_tiles(v):
    m, d = v.shape
    return v.reshape(m, d // LANES, LANES).astype(BF16)


def _from_row_tiles(p):
    m, c, l = p.shape
    return p.reshape(m, c * l)


def _inproj_kernel(cols_ref, x_ref, g_ref, w_ref, o_ref, xn_ref):
    del cols_ref

    @pl.when(pl.program_id(1) == 0)
    def _():
        x = x_ref[...]
        ms = jnp.mean(x * x, axis=-1, keepdims=True)
        xn_ref[...] = (x * lax.rsqrt(ms + EPS) * g_ref[...]).astype(BF16)

    o_ref[...] = jnp.dot(xn_ref[...], w_ref[...].astype(BF16),
                         preferred_element_type=F32).astype(o_ref.dtype)


def _in_proj(x2, g, w, col_blocks, tm, tn):
    t, d = x2.shape
    n = w.shape[1]
    grid_spec = pltpu.PrefetchScalarGridSpec(
        num_scalar_prefetch=1,
        grid=(t // tm, n // tn),
        in_specs=[
            pl.BlockSpec((tm, d), lambda i, j, cols: (i, 0)),
            pl.BlockSpec((1, d), lambda i, j, cols: (0, 0)),
            pl.BlockSpec((d, tn), lambda i, j, cols: (0, cols[j])),
        ],
        out_specs=pl.BlockSpec((tm, tn), lambda i, j, cols: (i, j)),
        scratch_shapes=[pltpu.VMEM((tm, d), BF16)],
    )
    return pl.pallas_call(
        _inproj_kernel,
        grid_spec=grid_spec,
        out_shape=jax.ShapeDtypeStruct((t, n), BF16),
        compiler_params=_cparams(("arbitrary", "arbitrary"), VMEM_LIMIT_IN_PROJ),
        name="in_proj",
    )(col_blocks, x2, g, w)


def _attn_scores(q_ref, rows, k_cur, k_prev):
    dh, g = ATTN_HEAD_DIM, Q_PER_KV
    nt = (((1,), (1,)), ((), ()))
    scores = []
    for h in range(N_KV_HEADS):
        q4 = jnp.concatenate(
            [q_ref[rows, (h * g + j) * dh:(h * g + j + 1) * dh] for j in range(g)], axis=0)
        scores.append(
            (lax.dot_general(q4, k_cur[:, h * dh:(h + 1) * dh], nt, preferred_element_type=F32),
             lax.dot_general(q4, k_prev[:, h * dh:(h + 1) * dh], nt, preferred_element_type=F32)))
    return scores


def _attn_finish(scores, v_cur, v_prev, sink_ref, prev_bias, o_ref, rows):
    blk, dh, g = ATTN_BLOCK, ATTN_HEAD_DIM, Q_PER_KV
    n_rows = g * blk
    qi = lax.broadcasted_iota(jnp.int32, (n_rows, blk), 0) % blk
    kj = lax.broadcasted_iota(jnp.int32, (n_rows, blk), 1)
    mask_cur = kj <= qi
    head_of_row = lax.broadcasted_iota(jnp.int32, (n_rows, 1), 0) // blk
    scale = dh ** -0.5
    for h in range(N_KV_HEADS):
        sc, sp = scores[h]
        s = jnp.where(mask_cur, sc, sp + prev_bias) * scale
        sink = jnp.zeros((n_rows, 1), F32)
        for j in range(g):
            sink = jnp.where(head_of_row == j, sink_ref[h * g + j], sink)
        m = jnp.maximum(jnp.max(s, axis=-1, keepdims=True), sink)
        p = jnp.exp(s - m)
        den = jnp.sum(p, axis=-1, keepdims=True) + jnp.exp(sink - m)
        p_cur = jnp.where(mask_cur, p, 0.0).astype(BF16)
        p_prev = jnp.where(mask_cur, 0.0, p).astype(BF16)
        o = (jnp.dot(p_cur, v_cur[:, h * dh:(h + 1) * dh], preferred_element_type=F32)
             + jnp.dot(p_prev, v_prev[:, h * dh:(h + 1) * dh], preferred_element_type=F32)) / den
        for j in range(g):
            o_ref[rows, (h * g + j) * dh:(h * g + j + 1) * dh] = (
                o[j * blk:(j + 1) * blk, :].astype(o_ref.dtype))


def _hgrn_kernel(q_ref, f_ref, i_ref, og_ref, lb_ref, gn_ref, o_ref,
                 st_ref, gh_s, f_s, b_s, qt_s, kt_s, qe_s, kd_s, dl_s, oi_s, u_s, a_s,
                 *, n_heads, n_chunks):
    c, dk = HGRN_CHUNK, HGRN_HEAD_DIM
    w = n_heads * dk

    @pl.when(pl.program_id(1) == 0)
    def _():
        st_ref[...] = jnp.zeros_like(st_ref)

    ti = lax.broadcasted_iota(jnp.int32, (c, c), 0)
    si = lax.broadcasted_iota(jnp.int32, (c, c), 1)
    causal = si <= ti
    tri = causal.astype(BF16)
    nt = (((1,), (1,)), ((), ()))
    tn = (((0,), (0,)), ((), ()))
    qscale = dk ** -0.5
    chunk_rows = [slice(ci * c, (ci + 1) * c) for ci in range(n_chunks)]
    head_cols = [slice(h * dk, (h + 1) * dk) for h in range(n_heads)]

    lb = lb_ref[...]
    c0 = 0.5 * (1.0 + lb)
    c1 = 0.5 * (1.0 - lb)
    tf = jnp.tanh(0.5 * f_ref[...].astype(F32))
    f_s[...] = c1 * (1.0 - tf)
    gl = jnp.log2(c0 + c1 * tf)
    g_hi = gl.astype(BF16)
    gh_s[:, :w] = g_hi
    gh_s[:, w:] = (gl - g_hi.astype(F32)).astype(BF16)

    for rows in chunk_rows:
        bcat = jnp.dot(tri, gh_s[rows, :], preferred_element_type=F32)
        b_s[rows, :] = bcat[:, :w] + bcat[:, w:]

    for ci, rows in enumerate(chunk_rows):
        bc = b_s[rows, :]
        b_last = bc[c - 1:c, :]
        r = 0.5 * b_last
        qv = q_ref[rows, :].astype(F32)
        e_fwd = jnp.exp2(bc - r)
        qt = (qv * (0.5 * qscale)) * (1.0 + jnp.tanh(0.5 * qv)) * e_fwd
        kt = f_s[rows, :] * (1.0 / e_fwd)
        qt_s[rows, :] = qt.astype(BF16)
        kt_s[rows, :] = kt.astype(BF16)
        qe_s[rows, :] = (qt * jnp.exp2(r)).astype(BF16)
        kd_s[rows, :] = (kt * jnp.exp2(b_last - r)).astype(BF16)
        dl_s[ci:ci + 1, :] = jnp.exp2(b_last)

    units = [(ci, rows, h, cols) for ci, rows in enumerate(chunk_rows)
             for h, cols in enumerate(head_cols)]
    for ci, rows, h, cols in units:
        a = lax.dot_general(qt_s[rows, cols], kt_s[rows, cols], nt, preferred_element_type=F32)
        a_s[ci * n_heads + h] = jnp.where(causal, a, 0.0).astype(BF16)
    for ci, rows, h, cols in units:
        u_s[ci, h] = lax.dot_general(
            i_ref[rows, cols], kd_s[rows, cols], tn, preferred_element_type=F32)
    for ci, rows, h, cols in units:
        oi_s[rows, cols] = jnp.dot(
            a_s[ci * n_heads + h], i_ref[rows, cols], preferred_element_type=F32)

    gn_half = 0.5 * gn_ref[...]
    for ci, rows in enumerate(chunk_rows):
        for h, cols in enumerate(head_cols):
            st = st_ref[h]
            o = oi_s[rows, cols] + lax.dot_general(
                qe_s[rows, cols], st.astype(BF16), nt, preferred_element_type=F32)
            st_ref[h] = st * dl_s[ci:ci + 1, cols] + u_s[ci, h]
            ms = jnp.mean(o * o, axis=-1, keepdims=True)
            ogv = og_ref[rows, cols].astype(F32)
            o_ref[rows, cols] = (
                (o * lax.rsqrt(ms + EPS) * gn_half)
                * (ogv * (1.0 + jnp.tanh(0.5 * ogv)))).astype(o_ref.dtype)


def _hgrn(proj3, lb, gn, off_q, off_f, off_i, off_og, width, ts):
    b, s, _ = proj3.shape
    n_heads = width // HGRN_HEAD_DIM
    n_chunks = ts // HGRN_CHUNK
    dk = HGRN_HEAD_DIM
    spec = lambda off: pl.BlockSpec((None, ts, width), lambda i, t: (i, t, off // width))
    return pl.pallas_call(
        functools.partial(_hgrn_kernel, n_heads=n_heads, n_chunks=n_chunks),
        grid=(b, s // ts),
        in_specs=[
            spec(off_q), spec(off_f), spec(off_i), spec(off_og),
            pl.BlockSpec((1, width), lambda i, t: (0, 0)),
            pl.BlockSpec((1, dk), lambda i, t: (0, 0)),
        ],
        out_specs=pl.BlockSpec((None, ts, width), lambda i, t: (i, t, 0)),
        out_shape=jax.ShapeDtypeStruct((b, s, width), BF16),
        scratch_shapes=[
            pltpu.VMEM((n_heads, dk, dk), F32),
            pltpu.VMEM((ts, 2 * width), BF16),
            pltpu.VMEM((ts, width), F32),
            pltpu.VMEM((ts, width), F32),
            pltpu.VMEM((ts, width), BF16),
            pltpu.VMEM((ts, width), BF16),
            pltpu.VMEM((ts, width), BF16),
            pltpu.VMEM((ts, width), BF16),
            pltpu.VMEM((max(n_chunks, 8), width), F32),
            pltpu.VMEM((ts, width), F32),
            pltpu.VMEM((n_chunks, n_heads, dk, dk), F32),
            pltpu.VMEM((n_chunks * n_heads, HGRN_CHUNK, HGRN_CHUNK), BF16),
        ],
        compiler_params=_cparams(("arbitrary", "arbitrary")),
        name="hgrn2",
    )(proj3, proj3, proj3, proj3, lb, gn)


def _merge_kernel(sink_ref, q_ref, kc_ref, kp_ref, vc_ref, vp_ref,
                  oh_ref, g0_ref, g1_ref, x_ref, wa_ref, wh_ref, wo_ref, gm_ref,
                  wrh_ref, wrl_ref, br_ref, h_ref, hn_ref, route_ref, plan_ref, cnt_ref,
                  oa_s, hn_s, *, tiles_per_seq):
    tm, d = x_ref.shape
    half = d // 2
    halves = (slice(0, half), slice(half, d))
    step = pl.program_id(0)
    blk = ATTN_BLOCK
    assert tm == 2 * blk

    @pl.when(step == 0)
    def _():
        cnt_ref[...] = jnp.zeros_like(cnt_ref)
        oa_s[...] = jnp.zeros_like(oa_s)
        hn_s[...] = jnp.zeros_like(hn_s)

    q_blocks = (slice(0, blk), slice(blk, tm))
    k_blocks = [kc_ref[rs, :] for rs in q_blocks]
    v_blocks = [vc_ref[rs, :] for rs in q_blocks]
    scores = [_attn_scores(q_ref, q_blocks[0], k_blocks[0], kp_ref[...]),
              _attn_scores(q_ref, q_blocks[1], k_blocks[1], k_blocks[0])]
    attn_tile = jnp.minimum(step, pl.num_programs(0) - 3)
    prev_bias = jnp.where(attn_tile % tiles_per_seq == 0, -jnp.inf, 0.0)

    hn_prev = hn_s[...]
    hn_hi = hn_prev.astype(BF16)
    hn_lo = (hn_prev - hn_hi.astype(F32)).astype(BF16)

    oa = oa_s[...]
    ya = [jnp.dot(oa, wa_ref[:, cs], preferred_element_type=F32) for cs in halves]
    yh = [jnp.dot(oh_ref[...], wh_ref[:, cs], preferred_element_type=F32) for cs in halves]

    logits = (jnp.dot(hn_hi, wrh_ref[...], preferred_element_type=F32)
              + jnp.dot(hn_hi, wrl_ref[...], preferred_element_type=F32)
              + jnp.dot(hn_lo, wrh_ref[...], preferred_element_type=F32)
              + br_ref[...])

    _attn_finish(scores[0], v_blocks[0], vp_ref[...], sink_ref, prev_bias, oa_s, q_blocks[0])
    _attn_finish(scores[1], v_blocks[1], v_blocks[0], sink_ref, 0.0, oa_s, q_blocks[1])

    lane = lax.broadcasted_iota(jnp.int32, (tm, ROUTE_LANES), 1)
    neg = -jnp.inf

    def first_argmax(v):
        m = jnp.max(v, axis=-1, keepdims=True)
        idx = jnp.min(jnp.where(v == m, lane, ROUTE_LANES), axis=-1, keepdims=True)
        return m, idx

    is_group = lane < N_GROUPS
    gmax, gidx = first_argmax(jnp.where(is_group, logits, neg))
    p_sel = 1.0 / jnp.sum(jnp.where(is_group, jnp.exp(logits - gmax), 0.0), axis=-1, keepdims=True)
    eidx = lane - EXPERT_LANE0
    in_group = jnp.logical_and(
        jnp.logical_and(eidx >= 0, eidx < N_EXPERTS), (eidx // EXPERTS_PER_GROUP) == gidx)
    el = jnp.where(in_group, logits, neg)
    m1, i1 = first_argmax(el)
    m2, i2 = first_argmax(jnp.where(lane == i1, neg, el))
    t = jnp.exp(m2 - m1)
    w1 = p_sel / (1.0 + t)
    w2 = p_sel * t / (1.0 + t)

    sel1 = lane == i1
    sel2 = lane == i2
    onehot = jnp.logical_or(sel1, sel2).astype(BF16)
    ri = lax.broadcasted_iota(jnp.int32, (tm, tm), 0)
    ci = lax.broadcasted_iota(jnp.int32, (tm, tm), 1)
    before = (ci < ri).astype(BF16)

    h = x_ref[...]
    for k, cs in enumerate(halves):
        merged = (_sigmoid(g0_ref[:, cs].astype(F32)) * ya[k]
                  + _sigmoid(g1_ref[:, cs].astype(F32)) * yh[k]).astype(BF16)
        h = h + jnp.dot(merged, wo_ref[cs, :], preferred_element_type=F32)
        if k == 0:
            cum = jnp.dot(before, onehot, preferred_element_type=F32) + cnt_ref[...]


    r1 = jnp.sum(jnp.where(sel1, cum, 0.0), axis=-1, keepdims=True)
    r2 = jnp.sum(jnp.where(sel2, cum, 0.0), axis=-1, keepdims=True)
    routed = jnp.where(step > 1, 1.0, 0.0)
    cnt_ref[...] += routed * jnp.sum(onehot.astype(F32), axis=0, keepdims=True)

    e1 = (i1 - EXPERT_LANE0).astype(F32)
    e2 = (i2 - EXPERT_LANE0).astype(F32)
    out = jnp.zeros((tm, ROUTE_LANES), F32)
    for k, v in enumerate((e1, e2, w1, w2, r1, r2)):
        out = jnp.where(lane == k, v, out)
    route_ref[...] = out
    plan_ref[...] = out.T[:PLAN_ROWS, :].astype(jnp.int32)

    h_ref[...] = h
    ms = jnp.mean(h * h, axis=-1, keepdims=True)
    hn = h * lax.rsqrt(ms + EPS) * gm_ref[...]
    hn_ref[...] = _to_row_tiles(hn)
    hn_s[...] = hn


def _attn_merge_route(sinks, o_h, proj, x2, wa, wh, wo, gm, wr_hi, wr_lo, br,
                      off, attn_w, kv_w, seq, tm):
    t, d = x2.shape
    nt = t // tm
    hw = o_h.shape[1]
    blk = ATTN_BLOCK
    const = lambda shape: pl.BlockSpec(shape, lambda i: (0, 0), pipeline_mode=pl.Buffered(1))
    att = lambda i: jnp.minimum(i, nt - 1)
    cur = lambda i: jnp.clip(i - 1, 0, nt - 1)
    prv = lambda i: jnp.clip(i - 2, 0, nt - 1)
    row = lambda w: pl.BlockSpec((tm, w), lambda i: (cur(i), 0))
    kv_col = {"k": off["k"] // kv_w, "v": off["v"] // kv_w}
    kv_cur = lambda n: pl.BlockSpec((tm, kv_w), lambda i: (att(i), kv_col[n]))
    kv_prev = lambda n: pl.BlockSpec(
        (blk, kv_w), lambda i: (jnp.maximum(att(i) * (tm // blk) - 1, 0), kv_col[n]))
    return pl.pallas_call(
        functools.partial(_merge_kernel, tiles_per_seq=seq // tm),
        grid=(nt + 2,),
        in_specs=[
            pl.BlockSpec(memory_space=pltpu.SMEM),
            pl.BlockSpec((tm, attn_w), lambda i: (att(i), off["qa"] // attn_w)),
            kv_cur("k"), kv_prev("k"), kv_cur("v"), kv_prev("v"),
            row(hw),
            pl.BlockSpec((tm, d), lambda i: (cur(i), off["g0"] // d)),
            pl.BlockSpec((tm, d), lambda i: (cur(i), off["g1"] // d)),
            row(d),
            const((attn_w, d)), const((hw, d)), const((d, d)), const((1, d)),
            const((d, ROUTE_LANES)), const((d, ROUTE_LANES)), const((1, ROUTE_LANES)),
        ],
        out_specs=[row(d), pl.BlockSpec((tm, d // LANES, LANES), lambda i: (cur(i), 0, 0)),
                   pl.BlockSpec((tm, ROUTE_LANES), lambda i: (prv(i), 0)),
                   pl.BlockSpec((None, PLAN_ROWS, tm), lambda i: (prv(i), 0, 0)),
                   pl.BlockSpec((1, ROUTE_LANES), lambda i: (0, 0))],
        out_shape=[
            jax.ShapeDtypeStruct((t, d), F32),
            jax.ShapeDtypeStruct((t, d // LANES, LANES), BF16),
            jax.ShapeDtypeStruct((t, ROUTE_LANES), F32),
            jax.ShapeDtypeStruct((nt, PLAN_ROWS, tm), jnp.int32),
            jax.ShapeDtypeStruct((1, ROUTE_LANES), F32),
        ],
        scratch_shapes=[pltpu.VMEM((tm, attn_w), BF16), pltpu.VMEM((tm, d), F32)],
        compiler_params=_cparams(("arbitrary",)),
        name="attn_merge_route",
    )(sinks, proj, proj, proj, proj, proj, o_h, proj, proj, x2, wa, wh, wo, gm, wr_hi, wr_lo, br)


def _dest_kernel(pst_ref, plan_ref, dest_ref):
    e = plan_ref[:, PLAN_E:PLAN_E + 2, :]
    start = jnp.zeros(e.shape, jnp.int32)
    for j in range(N_EXPERTS):
        start = jnp.where(e == j, pst_ref[j], start)
    dest_ref[...] = start + plan_ref[:, PLAN_R:PLAN_R + 2, :]


def _dest_rows(plan, pst, tiles):
    nt, _, tm = plan.shape
    grid_spec = pltpu.PrefetchScalarGridSpec(
        num_scalar_prefetch=1,
        grid=(nt // tiles,),
        in_specs=[pl.BlockSpec((tiles, PLAN_ROWS, tm), lambda i, *_: (i, 0, 0))],
        out_specs=pl.BlockSpec((tiles, 2, tm), lambda i, *_: (i, 0, 0)),
    )
    return pl.pallas_call(
        _dest_kernel,
        grid_spec=grid_spec,
        out_shape=jax.ShapeDtypeStruct((nt, 2, tm), jnp.int32),
        compiler_params=_cparams(("arbitrary",)),
        name="moe_dest",
    )(pst, plan)


ISSUE_UNROLL = 8


def _dispatch_kernel(fill_ref, nfill_ref, dest_ref, hn_ref, rows_ref, zbuf, sem):
    tm = hn_ref.shape[0]
    bm = zbuf.shape[0]

    @pl.when(pl.program_id(0) == 0)
    def _():
        zbuf[...] = jnp.zeros_like(zbuf)

        def fill(j, carry):
            start = pl.multiple_of(fill_ref[j], bm)
            pltpu.make_async_copy(zbuf, rows_ref.at[pl.ds(start, bm)], sem).start()
            return carry

        def drain(j, carry):
            pltpu.make_async_copy(zbuf, rows_ref.at[pl.ds(0, bm)], sem).wait()
            return carry

        lax.fori_loop(0, nfill_ref[0], fill, 0)
        lax.fori_loop(0, nfill_ref[0], drain, 0)

    sub, _, tmd = dest_ref.shape

    for j in range(sub):
        def issue(r, carry, j=j):
            for k in range(2):
                pltpu.make_async_copy(
                    hn_ref.at[j * tmd + r], rows_ref.at[dest_ref[j, k, r]], sem
                ).start(priority=k)
            return carry

        lax.fori_loop(0, tmd, issue, 0, unroll=ISSUE_UNROLL)
    for _ in range(2):
        pltpu.make_async_copy(hn_ref, rows_ref.at[pl.ds(0, tm)], sem).wait()


def _dispatch(hn, dest, fill, nfill, n_rows, sub, bm):
    t, c, l = hn.shape
    nt, _, tmd = dest.shape
    tm = sub * tmd
    grid_spec = pltpu.PrefetchScalarGridSpec(
        num_scalar_prefetch=2,
        grid=(nt // sub,),
        in_specs=[
            pl.BlockSpec((sub, 2, tmd), lambda i, *_: (i, 0, 0), memory_space=pltpu.SMEM),
            pl.BlockSpec((tm, c, l), lambda i, *_: (i, 0, 0)),
        ],
        out_specs=pl.BlockSpec(memory_space=pl.ANY),
        scratch_shapes=[pltpu.VMEM((bm, c, l), hn.dtype), pltpu.SemaphoreType.DMA],
    )
    return pl.pallas_call(
        _dispatch_kernel,
        grid_spec=grid_spec,
        out_shape=jax.ShapeDtypeStruct((n_rows, c, l), hn.dtype),
        compiler_params=_cparams(("arbitrary",)),
        name="moe_dispatch",
    )(fill, nfill, dest, hn)


def _expert_kernel(be_ref, meta_ref, slot_ref, nxt_ref, x_ref, w1_hbm, w3_hbm, w2_hbm, y_ref,
                   wf1, wf3, wf2, w1b, w3b, w2b, sems):
    i = pl.program_id(0)
    e = be_ref[i]
    new_expert = jnp.logical_or(i == 0, e != be_ref[jnp.maximum(i - 1, 0)])

    def weight_copies(expert, slot):
        return [pltpu.make_async_copy(src.at[expert], dst.at[slot], sems.at[slot])
                for src, dst in ((w1_hbm, wf1), (w3_hbm, wf3), (w2_hbm, wf2))]

    @pl.when(jnp.logical_and(i < meta_ref[0], new_expert))
    def _():
        slot = slot_ref[e]
        nxt = nxt_ref[e]

        @pl.when(i == 0)
        def _():
            for c in weight_copies(e, slot):
                c.start()

        @pl.when(nxt >= 0)
        def _():
            for c in weight_copies(nxt, 1 - slot):
                c.start(priority=1)

        for c in weight_copies(e, slot):
            c.wait()
        w1b[...] = wf1[slot].astype(BF16)
        w3b[...] = wf3[slot].astype(BF16)
        w2b[...] = wf2[slot].astype(BF16)

    @pl.when(i < meta_ref[0])
    def _():
        bm = x_ref.shape[0]
        parts = (slice(0, bm // 2), slice(bm // 2, bm))
        xs = [_from_row_tiles(x_ref[rs]) for rs in parts]
        ups = [(jnp.dot(x, w1b[...], preferred_element_type=F32),
                jnp.dot(x, w3b[...], preferred_element_type=F32)) for x in xs]
        for rs, (h1, h3) in zip(parts, ups):
            hb = (h1 * _sigmoid(h1) * h3).astype(BF16)
            y_ref[rs] = _to_row_tiles(jnp.dot(hb, w2b[...], preferred_element_type=F32))

    @pl.when(i >= meta_ref[0])
    def _():
        y_ref[...] = jnp.zeros_like(y_ref)


def _experts(rows, block_expert, meta, slot_e, nxt_e, w1, w3, w2, bm):
    n_rows, c, l = rows.shape
    d = c * l
    de = w1.shape[-1]
    n_blocks = n_rows // bm
    hbm = pl.BlockSpec(memory_space=pl.ANY)
    grid_spec = pltpu.PrefetchScalarGridSpec(
        num_scalar_prefetch=4,
        grid=(n_blocks,),
        in_specs=[
            pl.BlockSpec((bm, c, l), lambda i, be, meta, *_: (jnp.minimum(i, meta[0] - 1), 0, 0)),
            hbm, hbm, hbm,
        ],
        out_specs=pl.BlockSpec((bm, c, l), lambda i, *_: (i, 0, 0)),
        scratch_shapes=[
            pltpu.VMEM((2, d, de), w1.dtype), pltpu.VMEM((2, d, de), w3.dtype),
            pltpu.VMEM((2, de, d), w2.dtype),
            pltpu.VMEM((d, de), BF16), pltpu.VMEM((d, de), BF16), pltpu.VMEM((de, d), BF16),
            pltpu.SemaphoreType.DMA((2,)),
        ],
    )
    return pl.pallas_call(
        _expert_kernel,
        grid_spec=grid_spec,
        out_shape=jax.ShapeDtypeStruct((n_rows, c, l), BF16),
        compiler_params=_cparams(("arbitrary",)),
        name="moe_experts",
    )(block_expert, meta, slot_e, nxt_e, rows, w1, w3, w2)


def _combine_kernel(dcur_ref, dnext_ref, h_ref, route_ref, g_ref, y_ref, o_ref, ybuf, sems):
    i = pl.program_id(0)
    nt = pl.num_programs(0)
    tm, d = h_ref.shape
    slot = i % 2

    def gather(dest_ref, s):
        def issue(r, carry):
            for k in range(2):
                pltpu.make_async_copy(
                    y_ref.at[dest_ref[k, r]], ybuf.at[s, k, r], sems.at[s]).start(priority=k)
            return carry
        lax.fori_loop(0, tm, issue, 0, unroll=ISSUE_UNROLL)

    @pl.when(i == 0)
    def _():
        gather(dcur_ref, slot)

    @pl.when(i + 1 < nt)
    def _():
        gather(dnext_ref, 1 - slot)

    for k in range(2):
        pltpu.make_async_copy(y_ref.at[pl.ds(0, tm)], ybuf.at[slot, k], sems.at[slot]).wait()

    route = route_ref[...]
    w1 = route[:, 2:3]
    w2 = route[:, 3:4]
    ya = _from_row_tiles(ybuf[slot, 0]).astype(F32)
    yb = _from_row_tiles(ybuf[slot, 1]).astype(F32)
    h = h_ref[...] + w1 * ya + w2 * yb
    ms = jnp.mean(h * h, axis=-1, keepdims=True)
    o_ref[...] = h * lax.rsqrt(ms + EPS) * g_ref[...]


def _combine(h, route, dest, y_rows, g, tm):
    t, d = h.shape
    nt = t // tm
    dest_spec = lambda f: pl.BlockSpec((None, 2, tm), f, memory_space=pltpu.SMEM)
    return pl.pallas_call(
        _combine_kernel,
        grid=(nt,),
        in_specs=[
            dest_spec(lambda i: (i, 0, 0)),
            dest_spec(lambda i: (jnp.minimum(i + 1, nt - 1), 0, 0)),
            pl.BlockSpec((tm, d), lambda i: (i, 0)),
            pl.BlockSpec((tm, ROUTE_LANES), lambda i: (i, 0)),
            pl.BlockSpec((1, d), lambda i: (0, 0)),
            pl.BlockSpec(memory_space=pl.ANY),
        ],
        out_specs=pl.BlockSpec((tm, d), lambda i: (i, 0)),
        out_shape=jax.ShapeDtypeStruct((t, d), F32),
        scratch_shapes=[pltpu.VMEM((2, 2, tm, d // LANES, LANES), BF16),
                        pltpu.SemaphoreType.DMA((2,))],
        compiler_params=_cparams(("arbitrary",)),
        name="moe_combine",
    )(dest, dest, h, route, g, y_rows)


def kernel(x, w_in, attn_sinks, hgrn_lb_logits, hgrn_norm_g, w_br_attn, w_br_hgrn, w_out,
           mix_norm_g, moe_norm_g, w_router_group, b_router_group, w_router_expert,
           b_router_expert, w1, w3, w2, final_norm_g):
    b, s, d = x.shape
    t = b * s
    depth = w_in.shape[0]
    assert depth == 1
    attn_w = (d // 128) * ATTN_HEAD_DIM
    kv_w = attn_w // Q_PER_KV
    hg_w = (d // 256) * HGRN_HEAD_DIM
    assert kv_w == N_KV_HEADS * ATTN_HEAD_DIM

    src = {}
    off = 0
    for name, width in (("qa", attn_w), ("k", kv_w), ("v", kv_w), ("qh", hg_w), ("f", hg_w),
                        ("i", hg_w), ("og", hg_w), ("g0", d), ("g1", d)):
        src[name] = (off, width)
        off += width
    order = ("g0", "g1", "qa", "qh", "f", "i", "og", "k", "v")
    dst = {}
    off = 0
    for name in order:
        dst[name] = off
        off += src[name][1]
    n_in = off
    l = 0
    tn_in = 512
    assert all(src[n][0] % tn_in == 0 for n in order if n != "v") and (2 * kv_w) % tn_in == 0
    col_blocks = []
    for n in order[:-1]:
        width = src[n][1] if n != "k" else 2 * kv_w
        col_blocks += [src[n][0] // tn_in + c for c in range(width // tn_in)]
    col_blocks = jnp.asarray(col_blocks, jnp.int32)
    w_bf = w_in[l]

    lb = jnp.cumsum(jax.nn.softmax(hgrn_lb_logits.astype(F32), axis=0), axis=0)[l].reshape(1, hg_w)

    x2 = x.reshape(t, d)
    tm_in = min(2048, t)
    proj = _in_proj(x2, mix_norm_g[l].reshape(1, d), w_bf, col_blocks, tm_in, tn_in)
    proj3 = proj.reshape(b, s, n_in)

    o_h = _hgrn(proj3, lb, hgrn_norm_g[l].reshape(1, HGRN_HEAD_DIM).astype(F32),
                dst["qh"], dst["f"], dst["i"], dst["og"], hg_w, min(256, s))

    w_r = jnp.zeros((d, ROUTE_LANES), F32)
    w_r = w_r.at[:, :N_GROUPS].set(w_router_group[l])
    w_r = w_r.at[:, EXPERT_LANE0:EXPERT_LANE0 + N_EXPERTS].set(w_router_expert[l])
    wr_hi = w_r.astype(BF16)
    wr_lo = (w_r - wr_hi.astype(F32)).astype(BF16)
    b_r = jnp.zeros((1, ROUTE_LANES), F32)
    b_r = b_r.at[0, :N_GROUPS].set(b_router_group[l])
    b_r = b_r.at[0, EXPERT_LANE0:EXPERT_LANE0 + N_EXPERTS].set(b_router_expert[l])

    tm = 2 * ATTN_BLOCK
    h, hn, route, plan, cnt = _attn_merge_route(
        attn_sinks[l].astype(F32), o_h.reshape(t, hg_w), proj, x2,
        w_br_attn[l].astype(BF16), w_br_hgrn[l].astype(BF16), w_out[l].astype(BF16),
        moe_norm_g[l].reshape(1, d), wr_hi, wr_lo, b_r, dst, attn_w, kv_w, s, tm)

    bm = MOE_ROWS
    counts = cnt[0, EXPERT_LANE0:EXPERT_LANE0 + N_EXPERTS].astype(jnp.int32)
    padded = ((counts + bm - 1) // bm) * bm
    pad_end = jnp.cumsum(padded)
    pad_start = (pad_end - padded).astype(jnp.int32)
    n_blocks = (2 * t) // bm + N_EXPERTS
    n_used = pad_end[-1] // bm
    blk_ids = jnp.minimum(jnp.arange(n_blocks, dtype=jnp.int32), n_used - 1)
    block_expert = jnp.minimum(
        jnp.sum((pad_end[None, :] <= (blk_ids * bm)[:, None]).astype(jnp.int32), axis=1),
        N_EXPERTS - 1)
    meta = jnp.stack([n_used, n_used]).astype(jnp.int32)

    blk_all = jnp.arange(n_blocks, dtype=jnp.int32)
    fill = jnp.concatenate([
        jnp.maximum(pad_end - bm, 0).astype(jnp.int32),
        jnp.minimum(n_used + blk_all, n_blocks - 1) * bm])
    nfill = (N_EXPERTS + n_blocks - n_used).astype(jnp.int32).reshape(1)

    eid = jnp.arange(N_EXPERTS, dtype=jnp.int32)
    nonempty = counts > 0
    slot_e = ((jnp.cumsum(nonempty.astype(jnp.int32)) - 1) % 2).astype(jnp.int32)
    later = jnp.logical_and(nonempty[None, :], eid[None, :] > eid[:, None])
    nxt_e = jnp.min(jnp.where(later, eid[None, :], N_EXPERTS), axis=1)
    nxt_e = jnp.where(nxt_e == N_EXPERTS, -1, nxt_e).astype(jnp.int32)

    nt = t // tm
    dest = _dest_rows(plan, pad_start, min(8, nt))
    rows = _dispatch(hn, dest, fill, nfill, n_blocks * bm, min(8, nt), bm)
    y_rows = _experts(rows, block_expert, meta, slot_e, nxt_e, w1[l], w3[l], w2[l], bm)
    out = _combine(h, route, dest, y_rows, final_norm_g.reshape(1, d), tm)
    return out.reshape(b, s, d)
```

```python
import functools

import jax
import jax.numpy as jnp
from jax import lax
from jax.experimental import pallas as pl
from jax.experimental.pallas import tpu as pltpu

F32 = jnp.float32
BF16 = jnp.bfloat16

EPS = 1e-6
ATTN_HEAD_DIM = 64
Q_PER_KV = 4
N_KV_HEADS = 4
ATTN_BLOCK = 128
HGRN_HEAD_DIM = 128
HGRN_CHUNK = 64
N_GROUPS = 4
EXPERTS_PER_GROUP = 8
N_EXPERTS = N_GROUPS * EXPERTS_PER_GROUP
LANES = 128
ROUTE_LANES = LANES
EXPERT_LANE0 = N_GROUPS
MOE_ROWS = 256
WEIGHT_STAGE_ROWS = 256
PLAN_ROWS = 8
PLAN_E, PLAN_R = 0, 4
VMEM_LIMIT = 56 * 1024 * 1024
VMEM_LIMIT_IN_PROJ = 60 * 1024 * 1024


def _sigmoid(x):
    return 0.5 * jnp.tanh(0.5 * x) + 0.5


def _cparams(sem, vmem=VMEM_LIMIT):
    return pltpu.CompilerParams(dimension_semantics=sem, vmem_limit_bytes=vmem)


def _to_row_tiles(v):
    m, d = v.shape
    return v.reshape(m, d // LANES, LANES).astype(BF16)


def _from_row_tiles(p):
    m, c, l = p.shape
    return p.reshape(m, c * l)


def _inproj_kernel(cols_ref, x_ref, g_ref, w_ref, o_ref, xn_ref):
    del cols_ref

    @pl.when(pl.program_id(1) == 0)
    def _():
        x = x_ref[...]
        ms = jnp.mean(x * x, axis=-1, keepdims=True)
        xn_ref[...] = (x * lax.rsqrt(ms + EPS) * g_ref[...]).astype(BF16)

    o_ref[...] = jnp.dot(xn_ref[...], w_ref[...].astype(BF16),
                         preferred_element_type=F32).astype(o_ref.dtype)


def _in_proj(x2, g, w, col_blocks, tm, tn):
    t, d = x2.shape
    n = w.shape[1]
    grid_spec = pltpu.PrefetchScalarGridSpec(
        num_scalar_prefetch=1,
        grid=(t // tm, n // tn),
        in_specs=[
            pl.BlockSpec((tm, d), lambda i, j, cols: (i, 0)),
            pl.BlockSpec((1, d), lambda i, j, cols: (0, 0)),
            pl.BlockSpec((d, tn), lambda i, j, cols: (0, cols[j])),
        ],
        out_specs=pl.BlockSpec((tm, tn), lambda i, j, cols: (i, j)),
        scratch_shapes=[pltpu.VMEM((tm, d), BF16)],
    )
    return pl.pallas_call(
        _inproj_kernel,
        grid_spec=grid_spec,
        out_shape=jax.ShapeDtypeStruct((t, n), BF16),
        compiler_params=_cparams(("arbitrary", "arbitrary"), VMEM_LIMIT_IN_PROJ),
        name="in_proj",
    )(col_blocks, x2, g, w)


def _attn_scores(q_ref, rows, k_cur, k_prev):
    dh, g = ATTN_HEAD_DIM, Q_PER_KV
    nt = (((1,), (1,)), ((), ()))
    scores = []
    for h in range(N_KV_HEADS):
        q4 = jnp.concatenate(
            [q_ref[rows, (h * g + j) * dh:(h * g + j + 1) * dh] for j in range(g)], axis=0)
        scores.append(
            (lax.dot_general(q4, k_cur[:, h * dh:(h + 1) * dh], nt, preferred_element_type=F32),
             lax.dot_general(q4, k_prev[:, h * dh:(h + 1) * dh], nt, preferred_element_type=F32)))
    return scores


def _attn_finish(scores, v_cur, v_prev, sink_ref, prev_bias, o_ref, rows):
    blk, dh, g = ATTN_BLOCK, ATTN_HEAD_DIM, Q_PER_KV
    n_rows = g * blk
    qi = lax.broadcasted_iota(jnp.int32, (n_rows, blk), 0) % blk
    kj = lax.broadcasted_iota(jnp.int32, (n_rows, blk), 1)
    mask_cur = kj <= qi
    head_of_row = lax.broadcasted_iota(jnp.int32, (n_rows, 1), 0) // blk
    scale = dh ** -0.5
    for h in range(N_KV_HEADS):
        sc, sp = scores[h]
        s = jnp.where(mask_cur, sc, sp + prev_bias) * scale
        sink = jnp.zeros((n_rows, 1), F32)
        for j in range(g):
            sink = jnp.where(head_of_row == j, sink_ref[h * g + j], sink)
        m = jnp.maximum(jnp.max(s, axis=-1, keepdims=True), sink)
        p = jnp.exp(s - m)
        den = jnp.sum(p, axis=-1, keepdims=True) + jnp.exp(sink - m)
        p_cur = jnp.where(mask_cur, p, 0.0).astype(BF16)
        p_prev = jnp.where(mask_cur, 0.0, p).astype(BF16)
        o = (jnp.dot(p_cur, v_cur[:, h * dh:(h + 1) * dh], preferred_element_type=F32)
             + jnp.dot(p_prev, v_prev[:, h * dh:(h + 1) * dh], preferred_element_type=F32)) / den
        for j in range(g):
            o_ref[rows, (h * g + j) * dh:(h * g + j + 1) * dh] = (
                o[j * blk:(j + 1) * blk, :].astype(o_ref.dtype))


def _hgrn_kernel(q_ref, f_ref, i_ref, og_ref, lb_ref, gn_ref, o_ref,
                 st_ref, gh_s, f_s, b_s, qt_s, kt_s, qe_s, kd_s, dl_s, oi_s, u_s, a_s,
                 *, n_heads, n_chunks):
    c, dk = HGRN_CHUNK, HGRN_HEAD_DIM
    w = n_heads * dk

    @pl.when(pl.program_id(1) == 0)
    def _():
        st_ref[...] = jnp.zeros_like(st_ref)

    ti = lax.broadcasted_iota(jnp.int32, (c, c), 0)
    si = lax.broadcasted_iota(jnp.int32, (c, c), 1)
    causal = si <= ti
    tri = causal.astype(BF16)
    nt = (((1,), (1,)), ((), ()))
    tn = (((0,), (0,)), ((), ()))
    qscale = dk ** -0.5
    chunk_rows = [slice(ci * c, (ci + 1) * c) for ci in range(n_chunks)]
    head_cols = [slice(h * dk, (h + 1) * dk) for h in range(n_heads)]

    lb = lb_ref[...]
    c0 = 0.5 * (1.0 + lb)
    c1 = 0.5 * (1.0 - lb)
    tf = jnp.tanh(0.5 * f_ref[...].astype(F32))
    f_s[...] = c1 * (1.0 - tf)
    gl = jnp.log(c0 + c1 * tf)
    g_hi = gl.astype(BF16)
    gh_s[:, :w] = g_hi
    gh_s[:, w:] = (gl - g_hi.astype(F32)).astype(BF16)

    for rows in chunk_rows:
        bcat = jnp.dot(tri, gh_s[rows, :], preferred_element_type=F32)
        b_s[rows, :] = bcat[:, :w] + bcat[:, w:]

    for ci, rows in enumerate(chunk_rows):
        bc = b_s[rows, :]
        b_last = bc[c - 1:c, :]
        r = 0.5 * b_last
        qv = q_ref[rows, :].astype(F32)
        qt = (qv * (0.5 * qscale)) * (1.0 + jnp.tanh(0.5 * qv)) * jnp.exp(bc - r)
        kt = f_s[rows, :] * jnp.exp(r - bc)
        qt_s[rows, :] = qt.astype(BF16)
        kt_s[rows, :] = kt.astype(BF16)
        qe_s[rows, :] = (qt * jnp.exp(r)).astype(BF16)
        kd_s[rows, :] = (kt * jnp.exp(b_last - r)).astype(BF16)
        dl_s[ci:ci + 1, :] = jnp.exp(b_last)

    units = [(ci, rows, h, cols) for ci, rows in enumerate(chunk_rows)
             for h, cols in enumerate(head_cols)]
    for ci, rows, h, cols in units:
        a = lax.dot_general(qt_s[rows, cols], kt_s[rows, cols], nt, preferred_element_type=F32)
        a_s[ci * n_heads + h] = jnp.where(causal, a, 0.0).astype(BF16)
    for ci, rows, h, cols in units:
        u_s[ci, h] = lax.dot_general(
            i_ref[rows, cols], kd_s[rows, cols], tn, preferred_element_type=F32)
    for ci, rows, h, cols in units:
        oi_s[rows, cols] = jnp.dot(
            a_s[ci * n_heads + h], i_ref[rows, cols], preferred_element_type=F32)

    gn_half = 0.5 * gn_ref[...]
    for ci, rows in enumerate(chunk_rows):
        for h, cols in enumerate(head_cols):
            st = st_ref[h]
            o = oi_s[rows, cols] + lax.dot_general(
                qe_s[rows, cols], st.astype(BF16), nt, preferred_element_type=F32)
            st_ref[h] = st * dl_s[ci:ci + 1, cols] + u_s[ci, h]
            ms = jnp.mean(o * o, axis=-1, keepdims=True)
            ogv = og_ref[rows, cols].astype(F32)
            o_ref[rows, cols] = (
                (o * lax.rsqrt(ms + EPS) * gn_half)
                * (ogv * (1.0 + jnp.tanh(0.5 * ogv)))).astype(o_ref.dtype)


def _hgrn(proj3, lb, gn, off_q, off_f, off_i, off_og, width, ts):
    b, s, _ = proj3.shape
    n_heads = width // HGRN_HEAD_DIM
    n_chunks = ts // HGRN_CHUNK
    dk = HGRN_HEAD_DIM
    spec = lambda off: pl.BlockSpec((None, ts, width), lambda i, t: (i, t, off // width))
    return pl.pallas_call(
        functools.partial(_hgrn_kernel, n_heads=n_heads, n_chunks=n_chunks),
        grid=(b, s // ts),
        in_specs=[
            spec(off_q), spec(off_f), spec(off_i), spec(off_og),
            pl.BlockSpec((1, width), lambda i, t: (0, 0)),
            pl.BlockSpec((1, dk), lambda i, t: (0, 0)),
        ],
        out_specs=pl.BlockSpec((None, ts, width), lambda i, t: (i, t, 0)),
        out_shape=jax.ShapeDtypeStruct((b, s, width), BF16),
        scratch_shapes=[
            pltpu.VMEM((n_heads, dk, dk), F32),
            pltpu.VMEM((ts, 2 * width), BF16),
            pltpu.VMEM((ts, width), F32),
            pltpu.VMEM((ts, width), F32),
            pltpu.VMEM((ts, width), BF16),
            pltpu.VMEM((ts, width), BF16),
            pltpu.VMEM((ts, width), BF16),
            pltpu.VMEM((ts, width), BF16),
            pltpu.VMEM((max(n_chunks, 8), width), F32),
            pltpu.VMEM((ts, width), F32),
            pltpu.VMEM((n_chunks, n_heads, dk, dk), F32),
            pltpu.VMEM((n_chunks * n_heads, HGRN_CHUNK, HGRN_CHUNK), BF16),
        ],
        compiler_params=_cparams(("arbitrary", "arbitrary")),
        name="hgrn2",
    )(proj3, proj3, proj3, proj3, lb, gn)


def _merge_kernel(sink_ref, q_ref, kc_ref, kp_ref, vc_ref, vp_ref,
                  oh_ref, g0_ref, g1_ref, x_ref, wa_hbm, wh_hbm, wo_hbm, gm_ref,
                  wrh_ref, wrl_ref, br_ref, h_ref, hn_ref, route_ref, plan_ref, cnt_ref,
                  oa_s, hn_s, wa_ref, wh_ref, wo_ref, stage, stage_sem, *, tiles_per_seq):
    tm, d = x_ref.shape
    half = d // 2
    halves = (slice(0, half), slice(half, d))
    step = pl.program_id(0)
    blk = ATTN_BLOCK
    assert tm == 2 * blk

    @pl.when(step == 0)
    def _():
        cnt_ref[...] = jnp.zeros_like(cnt_ref)
        oa_s[...] = jnp.zeros_like(oa_s)
        hn_s[...] = jnp.zeros_like(hn_s)
        rows = stage.shape[1]
        chunks = [(src, dst, r0) for src, dst in ((wa_hbm, wa_ref), (wh_hbm, wh_ref), (wo_hbm, wo_ref))
                  for r0 in range(0, src.shape[0], rows)]

        def chunk_copy(n):
            src, _, r0 = chunks[n]
            return pltpu.make_async_copy(
                src.at[pl.ds(r0, rows)], stage.at[n % 2], stage_sem.at[n % 2])

        chunk_copy(0).start()
        for n, (_, dst, r0) in enumerate(chunks):
            if n + 1 < len(chunks):
                chunk_copy(n + 1).start()
            chunk_copy(n).wait()
            dst[r0:r0 + rows, :] = stage[n % 2].astype(BF16)

    q_blocks = (slice(0, blk), slice(blk, tm))
    k_blocks = [kc_ref[rs, :] for rs in q_blocks]
    v_blocks = [vc_ref[rs, :] for rs in q_blocks]
    scores = [_attn_scores(q_ref, q_blocks[0], k_blocks[0], kp_ref[...]),
              _attn_scores(q_ref, q_blocks[1], k_blocks[1], k_blocks[0])]
    attn_tile = jnp.minimum(step, pl.num_programs(0) - 3)
    prev_bias = jnp.where(attn_tile % tiles_per_seq == 0, -jnp.inf, 0.0)

    hn_prev = hn_s[...]
    hn_hi = hn_prev.astype(BF16)
    hn_lo = (hn_prev - hn_hi.astype(F32)).astype(BF16)

    oa = oa_s[...]
    ya = [jnp.dot(oa, wa_ref[:, cs], preferred_element_type=F32) for cs in halves]
    yh = [jnp.dot(oh_ref[...], wh_ref[:, cs], preferred_element_type=F32) for cs in halves]

    logits = (jnp.dot(hn_hi, wrh_ref[...], preferred_element_type=F32)
              + jnp.dot(hn_hi, wrl_ref[...], preferred_element_type=F32)
              + jnp.dot(hn_lo, wrh_ref[...], preferred_element_type=F32)
              + br_ref[...])

    _attn_finish(scores[0], v_blocks[0], vp_ref[...], sink_ref, prev_bias, oa_s, q_blocks[0])
    _attn_finish(scores[1], v_blocks[1], v_blocks[0], sink_ref, 0.0, oa_s, q_blocks[1])

    lane = lax.broadcasted_iota(jnp.int32, (tm, ROUTE_LANES), 1)
    neg = -jnp.inf

    def first_argmax(v):
        m = jnp.max(v, axis=-1, keepdims=True)
        idx = jnp.min(jnp.where(v == m, lane, ROUTE_LANES), axis=-1, keepdims=True)
        return m, idx

    is_group = lane < N_GROUPS
    gmax, gidx = first_argmax(jnp.where(is_group, logits, neg))
    p_sel = 1.0 / jnp.sum(jnp.where(is_group, jnp.exp(logits - gmax), 0.0), axis=-1, keepdims=True)
    eidx = lane - EXPERT_LANE0
    in_group = jnp.logical_and(
        jnp.logical_and(eidx >= 0, eidx < N_EXPERTS), (eidx // EXPERTS_PER_GROUP) == gidx)
    el = jnp.where(in_group, logits, neg)
    m1, i1 = first_argmax(el)
    m2, i2 = first_argmax(jnp.where(lane == i1, neg, el))
    t = jnp.exp(m2 - m1)
    w1 = p_sel / (1.0 + t)
    w2 = p_sel * t / (1.0 + t)

    sel1 = lane == i1
    sel2 = lane == i2
    onehot = jnp.logical_or(sel1, sel2).astype(BF16)
    ri = lax.broadcasted_iota(jnp.int32, (tm, tm), 0)
    ci = lax.broadcasted_iota(jnp.int32, (tm, tm), 1)
    before = (ci < ri).astype(BF16)

    h = x_ref[...]
    for k, cs in enumerate(halves):
        merged = (_sigmoid(g0_ref[:, cs].astype(F32)) * ya[k]
                  + _sigmoid(g1_ref[:, cs].astype(F32)) * yh[k]).astype(BF16)
        h = h + jnp.dot(merged, wo_ref[cs, :], preferred_element_type=F32)
        if k == 0:
            cum = jnp.dot(before, onehot, preferred_element_type=F32) + cnt_ref[...]


    r1 = jnp.sum(jnp.where(sel1, cum, 0.0), axis=-1, keepdims=True)
    r2 = jnp.sum(jnp.where(sel2, cum, 0.0), axis=-1, keepdims=True)
    routed = jnp.where(step > 1, 1.0, 0.0)
    cnt_ref[...] += routed * jnp.sum(onehot.astype(F32), axis=0, keepdims=True)

    e1 = (i1 - EXPERT_LANE0).astype(F32)
    e2 = (i2 - EXPERT_LANE0).astype(F32)
    out = jnp.zeros((tm, ROUTE_LANES), F32)
    for k, v in enumerate((e1, e2, w1, w2, r1, r2)):
        out = jnp.where(lane == k, v, out)
    route_ref[...] = out
    plan_ref[...] = out.T[:PLAN_ROWS, :].astype(jnp.int32)

    h_ref[...] = h
    ms = jnp.mean(h * h, axis=-1, keepdims=True)
    hn = h * lax.rsqrt(ms + EPS) * gm_ref[...]
    hn_ref[...] = _to_row_tiles(hn)
    hn_s[...] = hn


def _attn_merge_route(sinks, o_h, proj, x2, wa, wh, wo, gm, wr_hi, wr_lo, br,
                      off, attn_w, kv_w, seq, tm):
    t, d = x2.shape
    nt = t // tm
    hw = o_h.shape[1]
    blk = ATTN_BLOCK
    const = lambda shape: pl.BlockSpec(shape, lambda i: (0, 0), pipeline_mode=pl.Buffered(1))
    hbm = pl.BlockSpec(memory_space=pl.ANY)
    assert attn_w % WEIGHT_STAGE_ROWS == 0 and hw % WEIGHT_STAGE_ROWS == 0
    att = lambda i: jnp.minimum(i, nt - 1)
    cur = lambda i: jnp.clip(i - 1, 0, nt - 1)
    prv = lambda i: jnp.clip(i - 2, 0, nt - 1)
    row = lambda w: pl.BlockSpec((tm, w), lambda i: (cur(i), 0))
    kv_col = {"k": off["k"] // kv_w, "v": off["v"] // kv_w}
    kv_cur = lambda n: pl.BlockSpec((tm, kv_w), lambda i: (att(i), kv_col[n]))
    kv_prev = lambda n: pl.BlockSpec(
        (blk, kv_w), lambda i: (jnp.maximum(att(i) * (tm // blk) - 1, 0), kv_col[n]))
    return pl.pallas_call(
        functools.partial(_merge_kernel, tiles_per_seq=seq // tm),
        grid=(nt + 2,),
        in_specs=[
            pl.BlockSpec(memory_space=pltpu.SMEM),
            pl.BlockSpec((tm, attn_w), lambda i: (att(i), off["qa"] // attn_w)),
            kv_cur("k"), kv_prev("k"), kv_cur("v"), kv_prev("v"),
            row(hw),
            pl.BlockSpec((tm, d), lambda i: (cur(i), off["g0"] // d)),
            pl.BlockSpec((tm, d), lambda i: (cur(i), off["g1"] // d)),
            row(d),
            hbm, hbm, hbm, const((1, d)),
            const((d, ROUTE_LANES)), const((d, ROUTE_LANES)), const((1, ROUTE_LANES)),
        ],
        out_specs=[row(d), pl.BlockSpec((tm, d // LANES, LANES), lambda i: (cur(i), 0, 0)),
                   pl.BlockSpec((tm, ROUTE_LANES), lambda i: (prv(i), 0)),
                   pl.BlockSpec((None, PLAN_ROWS, tm), lambda i: (prv(i), 0, 0)),
                   pl.BlockSpec((1, ROUTE_LANES), lambda i: (0, 0))],
        out_shape=[
            jax.ShapeDtypeStruct((t, d), F32),
            jax.ShapeDtypeStruct((t, d // LANES, LANES), BF16),
            jax.ShapeDtypeStruct((t, ROUTE_LANES), F32),
            jax.ShapeDtypeStruct((nt, PLAN_ROWS, tm), jnp.int32),
            jax.ShapeDtypeStruct((1, ROUTE_LANES), F32),
        ],
        scratch_shapes=[
            pltpu.VMEM((tm, attn_w), BF16), pltpu.VMEM((tm, d), F32),
            pltpu.VMEM((attn_w, d), BF16), pltpu.VMEM((hw, d), BF16), pltpu.VMEM((d, d), BF16),
            pltpu.VMEM((2, WEIGHT_STAGE_ROWS, d), F32), pltpu.SemaphoreType.DMA((2,)),
        ],
        compiler_params=_cparams(("arbitrary",)),
        name="attn_merge_route",
    )(sinks, proj, proj, proj, proj, proj, o_h, proj, proj, x2, wa, wh, wo, gm, wr_hi, wr_lo, br)


def _dest_kernel(pst_ref, plan_ref, dest_ref):
    e = plan_ref[:, PLAN_E:PLAN_E + 2, :]
    start = jnp.zeros(e.shape, jnp.int32)
    for j in range(N_EXPERTS):
        start = jnp.where(e == j, pst_ref[j], start)
    dest_ref[...] = start + plan_ref[:, PLAN_R:PLAN_R + 2, :]


def _dest_rows(plan, pst, tiles):
    nt, _, tm = plan.shape
    grid_spec = pltpu.PrefetchScalarGridSpec(
        num_scalar_prefetch=1,
        grid=(nt // tiles,),
        in_specs=[pl.BlockSpec((tiles, PLAN_ROWS, tm), lambda i, *_: (i, 0, 0))],
        out_specs=pl.BlockSpec((tiles, 2, tm), lambda i, *_: (i, 0, 0)),
    )
    return pl.pallas_call(
        _dest_kernel,
        grid_spec=grid_spec,
        out_shape=jax.ShapeDtypeStruct((nt, 2, tm), jnp.int32),
        compiler_params=_cparams(("arbitrary",)),
        name="moe_dest",
    )(pst, plan)


ISSUE_UNROLL = 8


def _dispatch_kernel(fill_ref, nfill_ref, dest_ref, hn_ref, rows_ref, zbuf, sem):
    tm = hn_ref.shape[0]
    bm = zbuf.shape[0]

    @pl.when(pl.program_id(0) == 0)
    def _():
        zbuf[...] = jnp.zeros_like(zbuf)

        def fill(j, carry):
            start = pl.multiple_of(fill_ref[j], bm)
            pltpu.make_async_copy(zbuf, rows_ref.at[pl.ds(start, bm)], sem).start()
            return carry

        def drain(j, carry):
            pltpu.make_async_copy(zbuf, rows_ref.at[pl.ds(0, bm)], sem).wait()
            return carry

        lax.fori_loop(0, nfill_ref[0], fill, 0)
        lax.fori_loop(0, nfill_ref[0], drain, 0)

    sub, _, tmd = dest_ref.shape

    for j in range(sub):
        def issue(r, carry, j=j):
            for k in range(2):
                pltpu.make_async_copy(
                    hn_ref.at[j * tmd + r], rows_ref.at[dest_ref[j, k, r]], sem
                ).start(priority=k)
            return carry

        lax.fori_loop(0, tmd, issue, 0, unroll=ISSUE_UNROLL)
    for _ in range(2):
        pltpu.make_async_copy(hn_ref, rows_ref.at[pl.ds(0, tm)], sem).wait()


def _dispatch(hn, dest, fill, nfill, n_rows, sub, bm):
    t, c, l = hn.shape
    nt, _, tmd = dest.shape
    tm = sub * tmd
    grid_spec = pltpu.PrefetchScalarGridSpec(
        num_scalar_prefetch=2,
        grid=(nt // sub,),
        in_specs=[
            pl.BlockSpec((sub, 2, tmd), lambda i, *_: (i, 0, 0), memory_space=pltpu.SMEM),
            pl.BlockSpec((tm, c, l), lambda i, *_: (i, 0, 0)),
        ],
        out_specs=pl.BlockSpec(memory_space=pl.ANY),
        scratch_shapes=[pltpu.VMEM((bm, c, l), hn.dtype), pltpu.SemaphoreType.DMA],
    )
    return pl.pallas_call(
        _dispatch_kernel,
        grid_spec=grid_spec,
        out_shape=jax.ShapeDtypeStruct((n_rows, c, l), hn.dtype),
        compiler_params=_cparams(("arbitrary",)),
        name="moe_dispatch",
    )(fill, nfill, dest, hn)


def _expert_kernel(be_ref, meta_ref, slot_ref, nxt_ref, x_ref, w1_hbm, w3_hbm, w2_hbm, y_ref,
                   wf1, wf3, wf2, w1b, w3b, w2b, sems):
    i = pl.program_id(0)
    e = be_ref[i]
    new_expert = jnp.logical_or(i == 0, e != be_ref[jnp.maximum(i - 1, 0)])

    def weight_copies(expert, slot):
        return [pltpu.make_async_copy(src.at[expert], dst.at[slot], sems.at[slot])
                for src, dst in ((w1_hbm, wf1), (w3_hbm, wf3), (w2_hbm, wf2))]

    @pl.when(jnp.logical_and(i < meta_ref[0], new_expert))
    def _():
        slot = slot_ref[e]
        nxt = nxt_ref[e]

        @pl.when(i == 0)
        def _():
            for c in weight_copies(e, slot):
                c.start()

        @pl.when(nxt >= 0)
        def _():
            for c in weight_copies(nxt, 1 - slot):
                c.start(priority=1)

        for c in weight_copies(e, slot):
            c.wait()
        w1b[...] = wf1[slot].astype(BF16)
        w3b[...] = wf3[slot].astype(BF16)
        w2b[...] = wf2[slot].astype(BF16)

    @pl.when(i < meta_ref[0])
    def _():
        bm = x_ref.shape[0]
        parts = (slice(0, bm // 2), slice(bm // 2, bm))
        xs = [_from_row_tiles(x_ref[rs]) for rs in parts]
        ups = [(jnp.dot(x, w1b[...], preferred_element_type=F32),
                jnp.dot(x, w3b[...], preferred_element_type=F32)) for x in xs]
        for rs, (h1, h3) in zip(parts, ups):
            hb = (h1 * _sigmoid(h1) * h3).astype(BF16)
            y_ref[rs] = _to_row_tiles(jnp.dot(hb, w2b[...], preferred_element_type=F32))

    @pl.when(i >= meta_ref[0])
    def _():
        y_ref[...] = jnp.zeros_like(y_ref)


def _experts(rows, block_expert, meta, slot_e, nxt_e, w1, w3, w2, bm):
    n_rows, c, l = rows.shape
    d = c * l
    de = w1.shape[-1]
    n_blocks = n_rows // bm
    hbm = pl.BlockSpec(memory_space=pl.ANY)
    grid_spec = pltpu.PrefetchScalarGridSpec(
        num_scalar_prefetch=4,
        grid=(n_blocks,),
        in_specs=[
            pl.BlockSpec((bm, c, l), lambda i, be, meta, *_: (jnp.minimum(i, meta[0] - 1), 0, 0)),
            hbm, hbm, hbm,
        ],
        out_specs=pl.BlockSpec((bm, c, l), lambda i, *_: (i, 0, 0)),
        scratch_shapes=[
            pltpu.VMEM((2, d, de), w1.dtype), pltpu.VMEM((2, d, de), w3.dtype),
            pltpu.VMEM((2, de, d), w2.dtype),
            pltpu.VMEM((d, de), BF16), pltpu.VMEM((d, de), BF16), pltpu.VMEM((de, d), BF16),
            pltpu.SemaphoreType.DMA((2,)),
        ],
    )
    return pl.pallas_call(
        _expert_kernel,
        grid_spec=grid_spec,
        out_shape=jax.ShapeDtypeStruct((n_rows, c, l), BF16),
        compiler_params=_cparams(("arbitrary",)),
        name="moe_experts",
    )(block_expert, meta, slot_e, nxt_e, rows, w1, w3, w2)


def _combine_kernel(dcur_ref, dnext_ref, h_ref, route_ref, g_ref, y_ref, o_ref, ybuf, sems):
    i = pl.program_id(0)
    nt = pl.num_programs(0)
    tm, d = h_ref.shape
    slot = i % 2

    def gather(dest_ref, s):
        def issue(r, carry):
            for k in range(2):
                pltpu.make_async_copy(
                    y_ref.at[dest_ref[k, r]], ybuf.at[s, k, r], sems.at[s]).start(priority=k)
            return carry
        lax.fori_loop(0, tm, issue, 0, unroll=ISSUE_UNROLL)

    @pl.when(i == 0)
    def _():
        gather(dcur_ref, slot)

    @pl.when(i + 1 < nt)
    def _():
        gather(dnext_ref, 1 - slot)

    for k in range(2):
        pltpu.make_async_copy(y_ref.at[pl.ds(0, tm)], ybuf.at[slot, k], sems.at[slot]).wait()

    route = route_ref[...]
    w1 = route[:, 2:3]
    w2 = route[:, 3:4]
    ya = _from_row_tiles(ybuf[slot, 0]).astype(F32)
    yb = _from_row_tiles(ybuf[slot, 1]).astype(F32)
    h = h_ref[...] + w1 * ya + w2 * yb
    ms = jnp.mean(h * h, axis=-1, keepdims=True)
    o_ref[...] = h * lax.rsqrt(ms + EPS) * g_ref[...]


def _combine(h, route, dest, y_rows, g, tm):
    t, d = h.shape
    nt = t // tm
    dest_spec = lambda f: pl.BlockSpec((None, 2, tm), f, memory_space=pltpu.SMEM)
    return pl.pallas_call(
        _combine_kernel,
        grid=(nt,),
        in_specs=[
            dest_spec(lambda i: (i, 0, 0)),
            dest_spec(lambda i: (jnp.minimum(i + 1, nt - 1), 0, 0)),
            pl.BlockSpec((tm, d), lambda i: (i, 0)),
            pl.BlockSpec((tm, ROUTE_LANES), lambda i: (i, 0)),
            pl.BlockSpec((1, d), lambda i: (0, 0)),
            pl.BlockSpec(memory_space=pl.ANY),
        ],
        out_specs=pl.BlockSpec((tm, d), lambda i: (i, 0)),
        out_shape=jax.ShapeDtypeStruct((t, d), F32),
        scratch_shapes=[pltpu.VMEM((2, 2, tm, d // LANES, LANES), BF16),
                        pltpu.SemaphoreType.DMA((2,))],
        compiler_params=_cparams(("arbitrary",)),
        name="moe_combine",
    )(dest, dest, h, route, g, y_rows)


def kernel(x, w_in, attn_sinks, hgrn_lb_logits, hgrn_norm_g, w_br_attn, w_br_hgrn, w_out,
           mix_norm_g, moe_norm_g, w_router_group, b_router_group, w_router_expert,
           b_router_expert, w1, w3, w2, final_norm_g):
    b, s, d = x.shape
    t = b * s
    depth = w_in.shape[0]
    assert depth == 1
    attn_w = (d // 128) * ATTN_HEAD_DIM
    kv_w = attn_w // Q_PER_KV
    hg_w = (d // 256) * HGRN_HEAD_DIM
    assert kv_w == N_KV_HEADS * ATTN_HEAD_DIM

    src = {}
    off = 0
    for name, width in (("qa", attn_w), ("k", kv_w), ("v", kv_w), ("qh", hg_w), ("f", hg_w),
                        ("i", hg_w), ("og", hg_w), ("g0", d), ("g1", d)):
        src[name] = (off, width)
        off += width
    order = ("g0", "g1", "qa", "qh", "f", "i", "og", "k", "v")
    dst = {}
    off = 0
    for name in order:
        dst[name] = off
        off += src[name][1]
    n_in = off
    l = 0
    tn_in = 512
    assert all(src[n][0] % tn_in == 0 for n in order if n != "v") and (2 * kv_w) % tn_in == 0
    col_blocks = []
    for n in order[:-1]:
        width = src[n][1] if n != "k" else 2 * kv_w
        col_blocks += [src[n][0] // tn_in + c for c in range(width // tn_in)]
    col_blocks = jnp.asarray(col_blocks, jnp.int32)
    w_bf = w_in[l]

    lb = jnp.cumsum(jax.nn.softmax(hgrn_lb_logits.astype(F32), axis=0), axis=0)[l].reshape(1, hg_w)

    x2 = x.reshape(t, d)
    tm_in = min(2048, t)
    proj = _in_proj(x2, mix_norm_g[l].reshape(1, d), w_bf, col_blocks, tm_in, tn_in)
    proj3 = proj.reshape(b, s, n_in)

    o_h = _hgrn(proj3, lb, hgrn_norm_g[l].reshape(1, HGRN_HEAD_DIM).astype(F32),
                dst["qh"], dst["f"], dst["i"], dst["og"], hg_w, min(256, s))

    w_r = jnp.zeros((d, ROUTE_LANES), F32)
    w_r = w_r.at[:, :N_GROUPS].set(w_router_group[l])
    w_r = w_r.at[:, EXPERT_LANE0:EXPERT_LANE0 + N_EXPERTS].set(w_router_expert[l])
    wr_hi = w_r.astype(BF16)
    wr_lo = (w_r - wr_hi.astype(F32)).astype(BF16)
    b_r = jnp.zeros((1, ROUTE_LANES), F32)
    b_r = b_r.at[0, :N_GROUPS].set(b_router_group[l])
    b_r = b_r.at[0, EXPERT_LANE0:EXPERT_LANE0 + N_EXPERTS].set(b_router_expert[l])

    tm = 2 * ATTN_BLOCK
    h, hn, route, plan, cnt = _attn_merge_route(
        attn_sinks[l].astype(F32), o_h.reshape(t, hg_w), proj, x2,
        w_br_attn[l], w_br_hgrn[l], w_out[l],
        moe_norm_g[l].reshape(1, d), wr_hi, wr_lo, b_r, dst, attn_w, kv_w, s, tm)

    bm = MOE_ROWS
    counts = cnt[0, EXPERT_LANE0:EXPERT_LANE0 + N_EXPERTS].astype(jnp.int32)
    padded = ((counts + bm - 1) // bm) * bm
    pad_end = jnp.cumsum(padded)
    pad_start = (pad_end - padded).astype(jnp.int32)
    n_blocks = (2 * t) // bm + N_EXPERTS
    n_used = pad_end[-1] // bm
    blk_ids = jnp.minimum(jnp.arange(n_blocks, dtype=jnp.int32), n_used - 1)
    block_expert = jnp.minimum(
        jnp.sum((pad_end[None, :] <= (blk_ids * bm)[:, None]).astype(jnp.int32), axis=1),
        N_EXPERTS - 1)
    meta = jnp.stack([n_used, n_used]).astype(jnp.int32)

    blk_all = jnp.arange(n_blocks, dtype=jnp.int32)
    fill = jnp.concatenate([
        jnp.maximum(pad_end - bm, 0).astype(jnp.int32),
        jnp.minimum(n_used + blk_all, n_blocks - 1) * bm])
    nfill = (N_EXPERTS + n_blocks - n_used).astype(jnp.int32).reshape(1)

    eid = jnp.arange(N_EXPERTS, dtype=jnp.int32)
    nonempty = counts > 0
    slot_e = ((jnp.cumsum(nonempty.astype(jnp.int32)) - 1) % 2).astype(jnp.int32)
    later = jnp.logical_and(nonempty[None, :], eid[None, :] > eid[:, None])
    nxt_e = jnp.min(jnp.where(later, eid[None, :], N_EXPERTS), axis=1)
    nxt_e = jnp.where(nxt_e == N_EXPERTS, -1, nxt_e).astype(jnp.int32)

    nt = t // tm
    dest = _dest_rows(plan, pad_start, min(8, nt))
    rows = _dispatch(hn, dest, fill, nfill, n_blocks * bm, min(4, nt), bm)
    y_rows = _experts(rows, block_expert, meta, slot_e, nxt_e, w1[l], w3[l], w2[l], bm)
    out = _combine(h, route, dest, y_rows, final_norm_g.reshape(1, d), tm)
    return out.reshape(b, s, d)
```

```python
import functools

import jax
import jax.numpy as jnp
from jax import lax
from jax.experimental import pallas as pl
from jax.experimental.pallas import tpu as pltpu

F32 = jnp.float32
BF16 = jnp.bfloat16

EPS = 1e-6
ATTN_HEAD_DIM = 64
Q_PER_KV = 4
N_KV_HEADS = 4
ATTN_BLOCK = 128
HGRN_HEAD_DIM = 128
HGRN_CHUNK = 64
N_GROUPS = 4
EXPERTS_PER_GROUP = 8
N_EXPERTS = N_GROUPS * EXPERTS_PER_GROUP
LANES = 128
ROUTE_LANES = LANES
EXPERT_LANE0 = N_GROUPS
MOE_ROWS = 256
WEIGHT_STAGE_ROWS = 256
PLAN_ROWS = 8
PLAN_E, PLAN_R = 0, 4
VMEM_LIMIT = 56 * 1024 * 1024
VMEM_LIMIT_IN_PROJ = 60 * 1024 * 1024


def _sigmoid(x):
    return 0.5 * jnp.tanh(0.5 * x) + 0.5


def _cparams(sem, vmem=VMEM_LIMIT):
    return pltpu.CompilerParams(dimension_semantics=sem, vmem_limit_bytes=vmem)


def _to_row_tiles(v):
    m, d = v.shape
    return v.reshape(m, d // LANES, LANES).astype(BF16)


def _from_row_tiles(p):
    m, c, l = p.shape
    return p.reshape(m, c * l)


def _inproj_kernel(cols_ref, x_ref, g_ref, w_ref, o_ref, xn_ref):
    del cols_ref

    @pl.when(pl.program_id(1) == 0)
    def _():
        x = x_ref[...]
        ms = jnp.mean(x * x, axis=-1, keepdims=True)
        xn_ref[...] = (x * lax.rsqrt(ms + EPS) * g_ref[...]).astype(BF16)

    o_ref[...] = jnp.dot(xn_ref[...], w_ref[...].astype(BF16),
                         preferred_element_type=F32).astype(o_ref.dtype)


def _in_proj(x2, g, w, col_blocks, tm, tn):
    t, d = x2.shape
    n = w.shape[1]
    grid_spec = pltpu.PrefetchScalarGridSpec(
        num_scalar_prefetch=1,
        grid=(t // tm, n // tn),
        in_specs=[
            pl.BlockSpec((tm, d), lambda i, j, cols: (i, 0)),
            pl.BlockSpec((1, d), lambda i, j, cols: (0, 0)),
            pl.BlockSpec((d, tn), lambda i, j, cols: (0, cols[j])),
        ],
        out_specs=pl.BlockSpec((tm, tn), lambda i, j, cols: (i, j)),
        scratch_shapes=[pltpu.VMEM((tm, d), BF16)],
    )
    return pl.pallas_call(
        _inproj_kernel,
        grid_spec=grid_spec,
        out_shape=jax.ShapeDtypeStruct((t, n), BF16),
        compiler_params=_cparams(("arbitrary", "arbitrary"), VMEM_LIMIT_IN_PROJ),
        name="in_proj",
    )(col_blocks, x2, g, w)


def _attn_scores(q_ref, rows, k_cur, k_prev):
    dh, g = ATTN_HEAD_DIM, Q_PER_KV
    nt = (((1,), (1,)), ((), ()))
    scores = []
    for h in range(N_KV_HEADS):
        q4 = jnp.concatenate(
            [q_ref[rows, (h * g + j) * dh:(h * g + j + 1) * dh] for j in range(g)], axis=0)
        scores.append(
            (lax.dot_general(q4, k_cur[:, h * dh:(h + 1) * dh], nt, preferred_element_type=F32),
             lax.dot_general(q4, k_prev[:, h * dh:(h + 1) * dh], nt, preferred_element_type=F32)))
    return scores


def _attn_finish(scores, v_cur, v_prev, sink_ref, prev_bias, o_ref, rows):
    blk, dh, g = ATTN_BLOCK, ATTN_HEAD_DIM, Q_PER_KV
    n_rows = g * blk
    qi = lax.broadcasted_iota(jnp.int32, (n_rows, blk), 0) % blk
    kj = lax.broadcasted_iota(jnp.int32, (n_rows, blk), 1)
    mask_cur = kj <= qi
    head_of_row = lax.broadcasted_iota(jnp.int32, (n_rows, 1), 0) // blk
    scale = dh ** -0.5
    for h in range(N_KV_HEADS):
        sc, sp = scores[h]
        s = jnp.where(mask_cur, sc, sp + prev_bias) * scale
        sink = jnp.zeros((n_rows, 1), F32)
        for j in range(g):
            sink = jnp.where(head_of_row == j, sink_ref[h * g + j], sink)
        m = jnp.maximum(jnp.max(s, axis=-1, keepdims=True), sink)
        p = jnp.exp(s - m)
        den = jnp.sum(p, axis=-1, keepdims=True) + jnp.exp(sink - m)
        p_cur = jnp.where(mask_cur, p, 0.0).astype(BF16)
        p_prev = jnp.where(mask_cur, 0.0, p).astype(BF16)
        o = (jnp.dot(p_cur, v_cur[:, h * dh:(h + 1) * dh], preferred_element_type=F32)
             + jnp.dot(p_prev, v_prev[:, h * dh:(h + 1) * dh], preferred_element_type=F32)) / den
        for j in range(g):
            o_ref[rows, (h * g + j) * dh:(h * g + j + 1) * dh] = (
                o[j * blk:(j + 1) * blk, :].astype(o_ref.dtype))


def _hgrn_kernel(q_ref, f_ref, i_ref, og_ref, lb_ref, gn_ref, o_ref,
                 st_ref, gh_s, f_s, b_s, qt_s, kt_s, qe_s, kd_s, dl_s, oi_s, u_s, a_s,
                 *, n_heads, n_chunks):
    c, dk = HGRN_CHUNK, HGRN_HEAD_DIM
    w = n_heads * dk

    @pl.when(pl.program_id(1) == 0)
    def _():
        st_ref[...] = jnp.zeros_like(st_ref)

    ti = lax.broadcasted_iota(jnp.int32, (c, c), 0)
    si = lax.broadcasted_iota(jnp.int32, (c, c), 1)
    causal = si <= ti
    tri = causal.astype(BF16)
    nt = (((1,), (1,)), ((), ()))
    tn = (((0,), (0,)), ((), ()))
    qscale = dk ** -0.5
    chunk_rows = [slice(ci * c, (ci + 1) * c) for ci in range(n_chunks)]
    head_cols = [slice(h * dk, (h + 1) * dk) for h in range(n_heads)]

    lb = lb_ref[...]
    c0 = 0.5 * (1.0 + lb)
    c1 = 0.5 * (1.0 - lb)
    tf = jnp.tanh(0.5 * f_ref[...].astype(F32))
    f_s[...] = c1 * (1.0 - tf)
    gl = jnp.log(c0 + c1 * tf)
    g_hi = gl.astype(BF16)
    gh_s[:, :w] = g_hi
    gh_s[:, w:] = (gl - g_hi.astype(F32)).astype(BF16)

    for rows in chunk_rows:
        bcat = jnp.dot(tri, gh_s[rows, :], preferred_element_type=F32)
        b_s[rows, :] = bcat[:, :w] + bcat[:, w:]

    for ci, rows in enumerate(chunk_rows):
        bc = b_s[rows, :]
        b_last = bc[c - 1:c, :]
        r = 0.5 * b_last
        qv = q_ref[rows, :].astype(F32)
        qt = (qv * (0.5 * qscale)) * (1.0 + jnp.tanh(0.5 * qv)) * jnp.exp(bc - r)
        kt = f_s[rows, :] * jnp.exp(r - bc)
        qt_s[rows, :] = qt.astype(BF16)
        kt_s[rows, :] = kt.astype(BF16)
        qe_s[rows, :] = (qt * jnp.exp(r)).astype(BF16)
        kd_s[rows, :] = (kt * jnp.exp(b_last - r)).astype(BF16)
        dl_s[ci:ci + 1, :] = jnp.exp(b_last)

    units = [(ci, rows, h, cols) for ci, rows in enumerate(chunk_rows)
             for h, cols in enumerate(head_cols)]
    for ci, rows, h, cols in units:
        a = lax.dot_general(qt_s[rows, cols], kt_s[rows, cols], nt, preferred_element_type=F32)
        a_s[ci * n_heads + h] = jnp.where(causal, a, 0.0).astype(BF16)
    for ci, rows, h, cols in units:
        u_s[ci, h] = lax.dot_general(
            i_ref[rows, cols], kd_s[rows, cols], tn, preferred_element_type=F32)
    for ci, rows, h, cols in units:
        oi_s[rows, cols] = jnp.dot(
            a_s[ci * n_heads + h], i_ref[rows, cols], preferred_element_type=F32)

    gn_half = 0.5 * gn_ref[...]
    for ci, rows in enumerate(chunk_rows):
        for h, cols in enumerate(head_cols):
            st = st_ref[h]
            o = oi_s[rows, cols] + lax.dot_general(
                qe_s[rows, cols], st.astype(BF16), nt, preferred_element_type=F32)
            st_ref[h] = st * dl_s[ci:ci + 1, cols] + u_s[ci, h]
            ms = jnp.mean(o * o, axis=-1, keepdims=True)
            ogv = og_ref[rows, cols].astype(F32)
            o_ref[rows, cols] = (
                (o * lax.rsqrt(ms + EPS) * gn_half)
                * (ogv * (1.0 + jnp.tanh(0.5 * ogv)))).astype(o_ref.dtype)


def _hgrn(proj3, lb, gn, off_q, off_f, off_i, off_og, width, ts):
    b, s, _ = proj3.shape
    n_heads = width // HGRN_HEAD_DIM
    n_chunks = ts // HGRN_CHUNK
    dk = HGRN_HEAD_DIM
    spec = lambda off: pl.BlockSpec((None, ts, width), lambda i, t: (i, t, off // width))
    return pl.pallas_call(
        functools.partial(_hgrn_kernel, n_heads=n_heads, n_chunks=n_chunks),
        grid=(b, s // ts),
        in_specs=[
            spec(off_q), spec(off_f), spec(off_i), spec(off_og),
            pl.BlockSpec((1, width), lambda i, t: (0, 0)),
            pl.BlockSpec((1, dk), lambda i, t: (0, 0)),
        ],
        out_specs=pl.BlockSpec((None, ts, width), lambda i, t: (i, t, 0)),
        out_shape=jax.ShapeDtypeStruct((b, s, width), BF16),
        scratch_shapes=[
            pltpu.VMEM((n_heads, dk, dk), F32),
            pltpu.VMEM((ts, 2 * width), BF16),
            pltpu.VMEM((ts, width), F32),
            pltpu.VMEM((ts, width), F32),
            pltpu.VMEM((ts, width), BF16),
            pltpu.VMEM((ts, width), BF16),
            pltpu.VMEM((ts, width), BF16),
            pltpu.VMEM((ts, width), BF16),
            pltpu.VMEM((max(n_chunks, 8), width), F32),
            pltpu.VMEM((ts, width), F32),
            pltpu.VMEM((n_chunks, n_heads, dk, dk), F32),
            pltpu.VMEM((n_chunks * n_heads, HGRN_CHUNK, HGRN_CHUNK), BF16),
        ],
        compiler_params=_cparams(("arbitrary", "arbitrary")),
        name="hgrn2",
    )(proj3, proj3, proj3, proj3, lb, gn)


def _merge_kernel(sink_ref, q_ref, kc_ref, kp_ref, vc_ref, vp_ref,
                  oh_ref, g0_ref, g1_ref, x_ref, wa_hbm, wh_hbm, wo_hbm, gm_ref,
                  wrh_ref, wrl_ref, br_ref, h_ref, hn_ref, route_ref, plan_ref, cnt_ref,
                  oa_s, hn_s, wa_ref, wh_ref, wo_ref, stage, stage_sem, *, tiles_per_seq):
    tm, d = x_ref.shape
    half = d // 2
    halves = (slice(0, half), slice(half, d))
    step = pl.program_id(0)
    blk = ATTN_BLOCK
    assert tm == 2 * blk

    @pl.when(step == 0)
    def _():
        cnt_ref[...] = jnp.zeros_like(cnt_ref)
        oa_s[...] = jnp.zeros_like(oa_s)
        hn_s[...] = jnp.zeros_like(hn_s)
        rows = stage.shape[1]
        chunks = [(src, dst, r0) for src, dst in ((wa_hbm, wa_ref), (wh_hbm, wh_ref), (wo_hbm, wo_ref))
                  for r0 in range(0, src.shape[0], rows)]

        def chunk_copy(n):
            src, _, r0 = chunks[n]
            return pltpu.make_async_copy(
                src.at[pl.ds(r0, rows)], stage.at[n % 2], stage_sem.at[n % 2])

        chunk_copy(0).start()
        for n, (_, dst, r0) in enumerate(chunks):
            if n + 1 < len(chunks):
                chunk_copy(n + 1).start()
            chunk_copy(n).wait()
            dst[r0:r0 + rows, :] = stage[n % 2].astype(BF16)

    q_blocks = (slice(0, blk), slice(blk, tm))
    k_blocks = [kc_ref[rs, :] for rs in q_blocks]
    v_blocks = [vc_ref[rs, :] for rs in q_blocks]
    scores = [_attn_scores(q_ref, q_blocks[0], k_blocks[0], kp_ref[...]),
              _attn_scores(q_ref, q_blocks[1], k_blocks[1], k_blocks[0])]
    attn_tile = jnp.minimum(step, pl.num_programs(0) - 3)
    prev_bias = jnp.where(attn_tile % tiles_per_seq == 0, -jnp.inf, 0.0)

    hn_prev = hn_s[...]
    hn_hi = hn_prev.astype(BF16)
    hn_lo = (hn_prev - hn_hi.astype(F32)).astype(BF16)

    oa = oa_s[...]
    ya = [jnp.dot(oa, wa_ref[:, cs], preferred_element_type=F32) for cs in halves]
    yh = [jnp.dot(oh_ref[...], wh_ref[:, cs], preferred_element_type=F32) for cs in halves]

    logits = (jnp.dot(hn_hi, wrh_ref[...], preferred_element_type=F32)
              + jnp.dot(hn_hi, wrl_ref[...], preferred_element_type=F32)
              + jnp.dot(hn_lo, wrh_ref[...], preferred_element_type=F32)
              + br_ref[...])

    _attn_finish(scores[0], v_blocks[0], vp_ref[...], sink_ref, prev_bias, oa_s, q_blocks[0])
    _attn_finish(scores[1], v_blocks[1], v_blocks[0], sink_ref, 0.0, oa_s, q_blocks[1])

    lane = lax.broadcasted_iota(jnp.int32, (tm, ROUTE_LANES), 1)
    neg = -jnp.inf

    def first_argmax(v):
        m = jnp.max(v, axis=-1, keepdims=True)
        idx = jnp.min(jnp.where(v == m, lane, ROUTE_LANES), axis=-1, keepdims=True)
        return m, idx

    is_group = lane < N_GROUPS
    gmax, gidx = first_argmax(jnp.where(is_group, logits, neg))
    p_sel = 1.0 / jnp.sum(jnp.where(is_group, jnp.exp(logits - gmax), 0.0), axis=-1, keepdims=True)
    eidx = lane - EXPERT_LANE0
    in_group = jnp.logical_and(
        jnp.logical_and(eidx >= 0, eidx < N_EXPERTS), (eidx // EXPERTS_PER_GROUP) == gidx)
    el = jnp.where(in_group, logits, neg)
    m1, i1 = first_argmax(el)
    m2, i2 = first_argmax(jnp.where(lane == i1, neg, el))
    t = jnp.exp(m2 - m1)
    w1 = p_sel / (1.0 + t)
    w2 = p_sel * t / (1.0 + t)

    sel1 = lane == i1
    sel2 = lane == i2
    onehot = jnp.logical_or(sel1, sel2).astype(BF16)
    ri = lax.broadcasted_iota(jnp.int32, (tm, tm), 0)
    ci = lax.broadcasted_iota(jnp.int32, (tm, tm), 1)
    before = (ci < ri).astype(BF16)

    h = x_ref[...]
    for k, cs in enumerate(halves):
        merged = (_sigmoid(g0_ref[:, cs].astype(F32)) * ya[k]
                  + _sigmoid(g1_ref[:, cs].astype(F32)) * yh[k]).astype(BF16)
        h = h + jnp.dot(merged, wo_ref[cs, :], preferred_element_type=F32)
        if k == 0:
            cum = jnp.dot(before, onehot, preferred_element_type=F32) + cnt_ref[...]


    r1 = jnp.sum(jnp.where(sel1, cum, 0.0), axis=-1, keepdims=True)
    r2 = jnp.sum(jnp.where(sel2, cum, 0.0), axis=-1, keepdims=True)
    routed = jnp.where(step > 1, 1.0, 0.0)
    cnt_ref[...] += routed * jnp.sum(onehot.astype(F32), axis=0, keepdims=True)

    e1 = (i1 - EXPERT_LANE0).astype(F32)
    e2 = (i2 - EXPERT_LANE0).astype(F32)
    out = jnp.zeros((tm, ROUTE_LANES), F32)
    for k, v in enumerate((e1, e2, w1, w2, r1, r2)):
        out = jnp.where(lane == k, v, out)
    route_ref[...] = out
    plan_ref[...] = out.T[:PLAN_ROWS, :].astype(jnp.int32)

    h_ref[...] = h
    ms = jnp.mean(h * h, axis=-1, keepdims=True)
    hn = h * lax.rsqrt(ms + EPS) * gm_ref[...]
    hn_ref[...] = _to_row_tiles(hn)
    hn_s[...] = hn


def _attn_merge_route(sinks, o_h, proj, x2, wa, wh, wo, gm, wr_hi, wr_lo, br,
                      off, attn_w, kv_w, seq, tm):
    t, d = x2.shape
    nt = t // tm
    hw = o_h.shape[1]
    blk = ATTN_BLOCK
    const = lambda shape: pl.BlockSpec(shape, lambda i: (0, 0), pipeline_mode=pl.Buffered(1))
    hbm = pl.BlockSpec(memory_space=pl.ANY)
    assert attn_w % WEIGHT_STAGE_ROWS == 0 and hw % WEIGHT_STAGE_ROWS == 0
    att = lambda i: jnp.minimum(i, nt - 1)
    cur = lambda i: jnp.clip(i - 1, 0, nt - 1)
    prv = lambda i: jnp.clip(i - 2, 0, nt - 1)
    row = lambda w: pl.BlockSpec((tm, w), lambda i: (cur(i), 0))
    kv_col = {"k": off["k"] // kv_w, "v": off["v"] // kv_w}
    kv_cur = lambda n: pl.BlockSpec((tm, kv_w), lambda i: (att(i), kv_col[n]))
    kv_prev = lambda n: pl.BlockSpec(
        (blk, kv_w), lambda i: (jnp.maximum(att(i) * (tm // blk) - 1, 0), kv_col[n]))
    return pl.pallas_call(
        functools.partial(_merge_kernel, tiles_per_seq=seq // tm),
        grid=(nt + 2,),
        in_specs=[
            pl.BlockSpec(memory_space=pltpu.SMEM),
            pl.BlockSpec((tm, attn_w), lambda i: (att(i), off["qa"] // attn_w)),
            kv_cur("k"), kv_prev("k"), kv_cur("v"), kv_prev("v"),
            row(hw),
            pl.BlockSpec((tm, d), lambda i: (cur(i), off["g0"] // d)),
            pl.BlockSpec((tm, d), lambda i: (cur(i), off["g1"] // d)),
            row(d),
            hbm, hbm, hbm, const((1, d)),
            const((d, ROUTE_LANES)), const((d, ROUTE_LANES)), const((1, ROUTE_LANES)),
        ],
        out_specs=[row(d), pl.BlockSpec((tm, d // LANES, LANES), lambda i: (cur(i), 0, 0)),
                   pl.BlockSpec((tm, ROUTE_LANES), lambda i: (prv(i), 0)),
                   pl.BlockSpec((None, PLAN_ROWS, tm), lambda i: (prv(i), 0, 0)),
                   pl.BlockSpec((1, ROUTE_LANES), lambda i: (0, 0))],
        out_shape=[
            jax.ShapeDtypeStruct((t, d), F32),
            jax.ShapeDtypeStruct((t, d // LANES, LANES), BF16),
            jax.ShapeDtypeStruct((t, ROUTE_LANES), F32),
            jax.ShapeDtypeStruct((nt, PLAN_ROWS, tm), jnp.int32),
            jax.ShapeDtypeStruct((1, ROUTE_LANES), F32),
        ],
        scratch_shapes=[
            pltpu.VMEM((tm, attn_w), BF16), pltpu.VMEM((tm, d), F32),
            pltpu.VMEM((attn_w, d), BF16), pltpu.VMEM((hw, d), BF16), pltpu.VMEM((d, d), BF16),
            pltpu.VMEM((2, WEIGHT_STAGE_ROWS, d), F32), pltpu.SemaphoreType.DMA((2,)),
        ],
        compiler_params=_cparams(("arbitrary",)),
        name="attn_merge_route",
    )(sinks, proj, proj, proj, proj, proj, o_h, proj, proj, x2, wa, wh, wo, gm, wr_hi, wr_lo, br)


def _dest_kernel(pst_ref, plan_ref, dest_ref):
    e = plan_ref[:, PLAN_E:PLAN_E + 2, :]
    start = jnp.zeros(e.shape, jnp.int32)
    for j in range(N_EXPERTS):
        start = jnp.where(e == j, pst_ref[j], start)
    dest_ref[...] = start + plan_ref[:, PLAN_R:PLAN_R + 2, :]


def _dest_rows(plan, pst, tiles):
    nt, _, tm = plan.shape
    grid_spec = pltpu.PrefetchScalarGridSpec(
        num_scalar_prefetch=1,
        grid=(nt // tiles,),
        in_specs=[pl.BlockSpec((tiles, PLAN_ROWS, tm), lambda i, *_: (i, 0, 0))],
        out_specs=pl.BlockSpec((tiles, 2, tm), lambda i, *_: (i, 0, 0)),
    )
    return pl.pallas_call(
        _dest_kernel,
        grid_spec=grid_spec,
        out_shape=jax.ShapeDtypeStruct((nt, 2, tm), jnp.int32),
        compiler_params=_cparams(("arbitrary",)),
        name="moe_dest",
    )(pst, plan)


ISSUE_UNROLL = 8


def _dispatch_kernel(fill_ref, nfill_ref, dest_ref, hn_ref, rows_ref, zbuf, sem):
    tm = hn_ref.shape[0]
    bm = zbuf.shape[0]

    @pl.when(pl.program_id(0) == 0)
    def _():
        zbuf[...] = jnp.zeros_like(zbuf)

        def fill(j, carry):
            start = pl.multiple_of(fill_ref[j], bm)
            pltpu.make_async_copy(zbuf, rows_ref.at[pl.ds(start, bm)], sem).start()
            return carry

        def drain(j, carry):
            pltpu.make_async_copy(zbuf, rows_ref.at[pl.ds(0, bm)], sem).wait()
            return carry

        lax.fori_loop(0, nfill_ref[0], fill, 0)
        lax.fori_loop(0, nfill_ref[0], drain, 0)

    sub, _, tmd = dest_ref.shape

    for j in range(sub):
        def issue(r, carry, j=j):
            for k in range(2):
                pltpu.make_async_copy(
                    hn_ref.at[j * tmd + r], rows_ref.at[dest_ref[j, k, r]], sem
                ).start(priority=k)
            return carry

        lax.fori_loop(0, tmd, issue, 0, unroll=ISSUE_UNROLL)
    for _ in range(2):
        pltpu.make_async_copy(hn_ref, rows_ref.at[pl.ds(0, tm)], sem).wait()


def _dispatch(hn, dest, fill, nfill, n_rows, sub, bm):
    t, c, l = hn.shape
    nt, _, tmd = dest.shape
    tm = sub * tmd
    grid_spec = pltpu.PrefetchScalarGridSpec(
        num_scalar_prefetch=2,
        grid=(nt // sub,),
        in_specs=[
            pl.BlockSpec((sub, 2, tmd), lambda i, *_: (i, 0, 0), memory_space=pltpu.SMEM),
            pl.BlockSpec((tm, c, l), lambda i, *_: (i, 0, 0)),
        ],
        out_specs=pl.BlockSpec(memory_space=pl.ANY),
        scratch_shapes=[pltpu.VMEM((bm, c, l), hn.dtype), pltpu.SemaphoreType.DMA],
    )
    return pl.pallas_call(
        _dispatch_kernel,
        grid_spec=grid_spec,
        out_shape=jax.ShapeDtypeStruct((n_rows, c, l), hn.dtype),
        compiler_params=_cparams(("arbitrary",)),
        name="moe_dispatch",
    )(fill, nfill, dest, hn)


def _expert_kernel(be_ref, meta_ref, slot_ref, nxt_ref, x_ref, w1_hbm, w3_hbm, w2_hbm, y_ref,
                   wf1, wf3, wf2, w1b, w3b, w2b, sems):
    i = pl.program_id(0)
    e = be_ref[i]
    new_expert = jnp.logical_or(i == 0, e != be_ref[jnp.maximum(i - 1, 0)])

    def weight_copies(expert, slot):
        return [pltpu.make_async_copy(src.at[expert], dst.at[slot], sems.at[slot])
                for src, dst in ((w1_hbm, wf1), (w3_hbm, wf3), (w2_hbm, wf2))]

    @pl.when(jnp.logical_and(i < meta_ref[0], new_expert))
    def _():
        slot = slot_ref[e]
        nxt = nxt_ref[e]

        @pl.when(i == 0)
        def _():
            for c in weight_copies(e, slot):
                c.start()

        @pl.when(nxt >= 0)
        def _():
            for c in weight_copies(nxt, 1 - slot):
                c.start(priority=1)

        for c in weight_copies(e, slot):
            c.wait()
        w1b[...] = wf1[slot].astype(BF16)
        w3b[...] = wf3[slot].astype(BF16)
        w2b[...] = wf2[slot].astype(BF16)

    @pl.when(i < meta_ref[0])
    def _():
        bm = x_ref.shape[0]
        parts = (slice(0, bm // 2), slice(bm // 2, bm))
        xs = [_from_row_tiles(x_ref[rs]) for rs in parts]
        ups = [(jnp.dot(x, w1b[...], preferred_element_type=F32),
                jnp.dot(x, w3b[...], preferred_element_type=F32)) for x in xs]
        for rs, (h1, h3) in zip(parts, ups):
            hb = (h1 * _sigmoid(h1) * h3).astype(BF16)
            y_ref[rs] = _to_row_tiles(jnp.dot(hb, w2b[...], preferred_element_type=F32))

    @pl.when(i >= meta_ref[0])
    def _():
        y_ref[...] = jnp.zeros_like(y_ref)


def _experts(rows, block_expert, meta, slot_e, nxt_e, w1, w3, w2, bm):
    n_rows, c, l = rows.shape
    d = c * l
    de = w1.shape[-1]
    n_blocks = n_rows // bm
    hbm = pl.BlockSpec(memory_space=pl.ANY)
    grid_spec = pltpu.PrefetchScalarGridSpec(
        num_scalar_prefetch=4,
        grid=(n_blocks,),
        in_specs=[
            pl.BlockSpec((bm, c, l), lambda i, be, meta, *_: (jnp.minimum(i, meta[0] - 1), 0, 0)),
            hbm, hbm, hbm,
        ],
        out_specs=pl.BlockSpec((bm, c, l), lambda i, *_: (i, 0, 0)),
        scratch_shapes=[
            pltpu.VMEM((2, d, de), w1.dtype), pltpu.VMEM((2, d, de), w3.dtype),
            pltpu.VMEM((2, de, d), w2.dtype),
            pltpu.VMEM((d, de), BF16), pltpu.VMEM((d, de), BF16), pltpu.VMEM((de, d), BF16),
            pltpu.SemaphoreType.DMA((2,)),
        ],
    )
    return pl.pallas_call(
        _expert_kernel,
        grid_spec=grid_spec,
        out_shape=jax.ShapeDtypeStruct((n_rows, c, l), BF16),
        compiler_params=_cparams(("arbitrary",)),
        name="moe_experts",
    )(block_expert, meta, slot_e, nxt_e, rows, w1, w3, w2)


def _combine_kernel(dcur_ref, dnext_ref, h_ref, route_ref, g_ref, y_ref, o_ref, ybuf, sems):
    i = pl.program_id(0)
    nt = pl.num_programs(0)
    tm, d = h_ref.shape
    slot = i % 2

    def gather(dest_ref, s):
        def issue(r, carry):
            for k in range(2):
                pltpu.make_async_copy(
                    y_ref.at[dest_ref[k, r]], ybuf.at[s, k, r], sems.at[s]).start(priority=k)
            return carry
        lax.fori_loop(0, tm, issue, 0, unroll=ISSUE_UNROLL)

    @pl.when(i == 0)
    def _():
        gather(dcur_ref, slot)

    @pl.when(i + 1 < nt)
    def _():
        gather(dnext_ref, 1 - slot)

    for k in range(2):
        pltpu.make_async_copy(y_ref.at[pl.ds(0, tm)], ybuf.at[slot, k], sems.at[slot]).wait()

    route = route_ref[...]
    w1 = route[:, 2:3]
    w2 = route[:, 3:4]
    ya = _from_row_tiles(ybuf[slot, 0]).astype(F32)
    yb = _from_row_tiles(ybuf[slot, 1]).astype(F32)
    h = h_ref[...] + w1 * ya + w2 * yb
    ms = jnp.mean(h * h, axis=-1, keepdims=True)
    o_ref[...] = h * lax.rsqrt(ms + EPS) * g_ref[...]


def _combine(h, route, dest, y_rows, g, tm):
    t, d = h.shape
    nt = t // tm
    dest_spec = lambda f: pl.BlockSpec((None, 2, tm), f, memory_space=pltpu.SMEM)
    return pl.pallas_call(
        _combine_kernel,
        grid=(nt,),
        in_specs=[
            dest_spec(lambda i: (i, 0, 0)),
            dest_spec(lambda i: (jnp.minimum(i + 1, nt - 1), 0, 0)),
            pl.BlockSpec((tm, d), lambda i: (i, 0)),
            pl.BlockSpec((tm, ROUTE_LANES), lambda i: (i, 0)),
            pl.BlockSpec((1, d), lambda i: (0, 0)),
            pl.BlockSpec(memory_space=pl.ANY),
        ],
        out_specs=pl.BlockSpec((tm, d), lambda i: (i, 0)),
        out_shape=jax.ShapeDtypeStruct((t, d), F32),
        scratch_shapes=[pltpu.VMEM((2, 2, tm, d // LANES, LANES), BF16),
                        pltpu.SemaphoreType.DMA((2,))],
        compiler_params=_cparams(("arbitrary",)),
        name="moe_combine",
    )(dest, dest, h, route, g, y_rows)


def kernel(x, w_in, attn_sinks, hgrn_lb_logits, hgrn_norm_g, w_br_attn, w_br_hgrn, w_out,
           mix_norm_g, moe_norm_g, w_router_group, b_router_group, w_router_expert,
           b_router_expert, w1, w3, w2, final_norm_g):
    b, s, d = x.shape
    t = b * s
    depth = w_in.shape[0]
    assert depth == 1
    attn_w = (d // 128) * ATTN_HEAD_DIM
    kv_w = attn_w // Q_PER_KV
    hg_w = (d // 256) * HGRN_HEAD_DIM
    assert kv_w == N_KV_HEADS * ATTN_HEAD_DIM

    src = {}
    off = 0
    for name, width in (("qa", attn_w), ("k", kv_w), ("v", kv_w), ("qh", hg_w), ("f", hg_w),
                        ("i", hg_w), ("og", hg_w), ("g0", d), ("g1", d)):
        src[name] = (off, width)
        off += width
    order = ("g0", "g1", "qa", "qh", "f", "i", "og", "k", "v")
    dst = {}
    off = 0
    for name in order:
        dst[name] = off
        off += src[name][1]
    n_in = off
    l = 0
    tn_in = 512
    assert all(src[n][0] % tn_in == 0 for n in order if n != "v") and (2 * kv_w) % tn_in == 0
    col_blocks = []
    for n in order[:-1]:
        width = src[n][1] if n != "k" else 2 * kv_w
        col_blocks += [src[n][0] // tn_in + c for c in range(width // tn_in)]
    col_blocks = jnp.asarray(col_blocks, jnp.int32)
    w_bf = w_in[l]

    lb = jnp.cumsum(jax.nn.softmax(hgrn_lb_logits.astype(F32), axis=0), axis=0)[l].reshape(1, hg_w)

    x2 = x.reshape(t, d)
    tm_in = min(2048, t)
    proj = _in_proj(x2, mix_norm_g[l].reshape(1, d), w_bf, col_blocks, tm_in, tn_in)
    proj3 = proj.reshape(b, s, n_in)

    o_h = _hgrn(proj3, lb, hgrn_norm_g[l].reshape(1, HGRN_HEAD_DIM).astype(F32),
                dst["qh"], dst["f"], dst["i"], dst["og"], hg_w, min(512, s))

    w_r = jnp.zeros((d, ROUTE_LANES), F32)
    w_r = w_r.at[:, :N_GROUPS].set(w_router_group[l])
    w_r = w_r.at[:, EXPERT_LANE0:EXPERT_LANE0 + N_EXPERTS].set(w_router_expert[l])
    wr_hi = w_r.astype(BF16)
    wr_lo = (w_r - wr_hi.astype(F32)).astype(BF16)
    b_r = jnp.zeros((1, ROUTE_LANES), F32)
    b_r = b_r.at[0, :N_GROUPS].set(b_router_group[l])
    b_r = b_r.at[0, EXPERT_LANE0:EXPERT_LANE0 + N_EXPERTS].set(b_router_expert[l])

    tm = 2 * ATTN_BLOCK
    h, hn, route, plan, cnt = _attn_merge_route(
        attn_sinks[l].astype(F32), o_h.reshape(t, hg_w), proj, x2,
        w_br_attn[l], w_br_hgrn[l], w_out[l],
        moe_norm_g[l].reshape(1, d), wr_hi, wr_lo, b_r, dst, attn_w, kv_w, s, tm)

    bm = MOE_ROWS
    counts = cnt[0, EXPERT_LANE0:EXPERT_LANE0 + N_EXPERTS].astype(jnp.int32)
    padded = ((counts + bm - 1) // bm) * bm
    pad_end = jnp.cumsum(padded)
    pad_start = (pad_end - padded).astype(jnp.int32)
    n_blocks = (2 * t) // bm + N_EXPERTS
    n_used = pad_end[-1] // bm
    blk_ids = jnp.minimum(jnp.arange(n_blocks, dtype=jnp.int32), n_used - 1)
    block_expert = jnp.minimum(
        jnp.sum((pad_end[None, :] <= (blk_ids * bm)[:, None]).astype(jnp.int32), axis=1),
        N_EXPERTS - 1)
    meta = jnp.stack([n_used, n_used]).astype(jnp.int32)

    blk_all = jnp.arange(n_blocks, dtype=jnp.int32)
    fill = jnp.concatenate([
        jnp.maximum(pad_end - bm, 0).astype(jnp.int32),
        jnp.minimum(n_used + blk_all, n_blocks - 1) * bm])
    nfill = (N_EXPERTS + n_blocks - n_used).astype(jnp.int32).reshape(1)

    eid = jnp.arange(N_EXPERTS, dtype=jnp.int32)
    nonempty = counts > 0
    slot_e = ((jnp.cumsum(nonempty.astype(jnp.int32)) - 1) % 2).astype(jnp.int32)
    later = jnp.logical_and(nonempty[None, :], eid[None, :] > eid[:, None])
    nxt_e = jnp.min(jnp.where(later, eid[None, :], N_EXPERTS), axis=1)
    nxt_e = jnp.where(nxt_e == N_EXPERTS, -1, nxt_e).astype(jnp.int32)

    nt = t // tm
    dest = _dest_rows(plan, pad_start, min(8, nt))
    rows = _dispatch(hn, dest, fill, nfill, n_blocks * bm, min(4, nt), bm)
    y_rows = _experts(rows, block_expert, meta, slot_e, nxt_e, w1[l], w3[l], w2[l], bm)
    out = _combine(h, route, dest, y_rows, final_norm_g.reshape(1, d), tm)
    return out.reshape(b, s, d)
```

```python
import functools

import jax
import jax.numpy as jnp
from jax import lax
from jax.experimental import pallas as pl
from jax.experimental.pallas import tpu as pltpu

F32 = jnp.float32
BF16 = jnp.bfloat16

EPS = 1e-6
ATTN_HEAD_DIM = 64
Q_PER_KV = 4
N_KV_HEADS = 4
ATTN_BLOCK = 128
HGRN_HEAD_DIM = 128
HGRN_CHUNK = 64
N_GROUPS = 4
EXPERTS_PER_GROUP = 8
N_EXPERTS = N_GROUPS * EXPERTS_PER_GROUP
LANES = 128
ROUTE_LANES = LANES
EXPERT_LANE0 = N_GROUPS
MOE_ROWS = 256
WEIGHT_STAGE_ROWS = 256
PLAN_ROWS = 8
PLAN_E, PLAN_R = 0, 4
VMEM_LIMIT = 56 * 1024 * 1024
VMEM_LIMIT_IN_PROJ = 60 * 1024 * 1024


def _sigmoid(x):
    return 0.5 * jnp.tanh(0.5 * x) + 0.5


def _cparams(sem, vmem=VMEM_LIMIT):
    return pltpu.CompilerParams(dimension_semantics=sem, vmem_limit_bytes=vmem)


def _to_row_tiles(v):
    m, d = v.shape
    return v.reshape(m, d // LANES, LANES).astype(BF16)


def _from_row_tiles(p):
    m, c, l = p.shape
    return p.reshape(m, c * l)


def _inproj_kernel(cols_ref, x_ref, g_ref, w_ref, o_ref, xn_ref):
    del cols_ref

    @pl.when(pl.program_id(1) == 0)
    def _():
        x = x_ref[...]
        ms = jnp.mean(x * x, axis=-1, keepdims=True)
        xn_ref[...] = (x * lax.rsqrt(ms + EPS) * g_ref[...]).astype(BF16)

    o_ref[...] = jnp.dot(xn_ref[...], w_ref[...].astype(BF16),
                         preferred_element_type=F32).astype(o_ref.dtype)


def _in_proj(x2, g, w, col_blocks, tm, tn):
    t, d = x2.shape
    n = w.shape[1]
    grid_spec = pltpu.PrefetchScalarGridSpec(
        num_scalar_prefetch=1,
        grid=(t // tm, n // tn),
        in_specs=[
            pl.BlockSpec((tm, d), lambda i, j, cols: (i, 0)),
            pl.BlockSpec((1, d), lambda i, j, cols: (0, 0)),
            pl.BlockSpec((d, tn), lambda i, j, cols: (0, cols[j])),
        ],
        out_specs=pl.BlockSpec((tm, tn), lambda i, j, cols: (i, j)),
        scratch_shapes=[pltpu.VMEM((tm, d), BF16)],
    )
    return pl.pallas_call(
        _inproj_kernel,
        grid_spec=grid_spec,
        out_shape=jax.ShapeDtypeStruct((t, n), BF16),
        compiler_params=_cparams(("arbitrary", "arbitrary"), VMEM_LIMIT_IN_PROJ),
        name="in_proj",
    )(col_blocks, x2, g, w)


def _attn_scores(q_ref, rows, k_cur, k_prev):
    dh, g = ATTN_HEAD_DIM, Q_PER_KV
    nt = (((1,), (1,)), ((), ()))
    scores = []
    for h in range(N_KV_HEADS):
        q4 = jnp.concatenate(
            [q_ref[rows, (h * g + j) * dh:(h * g + j + 1) * dh] for j in range(g)], axis=0)
        scores.append(
            (lax.dot_general(q4, k_cur[:, h * dh:(h + 1) * dh], nt, preferred_element_type=F32),
             lax.dot_general(q4, k_prev[:, h * dh:(h + 1) * dh], nt, preferred_element_type=F32)))
    return scores


def _attn_finish(scores, v_cur, v_prev, sink_ref, prev_bias, o_ref, rows):
    blk, dh, g = ATTN_BLOCK, ATTN_HEAD_DIM, Q_PER_KV
    n_rows = g * blk
    qi = lax.broadcasted_iota(jnp.int32, (n_rows, blk), 0) % blk
    kj = lax.broadcasted_iota(jnp.int32, (n_rows, blk), 1)
    mask_cur = kj <= qi
    head_of_row = lax.broadcasted_iota(jnp.int32, (n_rows, 1), 0) // blk
    scale = dh ** -0.5
    for h in range(N_KV_HEADS):
        sc, sp = scores[h]
        s = jnp.where(mask_cur, sc, sp + prev_bias) * scale
        sink = jnp.zeros((n_rows, 1), F32)
        for j in range(g):
            sink = jnp.where(head_of_row == j, sink_ref[h * g + j], sink)
        m = jnp.maximum(jnp.max(s, axis=-1, keepdims=True), sink)
        p = jnp.exp(s - m)
        den = jnp.sum(p, axis=-1, keepdims=True) + jnp.exp(sink - m)
        p_cur = jnp.where(mask_cur, p, 0.0).astype(BF16)
        p_prev = jnp.where(mask_cur, 0.0, p).astype(BF16)
        o = (jnp.dot(p_cur, v_cur[:, h * dh:(h + 1) * dh], preferred_element_type=F32)
             + jnp.dot(p_prev, v_prev[:, h * dh:(h + 1) * dh], preferred_element_type=F32)) / den
        for j in range(g):
            o_ref[rows, (h * g + j) * dh:(h * g + j + 1) * dh] = (
                o[j * blk:(j + 1) * blk, :].astype(o_ref.dtype))


def _hgrn_kernel(q_ref, f_ref, i_ref, og_ref, lb_ref, gn_ref, o_ref,
                 st_ref, gh_s, f_s, b_s, qt_s, kt_s, qe_s, kd_s, dl_s, oi_s, u_s, a_s,
                 *, n_heads, n_chunks):
    c, dk = HGRN_CHUNK, HGRN_HEAD_DIM
    w = n_heads * dk

    @pl.when(pl.program_id(1) == 0)
    def _():
        st_ref[...] = jnp.zeros_like(st_ref)

    ti = lax.broadcasted_iota(jnp.int32, (c, c), 0)
    si = lax.broadcasted_iota(jnp.int32, (c, c), 1)
    causal = si <= ti
    tri = causal.astype(BF16)
    nt = (((1,), (1,)), ((), ()))
    tn = (((0,), (0,)), ((), ()))
    qscale = dk ** -0.5
    chunk_rows = [slice(ci * c, (ci + 1) * c) for ci in range(n_chunks)]
    head_cols = [slice(h * dk, (h + 1) * dk) for h in range(n_heads)]

    lb = lb_ref[...]
    c0 = 0.5 * (1.0 + lb)
    c1 = 0.5 * (1.0 - lb)
    tf = jnp.tanh(0.5 * f_ref[...].astype(F32))
    f_s[...] = c1 * (1.0 - tf)
    gl = jnp.log(c0 + c1 * tf)
    g_hi = gl.astype(BF16)
    gh_s[:, :w] = g_hi
    gh_s[:, w:] = (gl - g_hi.astype(F32)).astype(BF16)

    for rows in chunk_rows:
        bcat = jnp.dot(tri, gh_s[rows, :], preferred_element_type=F32)
        b_s[rows, :] = bcat[:, :w] + bcat[:, w:]

    for ci, rows in enumerate(chunk_rows):
        bc = b_s[rows, :]
        b_last = bc[c - 1:c, :]
        r = 0.5 * b_last
        qv = q_ref[rows, :].astype(F32)
        qt = (qv * (0.5 * qscale)) * (1.0 + jnp.tanh(0.5 * qv)) * jnp.exp(bc - r)
        kt = f_s[rows, :] * jnp.exp(r - bc)
        qt_s[rows, :] = qt.astype(BF16)
        kt_s[rows, :] = kt.astype(BF16)
        qe_s[rows, :] = (qt * jnp.exp(r)).astype(BF16)
        kd_s[rows, :] = (kt * jnp.exp(b_last - r)).astype(BF16)
        dl_s[ci:ci + 1, :] = jnp.exp(b_last)

    units = [(ci, rows, h, cols) for ci, rows in enumerate(chunk_rows)
             for h, cols in enumerate(head_cols)]
    for ci, rows, h, cols in units:
        a = lax.dot_general(qt_s[rows, cols], kt_s[rows, cols], nt, preferred_element_type=F32)
        a_s[ci * n_heads + h] = jnp.where(causal, a, 0.0).astype(BF16)
    for ci, rows, h, cols in units:
        u_s[ci, h] = lax.dot_general(
            i_ref[rows, cols], kd_s[rows, cols], tn, preferred_element_type=F32)
    for ci, rows, h, cols in units:
        oi_s[rows, cols] = jnp.dot(
            a_s[ci * n_heads + h], i_ref[rows, cols], preferred_element_type=F32)

    gn_half = 0.5 * gn_ref[...]
    for ci, rows in enumerate(chunk_rows):
        for h, cols in enumerate(head_cols):
            st = st_ref[h]
            o = oi_s[rows, cols] + lax.dot_general(
                qe_s[rows, cols], st.astype(BF16), nt, preferred_element_type=F32)
            st_ref[h] = st * dl_s[ci:ci + 1, cols] + u_s[ci, h]
            ms = jnp.mean(o * o, axis=-1, keepdims=True)
            ogv = og_ref[rows, cols].astype(F32)
            o_ref[rows, cols] = (
                (o * lax.rsqrt(ms + EPS) * gn_half)
                * (ogv * (1.0 + jnp.tanh(0.5 * ogv)))).astype(o_ref.dtype)


def _hgrn(proj3, lb, gn, off_q, off_f, off_i, off_og, width, ts):
    b, s, _ = proj3.shape
    n_heads = width // HGRN_HEAD_DIM
    n_chunks = ts // HGRN_CHUNK
    dk = HGRN_HEAD_DIM
    spec = lambda off: pl.BlockSpec((None, ts, width), lambda i, t: (i, t, off // width))
    return pl.pallas_call(
        functools.partial(_hgrn_kernel, n_heads=n_heads, n_chunks=n_chunks),
        grid=(b, s // ts),
        in_specs=[
            spec(off_q), spec(off_f), spec(off_i), spec(off_og),
            pl.BlockSpec((1, width), lambda i, t: (0, 0)),
            pl.BlockSpec((1, dk), lambda i, t: (0, 0)),
        ],
        out_specs=pl.BlockSpec((None, ts, width), lambda i, t: (i, t, 0)),
        out_shape=jax.ShapeDtypeStruct((b, s, width), BF16),
        scratch_shapes=[
            pltpu.VMEM((n_heads, dk, dk), F32),
            pltpu.VMEM((ts, 2 * width), BF16),
            pltpu.VMEM((ts, width), F32),
            pltpu.VMEM((ts, width), F32),
            pltpu.VMEM((ts, width), BF16),
            pltpu.VMEM((ts, width), BF16),
            pltpu.VMEM((ts, width), BF16),
            pltpu.VMEM((ts, width), BF16),
            pltpu.VMEM((max(n_chunks, 8), width), F32),
            pltpu.VMEM((ts, width), F32),
            pltpu.VMEM((n_chunks, n_heads, dk, dk), F32),
            pltpu.VMEM((n_chunks * n_heads, HGRN_CHUNK, HGRN_CHUNK), BF16),
        ],
        compiler_params=_cparams(("arbitrary", "arbitrary")),
        name="hgrn2",
    )(proj3, proj3, proj3, proj3, lb, gn)


def _merge_kernel(sink_ref, q_ref, kc_ref, kp_ref, vc_ref, vp_ref,
                  oh_ref, g0_ref, g1_ref, x_ref, wa_hbm, wh_hbm, wo_hbm, gm_ref,
                  wrh_ref, wrl_ref, br_ref, h_ref, hn_ref, route_ref, plan_ref, cnt_ref,
                  oa_s, hn_s, wa_ref, wh_ref, wo_ref, stage, stage_sem, sc_s, *, tiles_per_seq):
    tm, d = x_ref.shape
    half = d // 2
    halves = (slice(0, half), slice(half, d))
    step = pl.program_id(0)
    blk = ATTN_BLOCK
    assert tm == 2 * blk

    @pl.when(step == 0)
    def _():
        cnt_ref[...] = jnp.zeros_like(cnt_ref)
        oa_s[...] = jnp.zeros_like(oa_s)
        hn_s[...] = jnp.zeros_like(hn_s)
        sc_s[...] = jnp.zeros_like(sc_s)
        rows = stage.shape[1]
        chunks = [(src, dst, r0) for src, dst in ((wa_hbm, wa_ref), (wh_hbm, wh_ref), (wo_hbm, wo_ref))
                  for r0 in range(0, src.shape[0], rows)]

        def chunk_copy(n):
            src, _, r0 = chunks[n]
            return pltpu.make_async_copy(
                src.at[pl.ds(r0, rows)], stage.at[n % 2], stage_sem.at[n % 2])

        chunk_copy(0).start()
        for n, (_, dst, r0) in enumerate(chunks):
            if n + 1 < len(chunks):
                chunk_copy(n + 1).start()
            chunk_copy(n).wait()
            dst[r0:r0 + rows, :] = stage[n % 2].astype(BF16)

    q_blocks = (slice(0, blk), slice(blk, tm))
    v_blocks = [vc_ref[rs, :] for rs in q_blocks]
    scores = [[(sc_s[(b * N_KV_HEADS + h) * 2], sc_s[(b * N_KV_HEADS + h) * 2 + 1])
               for h in range(N_KV_HEADS)] for b in range(2)]
    attn_tile = jnp.clip(step - 1, 0, pl.num_programs(0) - 4)
    prev_bias = jnp.where(attn_tile % tiles_per_seq == 0, -jnp.inf, 0.0)

    hn_prev = hn_s[...]
    hn_hi = hn_prev.astype(BF16)
    hn_lo = (hn_prev - hn_hi.astype(F32)).astype(BF16)

    oa = oa_s[...]
    ya = [jnp.dot(oa, wa_ref[:, cs], preferred_element_type=F32) for cs in halves]
    yh = [jnp.dot(oh_ref[...], wh_ref[:, cs], preferred_element_type=F32) for cs in halves]

    logits = (jnp.dot(hn_hi, wrh_ref[...], preferred_element_type=F32)
              + jnp.dot(hn_hi, wrl_ref[...], preferred_element_type=F32)
              + jnp.dot(hn_lo, wrh_ref[...], preferred_element_type=F32)
              + br_ref[...])

    _attn_finish(scores[0], v_blocks[0], vp_ref[...], sink_ref, prev_bias, oa_s, q_blocks[0])
    _attn_finish(scores[1], v_blocks[1], v_blocks[0], sink_ref, 0.0, oa_s, q_blocks[1])

    lane = lax.broadcasted_iota(jnp.int32, (tm, ROUTE_LANES), 1)
    neg = -jnp.inf

    def first_argmax(v):
        m = jnp.max(v, axis=-1, keepdims=True)
        idx = jnp.min(jnp.where(v == m, lane, ROUTE_LANES), axis=-1, keepdims=True)
        return m, idx

    is_group = lane < N_GROUPS
    gmax, gidx = first_argmax(jnp.where(is_group, logits, neg))
    p_sel = 1.0 / jnp.sum(jnp.where(is_group, jnp.exp(logits - gmax), 0.0), axis=-1, keepdims=True)
    eidx = lane - EXPERT_LANE0
    in_group = jnp.logical_and(
        jnp.logical_and(eidx >= 0, eidx < N_EXPERTS), (eidx // EXPERTS_PER_GROUP) == gidx)
    el = jnp.where(in_group, logits, neg)
    m1, i1 = first_argmax(el)
    m2, i2 = first_argmax(jnp.where(lane == i1, neg, el))
    t = jnp.exp(m2 - m1)
    w1 = p_sel / (1.0 + t)
    w2 = p_sel * t / (1.0 + t)

    sel1 = lane == i1
    sel2 = lane == i2
    onehot = jnp.logical_or(sel1, sel2).astype(BF16)
    ri = lax.broadcasted_iota(jnp.int32, (tm, tm), 0)
    ci = lax.broadcasted_iota(jnp.int32, (tm, tm), 1)
    before = (ci < ri).astype(BF16)

    h = x_ref[...]
    for k, cs in enumerate(halves):
        merged = (_sigmoid(g0_ref[:, cs].astype(F32)) * ya[k]
                  + _sigmoid(g1_ref[:, cs].astype(F32)) * yh[k]).astype(BF16)
        h = h + jnp.dot(merged, wo_ref[cs, :], preferred_element_type=F32)
        if k == 0:
            cum = jnp.dot(before, onehot, preferred_element_type=F32) + cnt_ref[...]


    r1 = jnp.sum(jnp.where(sel1, cum, 0.0), axis=-1, keepdims=True)
    r2 = jnp.sum(jnp.where(sel2, cum, 0.0), axis=-1, keepdims=True)
    routed = jnp.where(step > 2, 1.0, 0.0)
    cnt_ref[...] += routed * jnp.sum(onehot.astype(F32), axis=0, keepdims=True)

    e1 = (i1 - EXPERT_LANE0).astype(F32)
    e2 = (i2 - EXPERT_LANE0).astype(F32)
    out = jnp.zeros((tm, ROUTE_LANES), F32)
    for k, v in enumerate((e1, e2, w1, w2, r1, r2)):
        out = jnp.where(lane == k, v, out)
    route_ref[...] = out
    plan_ref[...] = out.T[:PLAN_ROWS, :].astype(jnp.int32)

    h_ref[...] = h
    ms = jnp.mean(h * h, axis=-1, keepdims=True)
    hn = h * lax.rsqrt(ms + EPS) * gm_ref[...]
    hn_ref[...] = _to_row_tiles(hn)
    hn_s[...] = hn

    k_blocks = [kc_ref[rs, :] for rs in q_blocks]
    new_scores = [_attn_scores(q_ref, q_blocks[0], k_blocks[0], kp_ref[...]),
                  _attn_scores(q_ref, q_blocks[1], k_blocks[1], k_blocks[0])]
    for b in range(2):
        for h in range(N_KV_HEADS):
            for c in range(2):
                sc_s[(b * N_KV_HEADS + h) * 2 + c] = new_scores[b][h][c]


def _attn_merge_route(sinks, o_h, proj, x2, wa, wh, wo, gm, wr_hi, wr_lo, br,
                      off, attn_w, kv_w, seq, tm):
    t, d = x2.shape
    nt = t // tm
    hw = o_h.shape[1]
    blk = ATTN_BLOCK
    const = lambda shape: pl.BlockSpec(shape, lambda i: (0, 0), pipeline_mode=pl.Buffered(1))
    hbm = pl.BlockSpec(memory_space=pl.ANY)
    assert attn_w % WEIGHT_STAGE_ROWS == 0 and hw % WEIGHT_STAGE_ROWS == 0
    stage_tile = lambda lag: (lambda i: jnp.clip(i - lag, 0, nt - 1))
    sco = stage_tile(0)
    att = stage_tile(1)
    cur = stage_tile(2)
    prv = stage_tile(3)
    row = lambda w: pl.BlockSpec((tm, w), lambda i: (cur(i), 0))
    kv_col = {"k": off["k"] // kv_w, "v": off["v"] // kv_w}
    kv_cur = lambda n, tile: pl.BlockSpec((tm, kv_w), lambda i: (tile(i), kv_col[n]))
    kv_prev = lambda n, tile: pl.BlockSpec(
        (blk, kv_w), lambda i: (jnp.maximum(tile(i) * (tm // blk) - 1, 0), kv_col[n]))
    return pl.pallas_call(
        functools.partial(_merge_kernel, tiles_per_seq=seq // tm),
        grid=(nt + 3,),
        in_specs=[
            pl.BlockSpec(memory_space=pltpu.SMEM),
            pl.BlockSpec((tm, attn_w), lambda i: (sco(i), off["qa"] // attn_w)),
            kv_cur("k", sco), kv_prev("k", sco), kv_cur("v", att), kv_prev("v", att),
            row(hw),
            pl.BlockSpec((tm, d), lambda i: (cur(i), off["g0"] // d)),
            pl.BlockSpec((tm, d), lambda i: (cur(i), off["g1"] // d)),
            row(d),
            hbm, hbm, hbm, const((1, d)),
            const((d, ROUTE_LANES)), const((d, ROUTE_LANES)), const((1, ROUTE_LANES)),
        ],
        out_specs=[row(d), pl.BlockSpec((tm, d // LANES, LANES), lambda i: (cur(i), 0, 0)),
                   pl.BlockSpec((tm, ROUTE_LANES), lambda i: (prv(i), 0)),
                   pl.BlockSpec((None, PLAN_ROWS, tm), lambda i: (prv(i), 0, 0)),
                   pl.BlockSpec((1, ROUTE_LANES), lambda i: (0, 0))],
        out_shape=[
            jax.ShapeDtypeStruct((t, d), F32),
            jax.ShapeDtypeStruct((t, d // LANES, LANES), BF16),
            jax.ShapeDtypeStruct((t, ROUTE_LANES), F32),
            jax.ShapeDtypeStruct((nt, PLAN_ROWS, tm), jnp.int32),
            jax.ShapeDtypeStruct((1, ROUTE_LANES), F32),
        ],
        scratch_shapes=[
            pltpu.VMEM((tm, attn_w), BF16), pltpu.VMEM((tm, d), F32),
            pltpu.VMEM((attn_w, d), BF16), pltpu.VMEM((hw, d), BF16), pltpu.VMEM((d, d), BF16),
            pltpu.VMEM((2, WEIGHT_STAGE_ROWS, d), F32), pltpu.SemaphoreType.DMA((2,)),
            pltpu.VMEM((2 * N_KV_HEADS * 2, Q_PER_KV * blk, blk), F32),
        ],
        compiler_params=_cparams(("arbitrary",)),
        name="attn_merge_route",
    )(sinks, proj, proj, proj, proj, proj, o_h, proj, proj, x2, wa, wh, wo, gm, wr_hi, wr_lo, br)


def _dest_kernel(pst_ref, plan_ref, dest_ref):
    e = plan_ref[:, PLAN_E:PLAN_E + 2, :]
    start = jnp.zeros(e.shape, jnp.int32)
    for j in range(N_EXPERTS):
        start = jnp.where(e == j, pst_ref[j], start)
    dest_ref[...] = start + plan_ref[:, PLAN_R:PLAN_R + 2, :]


def _dest_rows(plan, pst, tiles):
    nt, _, tm = plan.shape
    grid_spec = pltpu.PrefetchScalarGridSpec(
        num_scalar_prefetch=1,
        grid=(nt // tiles,),
        in_specs=[pl.BlockSpec((tiles, PLAN_ROWS, tm), lambda i, *_: (i, 0, 0))],
        out_specs=pl.BlockSpec((tiles, 2, tm), lambda i, *_: (i, 0, 0)),
    )
    return pl.pallas_call(
        _dest_kernel,
        grid_spec=grid_spec,
        out_shape=jax.ShapeDtypeStruct((nt, 2, tm), jnp.int32),
        compiler_params=_cparams(("arbitrary",)),
        name="moe_dest",
    )(pst, plan)


ISSUE_UNROLL = 8


def _dispatch_kernel(fill_ref, nfill_ref, dest_ref, hn_ref, rows_ref, zbuf, sem):
    tm = hn_ref.shape[0]
    bm = zbuf.shape[0]

    @pl.when(pl.program_id(0) == 0)
    def _():
        zbuf[...] = jnp.zeros_like(zbuf)

        def fill(j, carry):
            start = pl.multiple_of(fill_ref[j], bm)
            pltpu.make_async_copy(zbuf, rows_ref.at[pl.ds(start, bm)], sem).start()
            return carry

        def drain(j, carry):
            pltpu.make_async_copy(zbuf, rows_ref.at[pl.ds(0, bm)], sem).wait()
            return carry

        lax.fori_loop(0, nfill_ref[0], fill, 0)
        lax.fori_loop(0, nfill_ref[0], drain, 0)

    sub, _, tmd = dest_ref.shape

    for j in range(sub):
        def issue(r, carry, j=j):
            for k in range(2):
                pltpu.make_async_copy(
                    hn_ref.at[j * tmd + r], rows_ref.at[dest_ref[j, k, r]], sem
                ).start(priority=k)
            return carry

        lax.fori_loop(0, tmd, issue, 0, unroll=ISSUE_UNROLL)
    for _ in range(2):
        pltpu.make_async_copy(hn_ref, rows_ref.at[pl.ds(0, tm)], sem).wait()


def _dispatch(hn, dest, fill, nfill, n_rows, sub, bm):
    t, c, l = hn.shape
    nt, _, tmd = dest.shape
    tm = sub * tmd
    grid_spec = pltpu.PrefetchScalarGridSpec(
        num_scalar_prefetch=2,
        grid=(nt // sub,),
        in_specs=[
            pl.BlockSpec((sub, 2, tmd), lambda i, *_: (i, 0, 0), memory_space=pltpu.SMEM),
            pl.BlockSpec((tm, c, l), lambda i, *_: (i, 0, 0)),
        ],
        out_specs=pl.BlockSpec(memory_space=pl.ANY),
        scratch_shapes=[pltpu.VMEM((bm, c, l), hn.dtype), pltpu.SemaphoreType.DMA],
    )
    return pl.pallas_call(
        _dispatch_kernel,
        grid_spec=grid_spec,
        out_shape=jax.ShapeDtypeStruct((n_rows, c, l), hn.dtype),
        compiler_params=_cparams(("arbitrary",)),
        name="moe_dispatch",
    )(fill, nfill, dest, hn)


def _expert_kernel(be_ref, meta_ref, slot_ref, nxt_ref, x_ref, w1_hbm, w3_hbm, w2_hbm, y_ref,
                   wf1, wf3, wf2, w1b, w3b, w2b, sems):
    i = pl.program_id(0)
    e = be_ref[i]
    new_expert = jnp.logical_or(i == 0, e != be_ref[jnp.maximum(i - 1, 0)])

    def weight_copies(expert, slot):
        return [pltpu.make_async_copy(src.at[expert], dst.at[slot], sems.at[slot])
                for src, dst in ((w1_hbm, wf1), (w3_hbm, wf3), (w2_hbm, wf2))]

    @pl.when(jnp.logical_and(i < meta_ref[0], new_expert))
    def _():
        slot = slot_ref[e]
        nxt = nxt_ref[e]

        @pl.when(i == 0)
        def _():
            for c in weight_copies(e, slot):
                c.start()

        @pl.when(nxt >= 0)
        def _():
            for c in weight_copies(nxt, 1 - slot):
                c.start(priority=1)

        for c in weight_copies(e, slot):
            c.wait()
        w1b[...] = wf1[slot].astype(BF16)
        w3b[...] = wf3[slot].astype(BF16)
        w2b[...] = wf2[slot].astype(BF16)

    @pl.when(i < meta_ref[0])
    def _():
        bm = x_ref.shape[0]
        parts = (slice(0, bm // 2), slice(bm // 2, bm))
        xs = [_from_row_tiles(x_ref[rs]) for rs in parts]
        ups = [(jnp.dot(x, w1b[...], preferred_element_type=F32),
                jnp.dot(x, w3b[...], preferred_element_type=F32)) for x in xs]
        for rs, (h1, h3) in zip(parts, ups):
            hb = (h1 * _sigmoid(h1) * h3).astype(BF16)
            y_ref[rs] = _to_row_tiles(jnp.dot(hb, w2b[...], preferred_element_type=F32))

    @pl.when(i >= meta_ref[0])
    def _():
        y_ref[...] = jnp.zeros_like(y_ref)


def _experts(rows, block_expert, meta, slot_e, nxt_e, w1, w3, w2, bm):
    n_rows, c, l = rows.shape
    d = c * l
    de = w1.shape[-1]
    n_blocks = n_rows // bm
    hbm = pl.BlockSpec(memory_space=pl.ANY)
    grid_spec = pltpu.PrefetchScalarGridSpec(
        num_scalar_prefetch=4,
        grid=(n_blocks,),
        in_specs=[
            pl.BlockSpec((bm, c, l), lambda i, be, meta, *_: (jnp.minimum(i, meta[0] - 1), 0, 0)),
            hbm, hbm, hbm,
        ],
        out_specs=pl.BlockSpec((bm, c, l), lambda i, *_: (i, 0, 0)),
        scratch_shapes=[
            pltpu.VMEM((2, d, de), w1.dtype), pltpu.VMEM((2, d, de), w3.dtype),
            pltpu.VMEM((2, de, d), w2.dtype),
            pltpu.VMEM((d, de), BF16), pltpu.VMEM((d, de), BF16), pltpu.VMEM((de, d), BF16),
            pltpu.SemaphoreType.DMA((2,)),
        ],
    )
    return pl.pallas_call(
        _expert_kernel,
        grid_spec=grid_spec,
        out_shape=jax.ShapeDtypeStruct((n_rows, c, l), BF16),
        compiler_params=_cparams(("arbitrary",)),
        name="moe_experts",
    )(block_expert, meta, slot_e, nxt_e, rows, w1, w3, w2)


def _combine_kernel(dcur_ref, dnext_ref, h_ref, route_ref, g_ref, y_ref, o_ref, ybuf, sems):
    i = pl.program_id(0)
    nt = pl.num_programs(0)
    tm, d = h_ref.shape
    slot = i % 2

    def gather(dest_ref, s):
        def issue(r, carry):
            for k in range(2):
                pltpu.make_async_copy(
                    y_ref.at[dest_ref[k, r]], ybuf.at[s, k, r], sems.at[s]).start(priority=k)
            return carry
        lax.fori_loop(0, tm, issue, 0, unroll=ISSUE_UNROLL)

    @pl.when(i == 0)
    def _():
        gather(dcur_ref, slot)

    @pl.when(i + 1 < nt)
    def _():
        gather(dnext_ref, 1 - slot)

    for k in range(2):
        pltpu.make_async_copy(y_ref.at[pl.ds(0, tm)], ybuf.at[slot, k], sems.at[slot]).wait()

    route = route_ref[...]
    w1 = route[:, 2:3]
    w2 = route[:, 3:4]
    ya = _from_row_tiles(ybuf[slot, 0]).astype(F32)
    yb = _from_row_tiles(ybuf[slot, 1]).astype(F32)
    h = h_ref[...] + w1 * ya + w2 * yb
    ms = jnp.mean(h * h, axis=-1, keepdims=True)
    o_ref[...] = h * lax.rsqrt(ms + EPS) * g_ref[...]


def _combine(h, route, dest, y_rows, g, tm):
    t, d = h.shape
    nt = t // tm
    dest_spec = lambda f: pl.BlockSpec((None, 2, tm), f, memory_space=pltpu.SMEM)
    return pl.pallas_call(
        _combine_kernel,
        grid=(nt,),
        in_specs=[
            dest_spec(lambda i: (i, 0, 0)),
            dest_spec(lambda i: (jnp.minimum(i + 1, nt - 1), 0, 0)),
            pl.BlockSpec((tm, d), lambda i: (i, 0)),
            pl.BlockSpec((tm, ROUTE_LANES), lambda i: (i, 0)),
            pl.BlockSpec((1, d), lambda i: (0, 0)),
            pl.BlockSpec(memory_space=pl.ANY),
        ],
        out_specs=pl.BlockSpec((tm, d), lambda i: (i, 0)),
        out_shape=jax.ShapeDtypeStruct((t, d), F32),
        scratch_shapes=[pltpu.VMEM((2, 2, tm, d // LANES, LANES), BF16),
                        pltpu.SemaphoreType.DMA((2,))],
        compiler_params=_cparams(("arbitrary",)),
        name="moe_combine",
    )(dest, dest, h, route, g, y_rows)


def kernel(x, w_in, attn_sinks, hgrn_lb_logits, hgrn_norm_g, w_br_attn, w_br_hgrn, w_out,
           mix_norm_g, moe_norm_g, w_router_group, b_router_group, w_router_expert,
           b_router_expert, w1, w3, w2, final_norm_g):
    b, s, d = x.shape
    t = b * s
    depth = w_in.shape[0]
    assert depth == 1
    attn_w = (d // 128) * ATTN_HEAD_DIM
    kv_w = attn_w // Q_PER_KV
    hg_w = (d // 256) * HGRN_HEAD_DIM
    assert kv_w == N_KV_HEADS * ATTN_HEAD_DIM

    src = {}
    off = 0
    for name, width in (("qa", attn_w), ("k", kv_w), ("v", kv_w), ("qh", hg_w), ("f", hg_w),
                        ("i", hg_w), ("og", hg_w), ("g0", d), ("g1", d)):
        src[name] = (off, width)
        off += width
    order = ("g0", "g1", "qa", "qh", "f", "i", "og", "k", "v")
    dst = {}
    off = 0
    for name in order:
        dst[name] = off
        off += src[name][1]
    n_in = off
    l = 0
    tn_in = 512
    assert all(src[n][0] % tn_in == 0 for n in order if n != "v") and (2 * kv_w) % tn_in == 0
    col_blocks = []
    for n in order[:-1]:
        width = src[n][1] if n != "k" else 2 * kv_w
        col_blocks += [src[n][0] // tn_in + c for c in range(width // tn_in)]
    col_blocks = jnp.asarray(col_blocks, jnp.int32)
    w_bf = w_in[l]

    lb = jnp.cumsum(jax.nn.softmax(hgrn_lb_logits.astype(F32), axis=0), axis=0)[l].reshape(1, hg_w)

    x2 = x.reshape(t, d)
    tm_in = min(2048, t)
    proj = _in_proj(x2, mix_norm_g[l].reshape(1, d), w_bf, col_blocks, tm_in, tn_in)
    proj3 = proj.reshape(b, s, n_in)

    o_h = _hgrn(proj3, lb, hgrn_norm_g[l].reshape(1, HGRN_HEAD_DIM).astype(F32),
                dst["qh"], dst["f"], dst["i"], dst["og"], hg_w, min(256, s))

    w_r = jnp.zeros((d, ROUTE_LANES), F32)
    w_r = w_r.at[:, :N_GROUPS].set(w_router_group[l])
    w_r = w_r.at[:, EXPERT_LANE0:EXPERT_LANE0 + N_EXPERTS].set(w_router_expert[l])
    wr_hi = w_r.astype(BF16)
    wr_lo = (w_r - wr_hi.astype(F32)).astype(BF16)
    b_r = jnp.zeros((1, ROUTE_LANES), F32)
    b_r = b_r.at[0, :N_GROUPS].set(b_router_group[l])
    b_r = b_r.at[0, EXPERT_LANE0:EXPERT_LANE0 + N_EXPERTS].set(b_router_expert[l])

    tm = 2 * ATTN_BLOCK
    h, hn, route, plan, cnt = _attn_merge_route(
        attn_sinks[l].astype(F32), o_h.reshape(t, hg_w), proj, x2,
        w_br_attn[l], w_br_hgrn[l], w_out[l],
        moe_norm_g[l].reshape(1, d), wr_hi, wr_lo, b_r, dst, attn_w, kv_w, s, tm)

    bm = MOE_ROWS
    counts = cnt[0, EXPERT_LANE0:EXPERT_LANE0 + N_EXPERTS].astype(jnp.int32)
    padded = ((counts + bm - 1) // bm) * bm
    pad_end = jnp.cumsum(padded)
    pad_start = (pad_end - padded).astype(jnp.int32)
    n_blocks = (2 * t) // bm + N_EXPERTS
    n_used = pad_end[-1] // bm
    blk_ids = jnp.minimum(jnp.arange(n_blocks, dtype=jnp.int32), n_used - 1)
    block_expert = jnp.minimum(
        jnp.sum((pad_end[None, :] <= (blk_ids * bm)[:, None]).astype(jnp.int32), axis=1),
        N_EXPERTS - 1)
    meta = jnp.stack([n_used, n_used]).astype(jnp.int32)

    blk_all = jnp.arange(n_blocks, dtype=jnp.int32)
    fill = jnp.concatenate([
        jnp.maximum(pad_end - bm, 0).astype(jnp.int32),
        jnp.minimum(n_used + blk_all, n_blocks - 1) * bm])
    nfill = (N_EXPERTS + n_blocks - n_used).astype(jnp.int32).reshape(1)

    eid = jnp.arange(N_EXPERTS, dtype=jnp.int32)
    nonempty = counts > 0
    slot_e = ((jnp.cumsum(nonempty.astype(jnp.int32)) - 1) % 2).astype(jnp.int32)
    later = jnp.logical_and(nonempty[None, :], eid[None, :] > eid[:, None])
    nxt_e = jnp.min(jnp.where(later, eid[None, :], N_EXPERTS), axis=1)
    nxt_e = jnp.where(nxt_e == N_EXPERTS, -1, nxt_e).astype(jnp.int32)

    nt = t // tm
    dest = _dest_rows(plan, pad_start, min(8, nt))
    rows = _dispatch(hn, dest, fill, nfill, n_blocks * bm, min(4, nt), bm)
    y_rows = _experts(rows, block_expert, meta, slot_e, nxt_e, w1[l], w3[l], w2[l], bm)
    out = _combine(h, route, dest, y_rows, final_norm_g.reshape(1, d), tm)
    return out.reshape(b, s, d)
```

```python
import functools

import jax
import jax.numpy as jnp
from jax import lax
from jax.experimental import pallas as pl
from jax.experimental.pallas import tpu as pltpu

F32 = jnp.float32
BF16 = jnp.bfloat16

EPS = 1e-6
ATTN_HEAD_DIM = 64
Q_PER_KV = 4
N_KV_HEADS = 4
ATTN_BLOCK = 128
HGRN_HEAD_DIM = 128
HGRN_CHUNK = 64
N_GROUPS = 4
EXPERTS_PER_GROUP = 8
N_EXPERTS = N_GROUPS * EXPERTS_PER_GROUP
LANES = 128
ROUTE_LANES = LANES
EXPERT_LANE0 = N_GROUPS
MOE_ROWS = 256
WEIGHT_STAGE_ROWS = 256
PLAN_ROWS = 8
PLAN_E, PLAN_R = 0, 4
VMEM_LIMIT = 56 * 1024 * 1024
VMEM_LIMIT_IN_PROJ = 60 * 1024 * 1024


def _sigmoid(x):
    return 0.5 * jnp.tanh(0.5 * x) + 0.5


def _cparams(sem, vmem=VMEM_LIMIT):
    return pltpu.CompilerParams(dimension_semantics=sem, vmem_limit_bytes=vmem)


def _to_row_tiles(v):
    m, d = v.shape
    return v.reshape(m, d // LANES, LANES).astype(BF16)


def _from_row_tiles(p):
    m, c, l = p.shape
    return p.reshape(m, c * l)


NORM_SLICES = 16


def _inproj_kernel(cols_ref, x_hbm, g_ref, w_ref, o_ref, xbuf, xn_a, xn_b, sem):
    del cols_ref
    i, j = pl.program_id(0), pl.program_id(1)
    ni = pl.num_programs(0)
    tm = xbuf.shape[0]
    rows = tm // NORM_SLICES

    def x_copy(tile):
        return pltpu.make_async_copy(
            x_hbm.at[pl.ds(pl.multiple_of(tile * tm, tm), tm)], xbuf, sem)

    def normalise(x):
        ms = jnp.mean(x * x, axis=-1, keepdims=True)
        return (x * lax.rsqrt(ms + EPS) * g_ref[...]).astype(BF16)

    @pl.when(jnp.logical_and(i == 0, j == 0))
    def _():
        first = x_copy(0)
        first.start()
        first.wait()
        xn_a[...] = normalise(xbuf[...])

    @pl.when(jnp.logical_and(j == 1, i + 1 < ni))
    def _():
        x_copy(i + 1).wait()

    def step(xn_cur, xn_next):
        r0 = pl.multiple_of(jnp.clip(j - 1, 0, NORM_SLICES - 1) * rows, rows)
        xn_next[pl.ds(r0, rows), :] = normalise(xbuf[pl.ds(r0, rows), :])
        o_ref[...] = jnp.dot(xn_cur[...], w_ref[...].astype(BF16),
                             preferred_element_type=F32).astype(o_ref.dtype)

    pl.when(i % 2 == 0)(lambda: step(xn_a, xn_b))
    pl.when(i % 2 == 1)(lambda: step(xn_b, xn_a))

    @pl.when(jnp.logical_and(j == 0, i + 1 < ni))
    def _():
        x_copy(i + 1).start()


def _in_proj(x2, g, w, col_blocks, tm, tn):
    t, d = x2.shape
    n = w.shape[1]
    assert n // tn > NORM_SLICES and tm % NORM_SLICES == 0
    grid_spec = pltpu.PrefetchScalarGridSpec(
        num_scalar_prefetch=1,
        grid=(t // tm, n // tn),
        in_specs=[
            pl.BlockSpec(memory_space=pl.ANY),
            pl.BlockSpec((1, d), lambda i, j, cols: (0, 0)),
            pl.BlockSpec((d, tn), lambda i, j, cols: (0, cols[j])),
        ],
        out_specs=pl.BlockSpec((tm, tn), lambda i, j, cols: (i, j)),
        scratch_shapes=[pltpu.VMEM((tm, d), F32), pltpu.VMEM((tm, d), BF16),
                        pltpu.VMEM((tm, d), BF16), pltpu.SemaphoreType.DMA],
    )
    return pl.pallas_call(
        _inproj_kernel,
        grid_spec=grid_spec,
        out_shape=jax.ShapeDtypeStruct((t, n), BF16),
        compiler_params=_cparams(("arbitrary", "arbitrary"), VMEM_LIMIT_IN_PROJ),
        name="in_proj",
    )(col_blocks, x2, g, w)


def _attn_scores(q_ref, rows, k_cur, k_prev):
    dh, g = ATTN_HEAD_DIM, Q_PER_KV
    nt = (((1,), (1,)), ((), ()))
    scores = []
    for h in range(N_KV_HEADS):
        q4 = jnp.concatenate(
            [q_ref[rows, (h * g + j) * dh:(h * g + j + 1) * dh] for j in range(g)], axis=0)
        scores.append(
            (lax.dot_general(q4, k_cur[:, h * dh:(h + 1) * dh], nt, preferred_element_type=F32),
             lax.dot_general(q4, k_prev[:, h * dh:(h + 1) * dh], nt, preferred_element_type=F32)))
    return scores


def _attn_finish(scores, v_cur, v_prev, sink_ref, prev_bias, o_ref, rows):
    blk, dh, g = ATTN_BLOCK, ATTN_HEAD_DIM, Q_PER_KV
    n_rows = g * blk
    qi = lax.broadcasted_iota(jnp.int32, (n_rows, blk), 0) % blk
    kj = lax.broadcasted_iota(jnp.int32, (n_rows, blk), 1)
    mask_cur = kj <= qi
    head_of_row = lax.broadcasted_iota(jnp.int32, (n_rows, 1), 0) // blk
    scale = dh ** -0.5
    for h in range(N_KV_HEADS):
        sc, sp = scores[h]
        s = jnp.where(mask_cur, sc, sp + prev_bias) * scale
        sink = jnp.zeros((n_rows, 1), F32)
        for j in range(g):
            sink = jnp.where(head_of_row == j, sink_ref[h * g + j], sink)
        m = jnp.maximum(jnp.max(s, axis=-1, keepdims=True), sink)
        p = jnp.exp(s - m)
        den = jnp.sum(p, axis=-1, keepdims=True) + jnp.exp(sink - m)
        p_cur = jnp.where(mask_cur, p, 0.0).astype(BF16)
        p_prev = jnp.where(mask_cur, 0.0, p).astype(BF16)
        o = (jnp.dot(p_cur, v_cur[:, h * dh:(h + 1) * dh], preferred_element_type=F32)
             + jnp.dot(p_prev, v_prev[:, h * dh:(h + 1) * dh], preferred_element_type=F32)) / den
        for j in range(g):
            o_ref[rows, (h * g + j) * dh:(h * g + j + 1) * dh] = (
                o[j * blk:(j + 1) * blk, :].astype(o_ref.dtype))


def _hgrn_kernel(q_ref, f_ref, i_ref, og_ref, lb_ref, gn_ref, o_ref,
                 st_ref, gh_s, f_s, b_s, qt_s, kt_s, qe_s, kd_s, dl_s, oi_s, u_s, a_s,
                 *, n_heads, n_chunks):
    c, dk = HGRN_CHUNK, HGRN_HEAD_DIM
    w = n_heads * dk

    @pl.when(pl.program_id(1) == 0)
    def _():
        st_ref[...] = jnp.zeros_like(st_ref)

    ti = lax.broadcasted_iota(jnp.int32, (c, c), 0)
    si = lax.broadcasted_iota(jnp.int32, (c, c), 1)
    causal = si <= ti
    tri = causal.astype(BF16)
    nt = (((1,), (1,)), ((), ()))
    tn = (((0,), (0,)), ((), ()))
    qscale = dk ** -0.5
    chunk_rows = [slice(ci * c, (ci + 1) * c) for ci in range(n_chunks)]
    head_cols = [slice(h * dk, (h + 1) * dk) for h in range(n_heads)]

    lb = lb_ref[...]
    c0 = 0.5 * (1.0 + lb)
    c1 = 0.5 * (1.0 - lb)
    tf = jnp.tanh(0.5 * f_ref[...].astype(F32))
    f_s[...] = c1 * (1.0 - tf)
    gl = jnp.log(c0 + c1 * tf)
    g_hi = gl.astype(BF16)
    gh_s[:, :w] = g_hi
    gh_s[:, w:] = (gl - g_hi.astype(F32)).astype(BF16)

    for rows in chunk_rows:
        bcat = jnp.dot(tri, gh_s[rows, :], preferred_element_type=F32)
        b_s[rows, :] = bcat[:, :w] + bcat[:, w:]

    for ci, rows in enumerate(chunk_rows):
        bc = b_s[rows, :]
        b_last = bc[c - 1:c, :]
        r = 0.5 * b_last
        qv = q_ref[rows, :].astype(F32)
        qt = (qv * (0.5 * qscale)) * (1.0 + jnp.tanh(0.5 * qv)) * jnp.exp(bc - r)
        kt = f_s[rows, :] * jnp.exp(r - bc)
        qt_s[rows, :] = qt.astype(BF16)
        kt_s[rows, :] = kt.astype(BF16)
        qe_s[rows, :] = (qt * jnp.exp(r)).astype(BF16)
        kd_s[rows, :] = (kt * jnp.exp(b_last - r)).astype(BF16)
        dl_s[ci:ci + 1, :] = jnp.exp(b_last)

    units = [(ci, rows, h, cols) for ci, rows in enumerate(chunk_rows)
             for h, cols in enumerate(head_cols)]
    for ci, rows, h, cols in units:
        a = lax.dot_general(qt_s[rows, cols], kt_s[rows, cols], nt, preferred_element_type=F32)
        a_s[ci * n_heads + h] = jnp.where(causal, a, 0.0).astype(BF16)
    for ci, rows, h, cols in units:
        u_s[ci, h] = lax.dot_general(
            i_ref[rows, cols], kd_s[rows, cols], tn, preferred_element_type=F32)
    for ci, rows, h, cols in units:
        oi_s[rows, cols] = jnp.dot(
            a_s[ci * n_heads + h], i_ref[rows, cols], preferred_element_type=F32)

    gn_half = 0.5 * gn_ref[...]
    for ci, rows in enumerate(chunk_rows):
        for h, cols in enumerate(head_cols):
            st = st_ref[h]
            o = oi_s[rows, cols] + lax.dot_general(
                qe_s[rows, cols], st.astype(BF16), nt, preferred_element_type=F32)
            st_ref[h] = st * dl_s[ci:ci + 1, cols] + u_s[ci, h]
            ms = jnp.mean(o * o, axis=-1, keepdims=True)
            ogv = og_ref[rows, cols].astype(F32)
            o_ref[rows, cols] = (
                (o * lax.rsqrt(ms + EPS) * gn_half)
                * (ogv * (1.0 + jnp.tanh(0.5 * ogv)))).astype(o_ref.dtype)


def _hgrn(proj3, lb, gn, off_q, off_f, off_i, off_og, width, ts):
    b, s, _ = proj3.shape
    n_heads = width // HGRN_HEAD_DIM
    n_chunks = ts // HGRN_CHUNK
    dk = HGRN_HEAD_DIM
    spec = lambda off: pl.BlockSpec((None, ts, width), lambda i, t: (i, t, off // width))
    return pl.pallas_call(
        functools.partial(_hgrn_kernel, n_heads=n_heads, n_chunks=n_chunks),
        grid=(b, s // ts),
        in_specs=[
            spec(off_q), spec(off_f), spec(off_i), spec(off_og),
            pl.BlockSpec((1, width), lambda i, t: (0, 0)),
            pl.BlockSpec((1, dk), lambda i, t: (0, 0)),
        ],
        out_specs=pl.BlockSpec((None, ts, width), lambda i, t: (i, t, 0)),
        out_shape=jax.ShapeDtypeStruct((b, s, width), BF16),
        scratch_shapes=[
            pltpu.VMEM((n_heads, dk, dk), F32),
            pltpu.VMEM((ts, 2 * width), BF16),
            pltpu.VMEM((ts, width), F32),
            pltpu.VMEM((ts, width), F32),
            pltpu.VMEM((ts, width), BF16),
            pltpu.VMEM((ts, width), BF16),
            pltpu.VMEM((ts, width), BF16),
            pltpu.VMEM((ts, width), BF16),
            pltpu.VMEM((max(n_chunks, 8), width), F32),
            pltpu.VMEM((ts, width), F32),
            pltpu.VMEM((n_chunks, n_heads, dk, dk), F32),
            pltpu.VMEM((n_chunks * n_heads, HGRN_CHUNK, HGRN_CHUNK), BF16),
        ],
        compiler_params=_cparams(("arbitrary", "arbitrary")),
        name="hgrn2",
    )(proj3, proj3, proj3, proj3, lb, gn)


def _merge_kernel(sink_ref, q_ref, kc_ref, kp_ref, vc_ref, vp_ref,
                  oh_ref, g0_ref, g1_ref, x_ref, wa_hbm, wh_hbm, wo_hbm, gm_ref,
                  wrh_ref, wrl_ref, br_ref, h_ref, hn_ref, route_ref, plan_ref, cnt_ref,
                  oa_s, hn_s, wa_ref, wh_ref, wo_ref, stage, stage_sem, sc_s, *, tiles_per_seq):
    tm, d = x_ref.shape
    half = d // 2
    halves = (slice(0, half), slice(half, d))
    step = pl.program_id(0)
    blk = ATTN_BLOCK
    assert tm == 2 * blk

    @pl.when(step == 0)
    def _():
        cnt_ref[...] = jnp.zeros_like(cnt_ref)
        oa_s[...] = jnp.zeros_like(oa_s)
        hn_s[...] = jnp.zeros_like(hn_s)
        sc_s[...] = jnp.zeros_like(sc_s)
        rows = stage.shape[1]
        chunks = [(src, dst, r0) for src, dst in ((wa_hbm, wa_ref), (wh_hbm, wh_ref), (wo_hbm, wo_ref))
                  for r0 in range(0, src.shape[0], rows)]

        def chunk_copy(n):
            src, _, r0 = chunks[n]
            return pltpu.make_async_copy(
                src.at[pl.ds(r0, rows)], stage.at[n % 2], stage_sem.at[n % 2])

        chunk_copy(0).start()
        for n, (_, dst, r0) in enumerate(chunks):
            if n + 1 < len(chunks):
                chunk_copy(n + 1).start()
            chunk_copy(n).wait()
            dst[r0:r0 + rows, :] = stage[n % 2].astype(BF16)

    q_blocks = (slice(0, blk), slice(blk, tm))
    v_blocks = [vc_ref[rs, :] for rs in q_blocks]
    scores = [[(sc_s[(b * N_KV_HEADS + h) * 2], sc_s[(b * N_KV_HEADS + h) * 2 + 1])
               for h in range(N_KV_HEADS)] for b in range(2)]
    attn_tile = jnp.clip(step - 1, 0, pl.num_programs(0) - 4)
    prev_bias = jnp.where(attn_tile % tiles_per_seq == 0, -jnp.inf, 0.0)

    hn_prev = hn_s[...]
    hn_hi = hn_prev.astype(BF16)
    hn_lo = (hn_prev - hn_hi.astype(F32)).astype(BF16)

    oa = oa_s[...]
    ya = [jnp.dot(oa, wa_ref[:, cs], preferred_element_type=F32) for cs in halves]
    yh = [jnp.dot(oh_ref[...], wh_ref[:, cs], preferred_element_type=F32) for cs in halves]

    logits = (jnp.dot(hn_hi, wrh_ref[...], preferred_element_type=F32)
              + jnp.dot(hn_hi, wrl_ref[...], preferred_element_type=F32)
              + jnp.dot(hn_lo, wrh_ref[...], preferred_element_type=F32)
              + br_ref[...])

    _attn_finish(scores[0], v_blocks[0], vp_ref[...], sink_ref, prev_bias, oa_s, q_blocks[0])
    _attn_finish(scores[1], v_blocks[1], v_blocks[0], sink_ref, 0.0, oa_s, q_blocks[1])

    lane = lax.broadcasted_iota(jnp.int32, (tm, ROUTE_LANES), 1)
    neg = -jnp.inf

    def first_argmax(v):
        m = jnp.max(v, axis=-1, keepdims=True)
        idx = jnp.min(jnp.where(v == m, lane, ROUTE_LANES), axis=-1, keepdims=True)
        return m, idx

    is_group = lane < N_GROUPS
    gmax, gidx = first_argmax(jnp.where(is_group, logits, neg))
    p_sel = 1.0 / jnp.sum(jnp.where(is_group, jnp.exp(logits - gmax), 0.0), axis=-1, keepdims=True)
    eidx = lane - EXPERT_LANE0
    in_group = jnp.logical_and(
        jnp.logical_and(eidx >= 0, eidx < N_EXPERTS), (eidx // EXPERTS_PER_GROUP) == gidx)
    el = jnp.where(in_group, logits, neg)
    m1, i1 = first_argmax(el)
    m2, i2 = first_argmax(jnp.where(lane == i1, neg, el))
    t = jnp.exp(m2 - m1)
    w1 = p_sel / (1.0 + t)
    w2 = p_sel * t / (1.0 + t)

    sel1 = lane == i1
    sel2 = lane == i2
    onehot = jnp.logical_or(sel1, sel2).astype(BF16)
    ri = lax.broadcasted_iota(jnp.int32, (tm, tm), 0)
    ci = lax.broadcasted_iota(jnp.int32, (tm, tm), 1)
    before = (ci < ri).astype(BF16)

    h = x_ref[...]
    for k, cs in enumerate(halves):
        merged = (_sigmoid(g0_ref[:, cs].astype(F32)) * ya[k]
                  + _sigmoid(g1_ref[:, cs].astype(F32)) * yh[k]).astype(BF16)
        h = h + jnp.dot(merged, wo_ref[cs, :], preferred_element_type=F32)
        if k == 0:
            cum = jnp.dot(before, onehot, preferred_element_type=F32) + cnt_ref[...]


    r1 = jnp.sum(jnp.where(sel1, cum, 0.0), axis=-1, keepdims=True)
    r2 = jnp.sum(jnp.where(sel2, cum, 0.0), axis=-1, keepdims=True)
    routed = jnp.where(step > 2, 1.0, 0.0)
    cnt_ref[...] += routed * jnp.sum(onehot.astype(F32), axis=0, keepdims=True)

    e1 = (i1 - EXPERT_LANE0).astype(F32)
    e2 = (i2 - EXPERT_LANE0).astype(F32)
    out = jnp.zeros((tm, ROUTE_LANES), F32)
    for k, v in enumerate((e1, e2, w1, w2, r1, r2)):
        out = jnp.where(lane == k, v, out)
    route_ref[...] = out
    plan_ref[...] = out.T[:PLAN_ROWS, :].astype(jnp.int32)

    h_ref[...] = h
    ms = jnp.mean(h * h, axis=-1, keepdims=True)
    hn = h * lax.rsqrt(ms + EPS) * gm_ref[...]
    hn_ref[...] = _to_row_tiles(hn)
    hn_s[...] = hn

    k_blocks = [kc_ref[rs, :] for rs in q_blocks]
    new_scores = [_attn_scores(q_ref, q_blocks[0], k_blocks[0], kp_ref[...]),
                  _attn_scores(q_ref, q_blocks[1], k_blocks[1], k_blocks[0])]
    for b in range(2):
        for h in range(N_KV_HEADS):
            for c in range(2):
                sc_s[(b * N_KV_HEADS + h) * 2 + c] = new_scores[b][h][c]


def _attn_merge_route(sinks, o_h, proj, x2, wa, wh, wo, gm, wr_hi, wr_lo, br,
                      off, attn_w, kv_w, seq, tm):
    t, d = x2.shape
    nt = t // tm
    hw = o_h.shape[1]
    blk = ATTN_BLOCK
    const = lambda shape: pl.BlockSpec(shape, lambda i: (0, 0), pipeline_mode=pl.Buffered(1))
    hbm = pl.BlockSpec(memory_space=pl.ANY)
    assert attn_w % WEIGHT_STAGE_ROWS == 0 and hw % WEIGHT_STAGE_ROWS == 0
    stage_tile = lambda lag: (lambda i: jnp.clip(i - lag, 0, nt - 1))
    sco = stage_tile(0)
    att = stage_tile(1)
    cur = stage_tile(2)
    prv = stage_tile(3)
    row = lambda w: pl.BlockSpec((tm, w), lambda i: (cur(i), 0))
    kv_col = {"k": off["k"] // kv_w, "v": off["v"] // kv_w}
    kv_cur = lambda n, tile: pl.BlockSpec((tm, kv_w), lambda i: (tile(i), kv_col[n]))
    kv_prev = lambda n, tile: pl.BlockSpec(
        (blk, kv_w), lambda i: (jnp.maximum(tile(i) * (tm // blk) - 1, 0), kv_col[n]))
    return pl.pallas_call(
        functools.partial(_merge_kernel, tiles_per_seq=seq // tm),
        grid=(nt + 3,),
        in_specs=[
            pl.BlockSpec(memory_space=pltpu.SMEM),
            pl.BlockSpec((tm, attn_w), lambda i: (sco(i), off["qa"] // attn_w)),
            kv_cur("k", sco), kv_prev("k", sco), kv_cur("v", att), kv_prev("v", att),
            row(hw),
            pl.BlockSpec((tm, d), lambda i: (cur(i), off["g0"] // d)),
            pl.BlockSpec((tm, d), lambda i: (cur(i), off["g1"] // d)),
            row(d),
            hbm, hbm, hbm, const((1, d)),
            const((d, ROUTE_LANES)), const((d, ROUTE_LANES)), const((1, ROUTE_LANES)),
        ],
        out_specs=[row(d), pl.BlockSpec((tm, d // LANES, LANES), lambda i: (cur(i), 0, 0)),
                   pl.BlockSpec((tm, ROUTE_LANES), lambda i: (prv(i), 0)),
                   pl.BlockSpec((None, PLAN_ROWS, tm), lambda i: (prv(i), 0, 0)),
                   pl.BlockSpec((1, ROUTE_LANES), lambda i: (0, 0))],
        out_shape=[
            jax.ShapeDtypeStruct((t, d), F32),
            jax.ShapeDtypeStruct((t, d // LANES, LANES), BF16),
            jax.ShapeDtypeStruct((t, ROUTE_LANES), F32),
            jax.ShapeDtypeStruct((nt, PLAN_ROWS, tm), jnp.int32),
            jax.ShapeDtypeStruct((1, ROUTE_LANES), F32),
        ],
        scratch_shapes=[
            pltpu.VMEM((tm, attn_w), BF16), pltpu.VMEM((tm, d), F32),
            pltpu.VMEM((attn_w, d), BF16), pltpu.VMEM((hw, d), BF16), pltpu.VMEM((d, d), BF16),
            pltpu.VMEM((2, WEIGHT_STAGE_ROWS, d), F32), pltpu.SemaphoreType.DMA((2,)),
            pltpu.VMEM((2 * N_KV_HEADS * 2, Q_PER_KV * blk, blk), F32),
        ],
        compiler_params=_cparams(("arbitrary",)),
        name="attn_merge_route",
    )(sinks, proj, proj, proj, proj, proj, o_h, proj, proj, x2, wa, wh, wo, gm, wr_hi, wr_lo, br)


def _dest_kernel(pst_ref, plan_ref, dest_ref):
    e = plan_ref[:, PLAN_E:PLAN_E + 2, :]
    start = jnp.zeros(e.shape, jnp.int32)
    for j in range(N_EXPERTS):
        start = jnp.where(e == j, pst_ref[j], start)
    dest_ref[...] = start + plan_ref[:, PLAN_R:PLAN_R + 2, :]


def _dest_rows(plan, pst, tiles):
    nt, _, tm = plan.shape
    grid_spec = pltpu.PrefetchScalarGridSpec(
        num_scalar_prefetch=1,
        grid=(nt // tiles,),
        in_specs=[pl.BlockSpec((tiles, PLAN_ROWS, tm), lambda i, *_: (i, 0, 0))],
        out_specs=pl.BlockSpec((tiles, 2, tm), lambda i, *_: (i, 0, 0)),
    )
    return pl.pallas_call(
        _dest_kernel,
        grid_spec=grid_spec,
        out_shape=jax.ShapeDtypeStruct((nt, 2, tm), jnp.int32),
        compiler_params=_cparams(("arbitrary",)),
        name="moe_dest",
    )(pst, plan)


ISSUE_UNROLL = 8


def _dispatch_kernel(fill_ref, nfill_ref, dest_ref, hn_ref, rows_ref, zbuf, sem):
    tm = hn_ref.shape[0]
    bm = zbuf.shape[0]

    @pl.when(pl.program_id(0) == 0)
    def _():
        zbuf[...] = jnp.zeros_like(zbuf)

        def fill(j, carry):
            start = pl.multiple_of(fill_ref[j], bm)
            pltpu.make_async_copy(zbuf, rows_ref.at[pl.ds(start, bm)], sem).start()
            return carry

        def drain(j, carry):
            pltpu.make_async_copy(zbuf, rows_ref.at[pl.ds(0, bm)], sem).wait()
            return carry

        lax.fori_loop(0, nfill_ref[0], fill, 0)
        lax.fori_loop(0, nfill_ref[0], drain, 0)

    sub, _, tmd = dest_ref.shape

    for j in range(sub):
        def issue(r, carry, j=j):
            for k in range(2):
                pltpu.make_async_copy(
                    hn_ref.at[j * tmd + r], rows_ref.at[dest_ref[j, k, r]], sem
                ).start(priority=k)
            return carry

        lax.fori_loop(0, tmd, issue, 0, unroll=ISSUE_UNROLL)
    for _ in range(2):
        pltpu.make_async_copy(hn_ref, rows_ref.at[pl.ds(0, tm)], sem).wait()


def _dispatch(hn, dest, fill, nfill, n_rows, sub, bm):
    t, c, l = hn.shape
    nt, _, tmd = dest.shape
    tm = sub * tmd
    grid_spec = pltpu.PrefetchScalarGridSpec(
        num_scalar_prefetch=2,
        grid=(nt // sub,),
        in_specs=[
            pl.BlockSpec((sub, 2, tmd), lambda i, *_: (i, 0, 0), memory_space=pltpu.SMEM),
            pl.BlockSpec((tm, c, l), lambda i, *_: (i, 0, 0)),
        ],
        out_specs=pl.BlockSpec(memory_space=pl.ANY),
        scratch_shapes=[pltpu.VMEM((bm, c, l), hn.dtype), pltpu.SemaphoreType.DMA],
    )
    return pl.pallas_call(
        _dispatch_kernel,
        grid_spec=grid_spec,
        out_shape=jax.ShapeDtypeStruct((n_rows, c, l), hn.dtype),
        compiler_params=_cparams(("arbitrary",)),
        name="moe_dispatch",
    )(fill, nfill, dest, hn)


def _expert_kernel(be_ref, meta_ref, slot_ref, nxt_ref, x_ref, w1_hbm, w3_hbm, w2_hbm, y_ref,
                   wf1, wf3, wf2, w1b, w3b, w2b, sems):
    i = pl.program_id(0)
    e = be_ref[i]
    new_expert = jnp.logical_or(i == 0, e != be_ref[jnp.maximum(i - 1, 0)])

    def weight_copies(expert, slot):
        return [pltpu.make_async_copy(src.at[expert], dst.at[slot], sems.at[slot])
                for src, dst in ((w1_hbm, wf1), (w3_hbm, wf3), (w2_hbm, wf2))]

    @pl.when(jnp.logical_and(i < meta_ref[0], new_expert))
    def _():
        slot = slot_ref[e]
        nxt = nxt_ref[e]

        @pl.when(i == 0)
        def _():
            for c in weight_copies(e, slot):
                c.start()

        @pl.when(nxt >= 0)
        def _():
            for c in weight_copies(nxt, 1 - slot):
                c.start(priority=1)

        for c in weight_copies(e, slot):
            c.wait()
        w1b[...] = wf1[slot].astype(BF16)
        w3b[...] = wf3[slot].astype(BF16)
        w2b[...] = wf2[slot].astype(BF16)

    @pl.when(i < meta_ref[0])
    def _():
        bm = x_ref.shape[0]
        parts = (slice(0, bm // 2), slice(bm // 2, bm))
        xs = [_from_row_tiles(x_ref[rs]) for rs in parts]
        ups = [(jnp.dot(x, w1b[...], preferred_element_type=F32),
                jnp.dot(x, w3b[...], preferred_element_type=F32)) for x in xs]
        for rs, (h1, h3) in zip(parts, ups):
            hb = (h1 * _sigmoid(h1) * h3).astype(BF16)
            y_ref[rs] = _to_row_tiles(jnp.dot(hb, w2b[...], preferred_element_type=F32))

    @pl.when(i >= meta_ref[0])
    def _():
        y_ref[...] = jnp.zeros_like(y_ref)


def _experts(rows, block_expert, meta, slot_e, nxt_e, w1, w3, w2, bm):
    n_rows, c, l = rows.shape
    d = c * l
    de = w1.shape[-1]
    n_blocks = n_rows // bm
    hbm = pl.BlockSpec(memory_space=pl.ANY)
    grid_spec = pltpu.PrefetchScalarGridSpec(
        num_scalar_prefetch=4,
        grid=(n_blocks,),
        in_specs=[
            pl.BlockSpec((bm, c, l), lambda i, be, meta, *_: (jnp.minimum(i, meta[0] - 1), 0, 0)),
            hbm, hbm, hbm,
        ],
        out_specs=pl.BlockSpec((bm, c, l), lambda i, *_: (i, 0, 0)),
        scratch_shapes=[
            pltpu.VMEM((2, d, de), w1.dtype), pltpu.VMEM((2, d, de), w3.dtype),
            pltpu.VMEM((2, de, d), w2.dtype),
            pltpu.VMEM((d, de), BF16), pltpu.VMEM((d, de), BF16), pltpu.VMEM((de, d), BF16),
            pltpu.SemaphoreType.DMA((2,)),
        ],
    )
    return pl.pallas_call(
        _expert_kernel,
        grid_spec=grid_spec,
        out_shape=jax.ShapeDtypeStruct((n_rows, c, l), BF16),
        compiler_params=_cparams(("arbitrary",)),
        name="moe_experts",
    )(block_expert, meta, slot_e, nxt_e, rows, w1, w3, w2)


def _combine_kernel(dcur_ref, dnext_ref, h_ref, route_ref, g_ref, y_ref, o_ref, ybuf, sems):
    i = pl.program_id(0)
    nt = pl.num_programs(0)
    tm, d = h_ref.shape
    slot = i % 2

    def gather(dest_ref, s):
        def issue(r, carry):
            for k in range(2):
                pltpu.make_async_copy(
                    y_ref.at[dest_ref[k, r]], ybuf.at[s, k, r], sems.at[s]).start(priority=k)
            return carry
        lax.fori_loop(0, tm, issue, 0, unroll=ISSUE_UNROLL)

    @pl.when(i == 0)
    def _():
        gather(dcur_ref, slot)

    @pl.when(i + 1 < nt)
    def _():
        gather(dnext_ref, 1 - slot)

    for k in range(2):
        pltpu.make_async_copy(y_ref.at[pl.ds(0, tm)], ybuf.at[slot, k], sems.at[slot]).wait()

    route = route_ref[...]
    w1 = route[:, 2:3]
    w2 = route[:, 3:4]
    ya = _from_row_tiles(ybuf[slot, 0]).astype(F32)
    yb = _from_row_tiles(ybuf[slot, 1]).astype(F32)
    h = h_ref[...] + w1 * ya + w2 * yb
    ms = jnp.mean(h * h, axis=-1, keepdims=True)
    o_ref[...] = h * lax.rsqrt(ms + EPS) * g_ref[...]


def _combine(h, route, dest, y_rows, g, tm):
    t, d = h.shape
    nt = t // tm
    dest_spec = lambda f: pl.BlockSpec((None, 2, tm), f, memory_space=pltpu.SMEM)
    return pl.pallas_call(
        _combine_kernel,
        grid=(nt,),
        in_specs=[
            dest_spec(lambda i: (i, 0, 0)),
            dest_spec(lambda i: (jnp.minimum(i + 1, nt - 1), 0, 0)),
            pl.BlockSpec((tm, d), lambda i: (i, 0)),
            pl.BlockSpec((tm, ROUTE_LANES), lambda i: (i, 0)),
            pl.BlockSpec((1, d), lambda i: (0, 0)),
            pl.BlockSpec(memory_space=pl.ANY),
        ],
        out_specs=pl.BlockSpec((tm, d), lambda i: (i, 0)),
        out_shape=jax.ShapeDtypeStruct((t, d), F32),
        scratch_shapes=[pltpu.VMEM((2, 2, tm, d // LANES, LANES), BF16),
                        pltpu.SemaphoreType.DMA((2,))],
        compiler_params=_cparams(("arbitrary",)),
        name="moe_combine",
    )(dest, dest, h, route, g, y_rows)


def kernel(x, w_in, attn_sinks, hgrn_lb_logits, hgrn_norm_g, w_br_attn, w_br_hgrn, w_out,
           mix_norm_g, moe_norm_g, w_router_group, b_router_group, w_router_expert,
           b_router_expert, w1, w3, w2, final_norm_g):
    b, s, d = x.shape
    t = b * s
    depth = w_in.shape[0]
    assert depth == 1
    attn_w = (d // 128) * ATTN_HEAD_DIM
    kv_w = attn_w // Q_PER_KV
    hg_w = (d // 256) * HGRN_HEAD_DIM
    assert kv_w == N_KV_HEADS * ATTN_HEAD_DIM

    src = {}
    off = 0
    for name, width in (("qa", attn_w), ("k", kv_w), ("v", kv_w), ("qh", hg_w), ("f", hg_w),
                        ("i", hg_w), ("og", hg_w), ("g0", d), ("g1", d)):
        src[name] = (off, width)
        off += width
    order = ("g0", "g1", "qa", "qh", "f", "i", "og", "k", "v")
    dst = {}
    off = 0
    for name in order:
        dst[name] = off
        off += src[name][1]
    n_in = off
    l = 0
    tn_in = 512
    assert all(src[n][0] % tn_in == 0 for n in order if n != "v") and (2 * kv_w) % tn_in == 0
    col_blocks = []
    for n in order[:-1]:
        width = src[n][1] if n != "k" else 2 * kv_w
        col_blocks += [src[n][0] // tn_in + c for c in range(width // tn_in)]
    col_blocks = jnp.asarray(col_blocks, jnp.int32)
    w_bf = w_in[l]

    lb = jnp.cumsum(jax.nn.softmax(hgrn_lb_logits.astype(F32), axis=0), axis=0)[l].reshape(1, hg_w)

    x2 = x.reshape(t, d)
    tm_in = min(2048, t)
    proj = _in_proj(x2, mix_norm_g[l].reshape(1, d), w_bf, col_blocks, tm_in, tn_in)
    proj3 = proj.reshape(b, s, n_in)

    o_h = _hgrn(proj3, lb, hgrn_norm_g[l].reshape(1, HGRN_HEAD_DIM).astype(F32),
                dst["qh"], dst["f"], dst["i"], dst["og"], hg_w, min(256, s))

    w_r = jnp.zeros((d, ROUTE_LANES), F32)
    w_r = w_r.at[:, :N_GROUPS].set(w_router_group[l])
    w_r = w_r.at[:, EXPERT_LANE0:EXPERT_LANE0 + N_EXPERTS].set(w_router_expert[l])
    wr_hi = w_r.astype(BF16)
    wr_lo = (w_r - wr_hi.astype(F32)).astype(BF16)
    b_r = jnp.zeros((1, ROUTE_LANES), F32)
    b_r = b_r.at[0, :N_GROUPS].set(b_router_group[l])
    b_r = b_r.at[0, EXPERT_LANE0:EXPERT_LANE0 + N_EXPERTS].set(b_router_expert[l])

    tm = 2 * ATTN_BLOCK
    h, hn, route, plan, cnt = _attn_merge_route(
        attn_sinks[l].astype(F32), o_h.reshape(t, hg_w), proj, x2,
        w_br_attn[l], w_br_hgrn[l], w_out[l],
        moe_norm_g[l].reshape(1, d), wr_hi, wr_lo, b_r, dst, attn_w, kv_w, s, tm)

    bm = MOE_ROWS
    counts = cnt[0, EXPERT_LANE0:EXPERT_LANE0 + N_EXPERTS].astype(jnp.int32)
    padded = ((counts + bm - 1) // bm) * bm
    pad_end = jnp.cumsum(padded)
    pad_start = (pad_end - padded).astype(jnp.int32)
    n_blocks = (2 * t) // bm + N_EXPERTS
    n_used = pad_end[-1] // bm
    blk_ids = jnp.minimum(jnp.arange(n_blocks, dtype=jnp.int32), n_used - 1)
    block_expert = jnp.minimum(
        jnp.sum((pad_end[None, :] <= (blk_ids * bm)[:, None]).astype(jnp.int32), axis=1),
        N_EXPERTS - 1)
    meta = jnp.stack([n_used, n_used]).astype(jnp.int32)

    blk_all = jnp.arange(n_blocks, dtype=jnp.int32)
    fill = jnp.concatenate([
        jnp.maximum(pad_end - bm, 0).astype(jnp.int32),
        jnp.minimum(n_used + blk_all, n_blocks - 1) * bm])
    nfill = (N_EXPERTS + n_blocks - n_used).astype(jnp.int32).reshape(1)

    eid = jnp.arange(N_EXPERTS, dtype=jnp.int32)
    nonempty = counts > 0
    slot_e = ((jnp.cumsum(nonempty.astype(jnp.int32)) - 1) % 2).astype(jnp.int32)
    later = jnp.logical_and(nonempty[None, :], eid[None, :] > eid[:, None])
    nxt_e = jnp.min(jnp.where(later, eid[None, :], N_EXPERTS), axis=1)
    nxt_e = jnp.where(nxt_e == N_EXPERTS, -1, nxt_e).astype(jnp.int32)

    nt = t // tm
    dest = _dest_rows(plan, pad_start, min(8, nt))
    rows = _dispatch(hn, dest, fill, nfill, n_blocks * bm, min(4, nt), bm)
    y_rows = _experts(rows, block_expert, meta, slot_e, nxt_e, w1[l], w3[l], w2[l], bm)
    out = _combine(h, route, dest, y_rows, final_norm_g.reshape(1, d), tm)
    return out.reshape(b, s, d)
```

```python
import functools

import jax
import jax.numpy as jnp
from jax import lax
from jax.experimental import pallas as pl
from jax.experimental.pallas import tpu as pltpu

F32 = jnp.float32
BF16 = jnp.bfloat16

EPS = 1e-6
ATTN_HEAD_DIM = 64
Q_PER_KV = 4
N_KV_HEADS = 4
ATTN_BLOCK = 128
HGRN_HEAD_DIM = 128
HGRN_CHUNK = 64
N_GROUPS = 4
EXPERTS_PER_GROUP = 8
N_EXPERTS = N_GROUPS * EXPERTS_PER_GROUP
LANES = 128
ROUTE_LANES = LANES
EXPERT_LANE0 = N_GROUPS
MOE_ROWS = 256
WEIGHT_STAGE_ROWS = 256
PLAN_ROWS = 8
PLAN_E, PLAN_R = 0, 4
VMEM_LIMIT = 56 * 1024 * 1024
VMEM_LIMIT_IN_PROJ = 60 * 1024 * 1024


def _sigmoid(x):
    return 0.5 * jnp.tanh(0.5 * x) + 0.5


def _cparams(sem, vmem=VMEM_LIMIT):
    return pltpu.CompilerParams(dimension_semantics=sem, vmem_limit_bytes=vmem)


def _to_row_tiles(v):
    m, d = v.shape
    return v.reshape(m, d // LANES, LANES).astype(BF16)


def _from_row_tiles(p):
    m, c, l = p.shape
    return p.reshape(m, c * l)


def _inproj_kernel(cols_ref, x_ref, g_ref, w_ref, o_ref, xn_ref):
    del cols_ref

    @pl.when(pl.program_id(1) == 0)
    def _():
        x = x_ref[...]
        ms = jnp.mean(x * x, axis=-1, keepdims=True)
        xn_ref[...] = (x * lax.rsqrt(ms + EPS) * g_ref[...]).astype(BF16)

    o_ref[...] = jnp.dot(xn_ref[...], w_ref[...].astype(BF16),
                         preferred_element_type=F32).astype(o_ref.dtype)


def _in_proj(x2, g, w, col_blocks, tm, tn):
    t, d = x2.shape
    n = w.shape[1]
    grid_spec = pltpu.PrefetchScalarGridSpec(
        num_scalar_prefetch=1,
        grid=(t // tm, n // tn),
        in_specs=[
            pl.BlockSpec((tm, d), lambda i, j, cols: (i, 0)),
            pl.BlockSpec((1, d), lambda i, j, cols: (0, 0)),
            pl.BlockSpec((d, tn), lambda i, j, cols: (0, cols[j])),
        ],
        out_specs=pl.BlockSpec((tm, tn), lambda i, j, cols: (i, j)),
        scratch_shapes=[pltpu.VMEM((tm, d), BF16)],
    )
    return pl.pallas_call(
        _inproj_kernel,
        grid_spec=grid_spec,
        out_shape=jax.ShapeDtypeStruct((t, n), BF16),
        compiler_params=_cparams(("arbitrary", "arbitrary"), VMEM_LIMIT_IN_PROJ),
        name="in_proj",
    )(col_blocks, x2, g, w)


def _attn_scores(q_ref, rows, k_cur, k_prev):
    dh, g = ATTN_HEAD_DIM, Q_PER_KV
    nt = (((1,), (1,)), ((), ()))
    scores = []
    for h in range(N_KV_HEADS):
        q4 = jnp.concatenate(
            [q_ref[rows, (h * g + j) * dh:(h * g + j + 1) * dh] for j in range(g)], axis=0)
        scores.append(
            (lax.dot_general(q4, k_cur[:, h * dh:(h + 1) * dh], nt, preferred_element_type=F32),
             lax.dot_general(q4, k_prev[:, h * dh:(h + 1) * dh], nt, preferred_element_type=F32)))
    return scores


def _attn_finish(scores, v_cur, v_prev, sink_ref, prev_bias, o_ref, rows):
    blk, dh, g = ATTN_BLOCK, ATTN_HEAD_DIM, Q_PER_KV
    n_rows = g * blk
    qi = lax.broadcasted_iota(jnp.int32, (n_rows, blk), 0) % blk
    kj = lax.broadcasted_iota(jnp.int32, (n_rows, blk), 1)
    mask_cur = kj <= qi
    head_of_row = lax.broadcasted_iota(jnp.int32, (n_rows, 1), 0) // blk
    scale = dh ** -0.5
    for h in range(N_KV_HEADS):
        sc, sp = scores[h]
        s = jnp.where(mask_cur, sc, sp + prev_bias) * scale
        sink = jnp.zeros((n_rows, 1), F32)
        for j in range(g):
            sink = jnp.where(head_of_row == j, sink_ref[h * g + j], sink)
        m = jnp.maximum(jnp.max(s, axis=-1, keepdims=True), sink)
        p = jnp.exp(s - m)
        den = jnp.sum(p, axis=-1, keepdims=True) + jnp.exp(sink - m)
        p_cur = jnp.where(mask_cur, p, 0.0).astype(BF16)
        p_prev = jnp.where(mask_cur, 0.0, p).astype(BF16)
        o = (jnp.dot(p_cur, v_cur[:, h * dh:(h + 1) * dh], preferred_element_type=F32)
             + jnp.dot(p_prev, v_prev[:, h * dh:(h + 1) * dh], preferred_element_type=F32)) / den
        for j in range(g):
            o_ref[rows, (h * g + j) * dh:(h * g + j + 1) * dh] = (
                o[j * blk:(j + 1) * blk, :].astype(o_ref.dtype))


def _hgrn_kernel(q_ref, f_ref, i_ref, og_ref, lb_ref, gn_ref, o_ref,
                 st_ref, gh_s, f_s, b_s, qt_s, kt_s, qe_s, kd_s, dl_s, oi_s, u_s, a_s,
                 *, n_heads, n_chunks):
    c, dk = HGRN_CHUNK, HGRN_HEAD_DIM
    w = n_heads * dk

    @pl.when(pl.program_id(1) == 0)
    def _():
        st_ref[...] = jnp.zeros_like(st_ref)

    ti = lax.broadcasted_iota(jnp.int32, (c, c), 0)
    si = lax.broadcasted_iota(jnp.int32, (c, c), 1)
    causal = si <= ti
    tri = causal.astype(BF16)
    nt = (((1,), (1,)), ((), ()))
    tn = (((0,), (0,)), ((), ()))
    qscale = dk ** -0.5
    chunk_rows = [slice(ci * c, (ci + 1) * c) for ci in range(n_chunks)]
    head_cols = [slice(h * dk, (h + 1) * dk) for h in range(n_heads)]

    lb = lb_ref[...]
    c0 = 0.5 * (1.0 + lb)
    c1 = 0.5 * (1.0 - lb)
    tf = jnp.tanh(0.5 * f_ref[...].astype(F32))
    f_s[...] = c1 * (1.0 - tf)
    gl = jnp.log(c0 + c1 * tf)
    g_hi = gl.astype(BF16)
    gh_s[:, :w] = g_hi
    gh_s[:, w:] = (gl - g_hi.astype(F32)).astype(BF16)

    for rows in chunk_rows:
        bcat = jnp.dot(tri, gh_s[rows, :], preferred_element_type=F32)
        b_s[rows, :] = bcat[:, :w] + bcat[:, w:]

    for ci, rows in enumerate(chunk_rows):
        bc = b_s[rows, :]
        b_last = bc[c - 1:c, :]
        r = 0.5 * b_last
        qv = q_ref[rows, :].astype(F32)
        qt = (qv * (0.5 * qscale)) * (1.0 + jnp.tanh(0.5 * qv)) * jnp.exp(bc - r)
        kt = f_s[rows, :] * jnp.exp(r - bc)
        qt_s[rows, :] = qt.astype(BF16)
        kt_s[rows, :] = kt.astype(BF16)
        qe_s[rows, :] = (qt * jnp.exp(r)).astype(BF16)
        kd_s[rows, :] = (kt * jnp.exp(b_last - r)).astype(BF16)
        dl_s[ci:ci + 1, :] = jnp.exp(b_last)

    units = [(ci, rows, h, cols) for ci, rows in enumerate(chunk_rows)
             for h, cols in enumerate(head_cols)]
    for ci, rows, h, cols in units:
        a = lax.dot_general(qt_s[rows, cols], kt_s[rows, cols], nt, preferred_element_type=F32)
        a_s[ci * n_heads + h] = jnp.where(causal, a, 0.0).astype(BF16)
    for ci, rows, h, cols in units:
        u_s[ci, h] = lax.dot_general(
            i_ref[rows, cols], kd_s[rows, cols], tn, preferred_element_type=F32)
    for ci, rows, h, cols in units:
        oi_s[rows, cols] = jnp.dot(
            a_s[ci * n_heads + h], i_ref[rows, cols], preferred_element_type=F32)

    gn_half = 0.5 * gn_ref[...]
    for ci, rows in enumerate(chunk_rows):
        for h, cols in enumerate(head_cols):
            st = st_ref[h]
            o = oi_s[rows, cols] + lax.dot_general(
                qe_s[rows, cols], st.astype(BF16), nt, preferred_element_type=F32)
            st_ref[h] = st * dl_s[ci:ci + 1, cols] + u_s[ci, h]
            ms = jnp.mean(o * o, axis=-1, keepdims=True)
            ogv = og_ref[rows, cols].astype(F32)
            o_ref[rows, cols] = (
                (o * lax.rsqrt(ms + EPS) * gn_half)
                * (ogv * (1.0 + jnp.tanh(0.5 * ogv)))).astype(o_ref.dtype)


def _hgrn(proj3, lb, gn, off_q, off_f, off_i, off_og, width, ts):
    b, s, _ = proj3.shape
    n_heads = width // HGRN_HEAD_DIM
    n_chunks = ts // HGRN_CHUNK
    dk = HGRN_HEAD_DIM
    spec = lambda off: pl.BlockSpec((None, ts, width), lambda i, t: (i, t, off // width))
    return pl.pallas_call(
        functools.partial(_hgrn_kernel, n_heads=n_heads, n_chunks=n_chunks),
        grid=(b, s // ts),
        in_specs=[
            spec(off_q), spec(off_f), spec(off_i), spec(off_og),
            pl.BlockSpec((1, width), lambda i, t: (0, 0)),
            pl.BlockSpec((1, dk), lambda i, t: (0, 0)),
        ],
        out_specs=pl.BlockSpec((None, ts, width), lambda i, t: (i, t, 0)),
        out_shape=jax.ShapeDtypeStruct((b, s, width), BF16),
        scratch_shapes=[
            pltpu.VMEM((n_heads, dk, dk), F32),
            pltpu.VMEM((ts, 2 * width), BF16),
            pltpu.VMEM((ts, width), F32),
            pltpu.VMEM((ts, width), F32),
            pltpu.VMEM((ts, width), BF16),
            pltpu.VMEM((ts, width), BF16),
            pltpu.VMEM((ts, width), BF16),
            pltpu.VMEM((ts, width), BF16),
            pltpu.VMEM((max(n_chunks, 8), width), F32),
            pltpu.VMEM((ts, width), F32),
            pltpu.VMEM((n_chunks, n_heads, dk, dk), F32),
            pltpu.VMEM((n_chunks * n_heads, HGRN_CHUNK, HGRN_CHUNK), BF16),
        ],
        compiler_params=_cparams(("arbitrary", "arbitrary")),
        name="hgrn2",
    )(proj3, proj3, proj3, proj3, lb, gn)


def _merge_kernel(sink_ref, q_ref, kc_ref, kp_ref, vc_ref, vp_ref,
                  oh_ref, g0_ref, g1_ref, x_ref, wa_hbm, wh_hbm, wo_hbm, gm_ref,
                  wrh_ref, wrl_ref, br_ref, h_ref, hn_ref, route_ref, plan_ref, cnt_ref,
                  oa_s, hn_s, wa_ref, wh_ref, wo_ref, stage, stage_sem, sc_s, *, tiles_per_seq):
    tm, d = x_ref.shape
    half = d // 2
    halves = (slice(0, half), slice(half, d))
    step = pl.program_id(0)
    blk = ATTN_BLOCK
    assert tm == 2 * blk

    @pl.when(step == 0)
    def _():
        cnt_ref[...] = jnp.zeros_like(cnt_ref)
        oa_s[...] = jnp.zeros_like(oa_s)
        hn_s[...] = jnp.zeros_like(hn_s)
        sc_s[...] = jnp.zeros_like(sc_s)
        rows = stage.shape[1]
        chunks = [(src, dst, r0) for src, dst in ((wa_hbm, wa_ref), (wh_hbm, wh_ref), (wo_hbm, wo_ref))
                  for r0 in range(0, src.shape[0], rows)]

        def chunk_copy(n):
            src, _, r0 = chunks[n]
            return pltpu.make_async_copy(
                src.at[pl.ds(r0, rows)], stage.at[n % 2], stage_sem.at[n % 2])

        chunk_copy(0).start()
        for n, (_, dst, r0) in enumerate(chunks):
            if n + 1 < len(chunks):
                chunk_copy(n + 1).start()
            chunk_copy(n).wait()
            dst[r0:r0 + rows, :] = stage[n % 2].astype(BF16)

    q_blocks = (slice(0, blk), slice(blk, tm))
    v_blocks = [vc_ref[rs, :] for rs in q_blocks]
    scores = [[(sc_s[(b * N_KV_HEADS + h) * 2], sc_s[(b * N_KV_HEADS + h) * 2 + 1])
               for h in range(N_KV_HEADS)] for b in range(2)]
    attn_tile = jnp.clip(step - 1, 0, pl.num_programs(0) - 4)
    prev_bias = jnp.where(attn_tile % tiles_per_seq == 0, -jnp.inf, 0.0)

    hn_prev = hn_s[...]
    hn_hi = hn_prev.astype(BF16)
    hn_lo = (hn_prev - hn_hi.astype(F32)).astype(BF16)

    oa = oa_s[...]
    ya = [jnp.dot(oa, wa_ref[:, cs], preferred_element_type=F32) for cs in halves]
    yh = [jnp.dot(oh_ref[...], wh_ref[:, cs], preferred_element_type=F32) for cs in halves]

    logits = (jnp.dot(hn_hi, wrh_ref[...], preferred_element_type=F32)
              + jnp.dot(hn_hi, wrl_ref[...], preferred_element_type=F32)
              + jnp.dot(hn_lo, wrh_ref[...], preferred_element_type=F32)
              + br_ref[...])

    _attn_finish(scores[0], v_blocks[0], vp_ref[...], sink_ref, prev_bias, oa_s, q_blocks[0])
    _attn_finish(scores[1], v_blocks[1], v_blocks[0], sink_ref, 0.0, oa_s, q_blocks[1])

    lane = lax.broadcasted_iota(jnp.int32, (tm, ROUTE_LANES), 1)
    neg = -jnp.inf

    def first_argmax(v):
        m = jnp.max(v, axis=-1, keepdims=True)
        idx = jnp.min(jnp.where(v == m, lane, ROUTE_LANES), axis=-1, keepdims=True)
        return m, idx

    is_group = lane < N_GROUPS
    gmax, gidx = first_argmax(jnp.where(is_group, logits, neg))
    p_sel = 1.0 / jnp.sum(jnp.where(is_group, jnp.exp(logits - gmax), 0.0), axis=-1, keepdims=True)
    eidx = lane - EXPERT_LANE0
    in_group = jnp.logical_and(
        jnp.logical_and(eidx >= 0, eidx < N_EXPERTS), (eidx // EXPERTS_PER_GROUP) == gidx)
    el = jnp.where(in_group, logits, neg)
    m1, i1 = first_argmax(el)
    m2, i2 = first_argmax(jnp.where(lane == i1, neg, el))
    t = jnp.exp(m2 - m1)
    w1 = p_sel / (1.0 + t)
    w2 = p_sel * t / (1.0 + t)

    sel1 = lane == i1
    sel2 = lane == i2
    onehot = jnp.logical_or(sel1, sel2).astype(BF16)
    ri = lax.broadcasted_iota(jnp.int32, (tm, tm), 0)
    ci = lax.broadcasted_iota(jnp.int32, (tm, tm), 1)
    before = (ci < ri).astype(BF16)

    h = x_ref[...]
    for k, cs in enumerate(halves):
        merged = (_sigmoid(g0_ref[:, cs].astype(F32)) * ya[k]
                  + _sigmoid(g1_ref[:, cs].astype(F32)) * yh[k]).astype(BF16)
        h = h + jnp.dot(merged, wo_ref[cs, :], preferred_element_type=F32)
        if k == 0:
            cum = jnp.dot(before, onehot, preferred_element_type=F32) + cnt_ref[...]


    r1 = jnp.sum(jnp.where(sel1, cum, 0.0), axis=-1, keepdims=True)
    r2 = jnp.sum(jnp.where(sel2, cum, 0.0), axis=-1, keepdims=True)
    routed = jnp.where(step > 2, 1.0, 0.0)
    cnt_ref[...] += routed * jnp.sum(onehot.astype(F32), axis=0, keepdims=True)

    e1 = (i1 - EXPERT_LANE0).astype(F32)
    e2 = (i2 - EXPERT_LANE0).astype(F32)
    out = jnp.zeros((tm, ROUTE_LANES), F32)
    for k, v in enumerate((e1, e2, w1, w2, r1, r2)):
        out = jnp.where(lane == k, v, out)
    route_ref[...] = out
    plan_ref[...] = out.T[:PLAN_ROWS, :].astype(jnp.int32)

    h_ref[...] = h
    ms = jnp.mean(h * h, axis=-1, keepdims=True)
    hn = h * lax.rsqrt(ms + EPS) * gm_ref[...]
    hn_ref[...] = _to_row_tiles(hn)
    hn_s[...] = hn

    k_blocks = [kc_ref[rs, :] for rs in q_blocks]
    new_scores = [_attn_scores(q_ref, q_blocks[0], k_blocks[0], kp_ref[...]),
                  _attn_scores(q_ref, q_blocks[1], k_blocks[1], k_blocks[0])]
    for b in range(2):
        for h in range(N_KV_HEADS):
            for c in range(2):
                sc_s[(b * N_KV_HEADS + h) * 2 + c] = new_scores[b][h][c]


def _attn_merge_route(sinks, o_h, proj, x2, wa, wh, wo, gm, wr_hi, wr_lo, br,
                      off, attn_w, kv_w, seq, tm):
    t, d = x2.shape
    nt = t // tm
    hw = o_h.shape[1]
    blk = ATTN_BLOCK
    const = lambda shape: pl.BlockSpec(shape, lambda i: (0, 0), pipeline_mode=pl.Buffered(1))
    hbm = pl.BlockSpec(memory_space=pl.ANY)
    assert attn_w % WEIGHT_STAGE_ROWS == 0 and hw % WEIGHT_STAGE_ROWS == 0
    stage_tile = lambda lag: (lambda i: jnp.clip(i - lag, 0, nt - 1))
    sco = stage_tile(0)
    att = stage_tile(1)
    cur = stage_tile(2)
    prv = stage_tile(3)
    row = lambda w: pl.BlockSpec((tm, w), lambda i: (cur(i), 0))
    kv_col = {"k": off["k"] // kv_w, "v": off["v"] // kv_w}
    kv_cur = lambda n, tile: pl.BlockSpec((tm, kv_w), lambda i: (tile(i), kv_col[n]))
    kv_prev = lambda n, tile: pl.BlockSpec(
        (blk, kv_w), lambda i: (jnp.maximum(tile(i) * (tm // blk) - 1, 0), kv_col[n]))
    return pl.pallas_call(
        functools.partial(_merge_kernel, tiles_per_seq=seq // tm),
        grid=(nt + 3,),
        in_specs=[
            pl.BlockSpec(memory_space=pltpu.SMEM),
            pl.BlockSpec((tm, attn_w), lambda i: (sco(i), off["qa"] // attn_w)),
            kv_cur("k", sco), kv_prev("k", sco), kv_cur("v", att), kv_prev("v", att),
            row(hw),
            pl.BlockSpec((tm, d), lambda i: (cur(i), off["g0"] // d)),
            pl.BlockSpec((tm, d), lambda i: (cur(i), off["g1"] // d)),
            row(d),
            hbm, hbm, hbm, const((1, d)),
            const((d, ROUTE_LANES)), const((d, ROUTE_LANES)), const((1, ROUTE_LANES)),
        ],
        out_specs=[row(d), pl.BlockSpec((tm, d // LANES, LANES), lambda i: (cur(i), 0, 0)),
                   pl.BlockSpec((tm, ROUTE_LANES), lambda i: (prv(i), 0)),
                   pl.BlockSpec((None, PLAN_ROWS, tm), lambda i: (prv(i), 0, 0)),
                   pl.BlockSpec((1, ROUTE_LANES), lambda i: (0, 0))],
        out_shape=[
            jax.ShapeDtypeStruct((t, d), F32),
            jax.ShapeDtypeStruct((t, d // LANES, LANES), BF16),
            jax.ShapeDtypeStruct((t, ROUTE_LANES), F32),
            jax.ShapeDtypeStruct((nt, PLAN_ROWS, tm), jnp.int32),
            jax.ShapeDtypeStruct((1, ROUTE_LANES), F32),
        ],
        scratch_shapes=[
            pltpu.VMEM((tm, attn_w), BF16), pltpu.VMEM((tm, d), F32),
            pltpu.VMEM((attn_w, d), BF16), pltpu.VMEM((hw, d), BF16), pltpu.VMEM((d, d), BF16),
            pltpu.VMEM((2, WEIGHT_STAGE_ROWS, d), F32), pltpu.SemaphoreType.DMA((2,)),
            pltpu.VMEM((2 * N_KV_HEADS * 2, Q_PER_KV * blk, blk), F32),
        ],
        compiler_params=_cparams(("arbitrary",)),
        name="attn_merge_route",
    )(sinks, proj, proj, proj, proj, proj, o_h, proj, proj, x2, wa, wh, wo, gm, wr_hi, wr_lo, br)


def _dest_kernel(pst_ref, plan_ref, dest_ref):
    e = plan_ref[:, PLAN_E:PLAN_E + 2, :]
    start = jnp.zeros(e.shape, jnp.int32)
    for j in range(N_EXPERTS):
        start = jnp.where(e == j, pst_ref[j], start)
    dest_ref[...] = start + plan_ref[:, PLAN_R:PLAN_R + 2, :]


def _dest_rows(plan, pst, tiles):
    nt, _, tm = plan.shape
    grid_spec = pltpu.PrefetchScalarGridSpec(
        num_scalar_prefetch=1,
        grid=(nt // tiles,),
        in_specs=[pl.BlockSpec((tiles, PLAN_ROWS, tm), lambda i, *_: (i, 0, 0))],
        out_specs=pl.BlockSpec((tiles, 2, tm), lambda i, *_: (i, 0, 0)),
    )
    return pl.pallas_call(
        _dest_kernel,
        grid_spec=grid_spec,
        out_shape=jax.ShapeDtypeStruct((nt, 2, tm), jnp.int32),
        compiler_params=_cparams(("arbitrary",)),
        name="moe_dest",
    )(pst, plan)


ISSUE_UNROLL = 8


def _dispatch_kernel(fill_ref, nfill_ref, dest_ref, hn_ref, rows_ref, zbuf, sem, tail_sem):
    tm = hn_ref.shape[0]
    bm = zbuf.shape[0]

    def zero_block(j, s):
        start = pl.multiple_of(fill_ref[j], bm)
        return pltpu.make_async_copy(zbuf, rows_ref.at[pl.ds(start, bm)], s)

    def started(s):
        return lambda j, carry: (zero_block(j, s).start(), carry)[1]

    def waited(s):
        return lambda j, carry: (zero_block(j, s).wait(), carry)[1]

    @pl.when(pl.program_id(0) == 0)
    def _():
        zbuf[...] = jnp.zeros_like(zbuf)
        lax.fori_loop(0, N_EXPERTS, started(sem), 0)
        lax.fori_loop(N_EXPERTS, nfill_ref[0], started(tail_sem), 0)
        lax.fori_loop(0, N_EXPERTS, waited(sem), 0)

    sub, _, tmd = dest_ref.shape

    for j in range(sub):
        def issue(r, carry, j=j):
            for k in range(2):
                pltpu.make_async_copy(
                    hn_ref.at[j * tmd + r], rows_ref.at[dest_ref[j, k, r]], sem
                ).start(priority=k)
            return carry

        lax.fori_loop(0, tmd, issue, 0, unroll=ISSUE_UNROLL)
    for _ in range(2):
        pltpu.make_async_copy(hn_ref, rows_ref.at[pl.ds(0, tm)], sem).wait()

    @pl.when(pl.program_id(0) == pl.num_programs(0) - 1)
    def _():
        lax.fori_loop(N_EXPERTS, nfill_ref[0], waited(tail_sem), 0)


def _dispatch(hn, dest, fill, nfill, n_rows, sub, bm):
    t, c, l = hn.shape
    nt, _, tmd = dest.shape
    tm = sub * tmd
    grid_spec = pltpu.PrefetchScalarGridSpec(
        num_scalar_prefetch=2,
        grid=(nt // sub,),
        in_specs=[
            pl.BlockSpec((sub, 2, tmd), lambda i, *_: (i, 0, 0), memory_space=pltpu.SMEM),
            pl.BlockSpec((tm, c, l), lambda i, *_: (i, 0, 0)),
        ],
        out_specs=pl.BlockSpec(memory_space=pl.ANY),
        scratch_shapes=[pltpu.VMEM((bm, c, l), hn.dtype), pltpu.SemaphoreType.DMA,
                        pltpu.SemaphoreType.DMA],
    )
    return pl.pallas_call(
        _dispatch_kernel,
        grid_spec=grid_spec,
        out_shape=jax.ShapeDtypeStruct((n_rows, c, l), hn.dtype),
        compiler_params=_cparams(("arbitrary",)),
        name="moe_dispatch",
    )(fill, nfill, dest, hn)


def _expert_kernel(be_ref, meta_ref, slot_ref, nxt_ref, x_ref, w1_hbm, w3_hbm, w2_hbm, y_ref,
                   wf1, wf3, wf2, w1b, w3b, w2b, sems):
    i = pl.program_id(0)
    e = be_ref[i]
    new_expert = jnp.logical_or(i == 0, e != be_ref[jnp.maximum(i - 1, 0)])

    def weight_copies(expert, slot):
        return [pltpu.make_async_copy(src.at[expert], dst.at[slot], sems.at[slot])
                for src, dst in ((w1_hbm, wf1), (w3_hbm, wf3), (w2_hbm, wf2))]

    @pl.when(jnp.logical_and(i < meta_ref[0], new_expert))
    def _():
        slot = slot_ref[e]
        nxt = nxt_ref[e]

        @pl.when(i == 0)
        def _():
            for c in weight_copies(e, slot):
                c.start()

        @pl.when(nxt >= 0)
        def _():
            for c in weight_copies(nxt, 1 - slot):
                c.start(priority=1)

        for c in weight_copies(e, slot):
            c.wait()
        w1b[...] = wf1[slot].astype(BF16)
        w3b[...] = wf3[slot].astype(BF16)
        w2b[...] = wf2[slot].astype(BF16)

    @pl.when(i < meta_ref[0])
    def _():
        bm = x_ref.shape[0]
        parts = (slice(0, bm // 2), slice(bm // 2, bm))
        xs = [_from_row_tiles(x_ref[rs]) for rs in parts]
        ups = [(jnp.dot(x, w1b[...], preferred_element_type=F32),
                jnp.dot(x, w3b[...], preferred_element_type=F32)) for x in xs]
        for rs, (h1, h3) in zip(parts, ups):
            hb = (h1 * _sigmoid(h1) * h3).astype(BF16)
            y_ref[rs] = _to_row_tiles(jnp.dot(hb, w2b[...], preferred_element_type=F32))

    @pl.when(i >= meta_ref[0])
    def _():
        y_ref[...] = jnp.zeros_like(y_ref)


def _experts(rows, block_expert, meta, slot_e, nxt_e, w1, w3, w2, bm):
    n_rows, c, l = rows.shape
    d = c * l
    de = w1.shape[-1]
    n_blocks = n_rows // bm
    hbm = pl.BlockSpec(memory_space=pl.ANY)
    grid_spec = pltpu.PrefetchScalarGridSpec(
        num_scalar_prefetch=4,
        grid=(n_blocks,),
        in_specs=[
            pl.BlockSpec((bm, c, l), lambda i, be, meta, *_: (jnp.minimum(i, meta[0] - 1), 0, 0)),
            hbm, hbm, hbm,
        ],
        out_specs=pl.BlockSpec((bm, c, l), lambda i, *_: (i, 0, 0)),
        scratch_shapes=[
            pltpu.VMEM((2, d, de), w1.dtype), pltpu.VMEM((2, d, de), w3.dtype),
            pltpu.VMEM((2, de, d), w2.dtype),
            pltpu.VMEM((d, de), BF16), pltpu.VMEM((d, de), BF16), pltpu.VMEM((de, d), BF16),
            pltpu.SemaphoreType.DMA((2,)),
        ],
    )
    return pl.pallas_call(
        _expert_kernel,
        grid_spec=grid_spec,
        out_shape=jax.ShapeDtypeStruct((n_rows, c, l), BF16),
        compiler_params=_cparams(("arbitrary",)),
        name="moe_experts",
    )(block_expert, meta, slot_e, nxt_e, rows, w1, w3, w2)


def _combine_kernel(dcur_ref, dnext_ref, h_ref, route_ref, g_ref, y_ref, o_ref, ybuf, sems):
    i = pl.program_id(0)
    nt = pl.num_programs(0)
    tm, d = h_ref.shape
    slot = i % 2

    def gather(dest_ref, s):
        def issue(r, carry):
            for k in range(2):
                pltpu.make_async_copy(
                    y_ref.at[dest_ref[k, r]], ybuf.at[s, k, r], sems.at[s]).start(priority=k)
            return carry
        lax.fori_loop(0, tm, issue, 0, unroll=ISSUE_UNROLL)

    @pl.when(i == 0)
    def _():
        gather(dcur_ref, slot)

    @pl.when(i + 1 < nt)
    def _():
        gather(dnext_ref, 1 - slot)

    for k in range(2):
        pltpu.make_async_copy(y_ref.at[pl.ds(0, tm)], ybuf.at[slot, k], sems.at[slot]).wait()

    route = route_ref[...]
    w1 = route[:, 2:3]
    w2 = route[:, 3:4]
    ya = _from_row_tiles(ybuf[slot, 0]).astype(F32)
    yb = _from_row_tiles(ybuf[slot, 1]).astype(F32)
    h = h_ref[...] + w1 * ya + w2 * yb
    ms = jnp.mean(h * h, axis=-1, keepdims=True)
    o_ref[...] = h * lax.rsqrt(ms + EPS) * g_ref[...]


def _combine(h, route, dest, y_rows, g, tm):
    t, d = h.shape
    nt = t // tm
    dest_spec = lambda f: pl.BlockSpec((None, 2, tm), f, memory_space=pltpu.SMEM)
    return pl.pallas_call(
        _combine_kernel,
        grid=(nt,),
        in_specs=[
            dest_spec(lambda i: (i, 0, 0)),
            dest_spec(lambda i: (jnp.minimum(i + 1, nt - 1), 0, 0)),
            pl.BlockSpec((tm, d), lambda i: (i, 0)),
            pl.BlockSpec((tm, ROUTE_LANES), lambda i: (i, 0)),
            pl.BlockSpec((1, d), lambda i: (0, 0)),
            pl.BlockSpec(memory_space=pl.ANY),
        ],
        out_specs=pl.BlockSpec((tm, d), lambda i: (i, 0)),
        out_shape=jax.ShapeDtypeStruct((t, d), F32),
        scratch_shapes=[pltpu.VMEM((2, 2, tm, d // LANES, LANES), BF16),
                        pltpu.SemaphoreType.DMA((2,))],
        compiler_params=_cparams(("arbitrary",)),
        name="moe_combine",
    )(dest, dest, h, route, g, y_rows)


def kernel(x, w_in, attn_sinks, hgrn_lb_logits, hgrn_norm_g, w_br_attn, w_br_hgrn, w_out,
           mix_norm_g, moe_norm_g, w_router_group, b_router_group, w_router_expert,
           b_router_expert, w1, w3, w2, final_norm_g):
    b, s, d = x.shape
    t = b * s
    depth = w_in.shape[0]
    assert depth == 1
    attn_w = (d // 128) * ATTN_HEAD_DIM
    kv_w = attn_w // Q_PER_KV
    hg_w = (d // 256) * HGRN_HEAD_DIM
    assert kv_w == N_KV_HEADS * ATTN_HEAD_DIM

    src = {}
    off = 0
    for name, width in (("qa", attn_w), ("k", kv_w), ("v", kv_w), ("qh", hg_w), ("f", hg_w),
                        ("i", hg_w), ("og", hg_w), ("g0", d), ("g1", d)):
        src[name] = (off, width)
        off += width
    order = ("g0", "g1", "qa", "qh", "f", "i", "og", "k", "v")
    dst = {}
    off = 0
    for name in order:
        dst[name] = off
        off += src[name][1]
    n_in = off
    l = 0
    tn_in = 512
    assert all(src[n][0] % tn_in == 0 for n in order if n != "v") and (2 * kv_w) % tn_in == 0
    col_blocks = []
    for n in order[:-1]:
        width = src[n][1] if n != "k" else 2 * kv_w
        col_blocks += [src[n][0] // tn_in + c for c in range(width // tn_in)]
    col_blocks = jnp.asarray(col_blocks, jnp.int32)
    w_bf = w_in[l]

    lb = jnp.cumsum(jax.nn.softmax(hgrn_lb_logits.astype(F32), axis=0), axis=0)[l].reshape(1, hg_w)

    x2 = x.reshape(t, d)
    tm_in = min(2048, t)
    proj = _in_proj(x2, mix_norm_g[l].reshape(1, d), w_bf, col_blocks, tm_in, tn_in)
    proj3 = proj.reshape(b, s, n_in)

    o_h = _hgrn(proj3, lb, hgrn_norm_g[l].reshape(1, HGRN_HEAD_DIM).astype(F32),
                dst["qh"], dst["f"], dst["i"], dst["og"], hg_w, min(256, s))

    w_r = jnp.zeros((d, ROUTE_LANES), F32)
    w_r = w_r.at[:, :N_GROUPS].set(w_router_group[l])
    w_r = w_r.at[:, EXPERT_LANE0:EXPERT_LANE0 + N_EXPERTS].set(w_router_expert[l])
    wr_hi = w_r.astype(BF16)
    wr_lo = (w_r - wr_hi.astype(F32)).astype(BF16)
    b_r = jnp.zeros((1, ROUTE_LANES), F32)
    b_r = b_r.at[0, :N_GROUPS].set(b_router_group[l])
    b_r = b_r.at[0, EXPERT_LANE0:EXPERT_LANE0 + N_EXPERTS].set(b_router_expert[l])

    tm = 2 * ATTN_BLOCK
    h, hn, route, plan, cnt = _attn_merge_route(
        attn_sinks[l].astype(F32), o_h.reshape(t, hg_w), proj, x2,
        w_br_attn[l], w_br_hgrn[l], w_out[l],
        moe_norm_g[l].reshape(1, d), wr_hi, wr_lo, b_r, dst, attn_w, kv_w, s, tm)

    bm = MOE_ROWS
    counts = cnt[0, EXPERT_LANE0:EXPERT_LANE0 + N_EXPERTS].astype(jnp.int32)
    padded = ((counts + bm - 1) // bm) * bm
    pad_end = jnp.cumsum(padded)
    pad_start = (pad_end - padded).astype(jnp.int32)
    n_blocks = (2 * t) // bm + N_EXPERTS
    n_used = pad_end[-1] // bm
    blk_ids = jnp.minimum(jnp.arange(n_blocks, dtype=jnp.int32), n_used - 1)
    block_expert = jnp.minimum(
        jnp.sum((pad_end[None, :] <= (blk_ids * bm)[:, None]).astype(jnp.int32), axis=1),
        N_EXPERTS - 1)
    meta = jnp.stack([n_used, n_used]).astype(jnp.int32)

    blk_all = jnp.arange(n_blocks, dtype=jnp.int32)
    fill = jnp.concatenate([
        jnp.maximum(pad_end - bm, 0).astype(jnp.int32),
        jnp.minimum(n_used + blk_all, n_blocks - 1) * bm])
    nfill = (N_EXPERTS + n_blocks - n_used).astype(jnp.int32).reshape(1)

    eid = jnp.arange(N_EXPERTS, dtype=jnp.int32)
    nonempty = counts > 0
    slot_e = ((jnp.cumsum(nonempty.astype(jnp.int32)) - 1) % 2).astype(jnp.int32)
    later = jnp.logical_and(nonempty[None, :], eid[None, :] > eid[:, None])
    nxt_e = jnp.min(jnp.where(later, eid[None, :], N_EXPERTS), axis=1)
    nxt_e = jnp.where(nxt_e == N_EXPERTS, -1, nxt_e).astype(jnp.int32)

    nt = t // tm
    dest = _dest_rows(plan, pad_start, min(8, nt))
    rows = _dispatch(hn, dest, fill, nfill, n_blocks * bm, min(4, nt), bm)
    y_rows = _experts(rows, block_expert, meta, slot_e, nxt_e, w1[l], w3[l], w2[l], bm)
    out = _combine(h, route, dest, y_rows, final_norm_g.reshape(1, d), tm)
    return out.reshape(b, s, d)
```
